```python
import jax
import jax.numpy as jnp
from jax import lax
import numpy as np

D_MODEL = 1024
BATCH = 8
SEQ = 4096
DEPTH = 2

GRID_W = 64
CTX_LEN = 256
N_MIXERS = 2
N_CONV_LAYERS = (DEPTH + 1) // 2
N_ATTN_LAYERS = DEPTH // 2
CONV_CH = D_MODEL
CONV_WIDTH = 31
CONV_PAD = CONV_WIDTH // 2
HEAD_DIM = 64
N_HEADS = D_MODEL // HEAD_DIM
N_KV_HEADS = N_HEADS // 4
GROUP = N_HEADS // N_KV_HEADS
Q_DIM = N_HEADS * HEAD_DIM
KV_DIM = N_KV_HEADS * HEAD_DIM
WINDOW = 128
BLOCK = 128
ROPE_THETA = 10000.0
ROPE_PAIRS = HEAD_DIM // 4
N_EXPERTS = 16
N_GROUPS = 4
EXPERTS_PER_GROUP = N_EXPERTS // N_GROUPS
TOP_K = 2
D_EXPERT = D_MODEL // 2
EPS = 1e-6
NEG_INF = -1e30

kernel_name = "hybrid_conv_swa_moe_diffusion_block"


def _rmsnorm(x, g):
    xf = x.astype(jnp.float32)
    xf = xf * lax.rsqrt(jnp.mean(xf * xf, axis=-1, keepdims=True) + EPS)
    return xf.astype(x.dtype) * g


def _layernorm(x, g, b):
    xf = x.astype(jnp.float32)
    mu = jnp.mean(xf, axis=-1, keepdims=True)
    var = jnp.mean(jnp.square(xf - mu), axis=-1, keepdims=True)
    return ((xf - mu) * lax.rsqrt(var + EPS)).astype(x.dtype) * g + b


def _adaln(cond, w, b):
    mod = jax.nn.silu(cond) @ w + b
    parts = jnp.split(mod, 6, axis=-1)
    if cond.ndim == 2:
        parts = [p[:, None, :] for p in parts]
    return parts


def _modulate(h, shift, scale):
    return h * (1 + scale) + shift


def _axial_rope_tables(seq_len, dtype):
    rows = seq_len // GRID_W
    row = jnp.broadcast_to(jnp.arange(rows, dtype=jnp.int32)[:, None], (rows, GRID_W)).reshape(seq_len)
    col = jnp.broadcast_to(jnp.arange(GRID_W, dtype=jnp.int32)[None, :], (rows, GRID_W)).reshape(seq_len)
    inv_freq = jnp.power(jnp.float32(ROPE_THETA), -jnp.arange(ROPE_PAIRS, dtype=jnp.float32) / ROPE_PAIRS)
    ang_r = row.astype(jnp.float32)[:, None] * inv_freq
    ang_c = col.astype(jnp.float32)[:, None] * inv_freq
    return (jnp.cos(ang_r).astype(dtype), jnp.sin(ang_r).astype(dtype),
            jnp.cos(ang_c).astype(dtype), jnp.sin(ang_c).astype(dtype))


def _rotate(x, cos, sin):
    x1, x2 = x[..., :ROPE_PAIRS], x[..., ROPE_PAIRS:]
    return jnp.concatenate([x1 * cos - x2 * sin, x2 * cos + x1 * sin], axis=-1)


def _apply_axial_rope(x, rope):
    cos_r, sin_r, cos_c, sin_c = rope
    shape = (1, x.shape[1]) + (1,) * (x.ndim - 3) + (ROPE_PAIRS,)
    half = HEAD_DIM // 2
    return jnp.concatenate([
        _rotate(x[..., :half], cos_r.reshape(shape), sin_r.reshape(shape)),
        _rotate(x[..., half:], cos_c.reshape(shape), sin_c.reshape(shape)),
    ], axis=-1)


def _conformer_conv(h, w_pw1, b_pw1, w_dw, b_dw, ln_g, ln_b, w_pw2, b_pw2):
    u = h @ w_pw1 + b_pw1
    a, gate = jnp.split(u, 2, axis=-1)
    u = a * jax.nn.sigmoid(gate)
    u = lax.conv_general_dilated(
        u, w_dw[:, None, :], window_strides=(1,), padding=[(CONV_PAD, CONV_PAD)],
        dimension_numbers=("NWC", "WIO", "NWC"), feature_group_count=CONV_CH) + b_dw
    u = jax.nn.silu(_layernorm(u, ln_g, ln_b))
    return u @ w_pw2 + b_pw2


def _scores(q, k):
    return jnp.einsum("bqhgd,bkhd->bhgqk", q, k, preferred_element_type=jnp.float32)


def _sink_softmax(score_parts, sink):
    b, kvh, g, q = score_parts[0].shape[:4]
    col = jnp.broadcast_to(sink.astype(jnp.float32).reshape(1, kvh, g, 1, 1), (b, kvh, g, q, 1))
    p = jax.nn.softmax(jnp.concatenate(score_parts + [col], axis=-1), axis=-1)
    outs = []
    off = 0
    for sp in score_parts:
        n = sp.shape[-1]
        outs.append(p[..., off:off + n])
        off += n
    return outs


def _attend(probs, values):
    out = None
    for p, v in zip(probs, values):
        o = jnp.einsum("bhgqk,bkhd->bqhgd", p.astype(v.dtype), v)
        out = o if out is None else out + o
    return out.reshape(out.shape[0], out.shape[1], Q_DIM)


def _windowed_sink_gqa(h, hc, w_qkv, w_o, sink, rope, with_ctx_out):
    b, s, _ = h.shape
    lc = hc.shape[1]
    sink = sink.reshape(N_KV_HEADS, GROUP)
    qkv = h @ w_qkv
    q = qkv[..., :Q_DIM].reshape(b, s, N_KV_HEADS, GROUP, HEAD_DIM)
    k = qkv[..., Q_DIM:Q_DIM + KV_DIM].reshape(b, s, N_KV_HEADS, HEAD_DIM)
    v = qkv[..., Q_DIM + KV_DIM:].reshape(b, s, N_KV_HEADS, HEAD_DIM)
    q = _apply_axial_rope(q, rope) * (HEAD_DIM ** -0.5)
    k = _apply_axial_rope(k, rope)
    kvc = hc @ w_qkv[:, Q_DIM:]
    kc = kvc[..., :KV_DIM].reshape(b, lc, N_KV_HEADS, HEAD_DIM)
    vc = kvc[..., KV_DIM:].reshape(b, lc, N_KV_HEADS, HEAD_DIM)
    pad = ((0, 0), (WINDOW, WINDOW), (0, 0), (0, 0))
    kp = jnp.pad(k, pad)
    vp = jnp.pad(v, pad)
    span = BLOCK + 2 * WINDOW

    def block(i):
        start = i * BLOCK
        qb = lax.dynamic_slice_in_dim(q, start, BLOCK, axis=1)
        kb = lax.dynamic_slice_in_dim(kp, start, span, axis=1)
        vb = lax.dynamic_slice_in_dim(vp, start, span, axis=1)
        qpos = start + jnp.arange(BLOCK)
        kpos = start - WINDOW + jnp.arange(span)
        valid = ((jnp.abs(qpos[:, None] - kpos[None, :]) <= WINDOW)
                 & (kpos[None, :] >= 0) & (kpos[None, :] < s))
        s_band = jnp.where(valid, _scores(qb, kb), NEG_INF)
        s_ctx = _scores(qb, kc)
        p_band, p_ctx = _sink_softmax([s_band, s_ctx], sink)
        return _attend([p_band, p_ctx], [vb, vc])

    out = lax.map(block, jnp.arange(s // BLOCK))
    out = jnp.transpose(out, (1, 0, 2, 3)).reshape(b, s, Q_DIM)
    y = out @ w_o
    if not with_ctx_out:
        return y, None
    qc = (hc @ w_qkv[:, :Q_DIM]).reshape(b, lc, N_KV_HEADS, GROUP, HEAD_DIM) * (HEAD_DIM ** -0.5)
    (p_c,) = _sink_softmax([_scores(qc, kc)], sink)
    yc = _attend([p_c], [vc]) @ w_o
    return y, yc


def _route(h, router_w, router_b):
    t = h.shape[0]
    s = jax.nn.sigmoid(h.astype(jnp.float32) @ router_w.astype(jnp.float32))
    sel = s + router_b.astype(jnp.float32)
    sel_g = sel.reshape(t, N_GROUPS, EXPERTS_PER_GROUP)
    group_score = jnp.sum(lax.top_k(sel_g, TOP_K)[0], axis=-1)
    best_group = jnp.argmax(group_score, axis=-1)
    in_group = best_group[:, None] == jnp.arange(N_GROUPS)[None, :]
    sel_masked = jnp.where(in_group[:, :, None], sel_g, NEG_INF).reshape(t, N_EXPERTS)
    _, top_idx = lax.top_k(sel_masked, TOP_K)
    w = jnp.take_along_axis(s, top_idx, axis=-1)
    w = w / jnp.sum(w, axis=-1, keepdims=True)
    return jnp.sum(jax.nn.one_hot(top_idx, N_EXPERTS, dtype=jnp.float32) * w[..., None], axis=1)


def _grouped_moe(h, router_w, router_b, w_gate, w_up, w_down):
    gates = _route(h, router_w, router_b).astype(h.dtype)
    y = jnp.zeros_like(h)
    for e in range(N_EXPERTS):
        act = jax.nn.silu(h @ w_gate[e]) * (h @ w_up[e])
        y = y + gates[:, e:e + 1] * (act @ w_down[e])
    return y


def setup_inputs(seed: int = 0) -> dict:
    key = jax.random.key(seed)
    ks = jax.random.split(key, 25)

    def nrm(k, shape, scale):
        return jax.random.normal(k, shape, jnp.float32) * scale

    d, f = D_MODEL, D_EXPERT
    return {
        "x": nrm(ks[0], (BATCH, SEQ, d), 1.0),
        "c": nrm(ks[1], (BATCH, d), 1.0),
        "ctx": nrm(ks[2], (BATCH, CTX_LEN, d), 1.0),
        "c_ctx": nrm(ks[3], (d,), 1.0),
        "w_mod": nrm(ks[4], (DEPTH, d, 6 * d), 0.5 * d ** -0.5),
        "b_mod": nrm(ks[5], (DEPTH, 6 * d), 0.02),
        "norm1_g": 1.0 + nrm(ks[6], (DEPTH, d), 0.02),
        "norm2_g": 1.0 + nrm(ks[7], (DEPTH, d), 0.02),
        "conv_w_pw1": nrm(ks[8], (N_CONV_LAYERS, d, 2 * CONV_CH), d ** -0.5),
        "conv_b_pw1": nrm(ks[9], (N_CONV_LAYERS, 2 * CONV_CH), 0.02),
        "conv_w_dw": nrm(ks[10], (N_CONV_LAYERS, CONV_WIDTH, CONV_CH), CONV_WIDTH ** -0.5),
        "conv_b_dw": nrm(ks[11], (N_CONV_LAYERS, CONV_CH), 0.02),
        "conv_ln_g": 1.0 + nrm(ks[12], (N_CONV_LAYERS, CONV_CH), 0.02),
        "conv_ln_b": nrm(ks[13], (N_CONV_LAYERS, CONV_CH), 0.02),
        "conv_w_pw2": nrm(ks[14], (N_CONV_LAYERS, CONV_CH, d), CONV_CH ** -0.5),
        "conv_b_pw2": nrm(ks[15], (N_CONV_LAYERS, d), 0.02),
        "attn_w_qkv": nrm(ks[16], (N_ATTN_LAYERS, d, Q_DIM + 2 * KV_DIM), d ** -0.5),
        "attn_w_o": nrm(ks[17], (N_ATTN_LAYERS, Q_DIM, d), Q_DIM ** -0.5),
        "attn_sink": nrm(ks[18], (N_ATTN_LAYERS, N_HEADS), 1.0),
        "router_w": nrm(ks[19], (d, N_EXPERTS), d ** -0.5),
        "router_b": nrm(ks[20], (N_EXPERTS,), 0.01),
        "moe_w_gate": nrm(ks[21], (DEPTH, N_EXPERTS, d, f), d ** -0.5),
        "moe_w_up": nrm(ks[22], (DEPTH, N_EXPERTS, d, f), d ** -0.5),
        "moe_w_down": nrm(ks[23], (DEPTH, N_EXPERTS, f, d), f ** -0.5),
        "final_g": 1.0 + nrm(ks[24], (d,), 0.02),
    }


def reference(x, c, ctx, c_ctx, w_mod, b_mod, norm1_g, norm2_g,
              conv_w_pw1, conv_b_pw1, conv_w_dw, conv_b_dw, conv_ln_g, conv_ln_b, conv_w_pw2, conv_b_pw2,
              attn_w_qkv, attn_w_o, attn_sink, router_w, router_b,
              moe_w_gate, moe_w_up, moe_w_down, final_g):
    b, s, d = x.shape
    lc = ctx.shape[1]
    rope = _axial_rope_tables(s, x.dtype)
    for i in range(DEPTH):
        last = i == DEPTH - 1
        is_attn = (i % N_MIXERS) == 1
        j = i // N_MIXERS
        sh1, sc1, g1, sh2, sc2, g2 = _adaln(c, w_mod[i], b_mod[i])
        csh1, csc1, cg1, csh2, csc2, cg2 = _adaln(c_ctx, w_mod[i], b_mod[i])
        h = _modulate(_rmsnorm(x, norm1_g[i]), sh1, sc1)
        if is_attn:
            hc = _modulate(_rmsnorm(ctx, norm1_g[i]), csh1, csc1)
            y, yc = _windowed_sink_gqa(h, hc, attn_w_qkv[j], attn_w_o[j], attn_sink[j], rope, not last)
        else:
            conv_p = (conv_w_pw1[j], conv_b_pw1[j], conv_w_dw[j], conv_b_dw[j],
                      conv_ln_g[j], conv_ln_b[j], conv_w_pw2[j], conv_b_pw2[j])
            y = _conformer_conv(h, *conv_p)
            if not last:
                hc = _modulate(_rmsnorm(ctx, norm1_g[i]), csh1, csc1)
                yc = _conformer_conv(hc, *conv_p)
        x = x + g1 * y
        h2 = _modulate(_rmsnorm(x, norm2_g[i]), sh2, sc2)
        if last:
            out = _grouped_moe(h2.reshape(b * s, d), router_w, router_b,
                               moe_w_gate[i], moe_w_up[i], moe_w_down[i]).reshape(b, s, d)
            x = x + g2 * out
        else:
            ctx = ctx + cg1 * yc
            hc2 = _modulate(_rmsnorm(ctx, norm2_g[i]), csh2, csc2)
            tokens = jnp.concatenate([hc2, h2], axis=1).reshape(b * (lc + s), d)
            out = _grouped_moe(tokens, router_w, router_b,
                               moe_w_gate[i], moe_w_up[i], moe_w_down[i]).reshape(b, lc + s, d)
            ctx = ctx + cg2 * out[:, :lc]
            x = x + g2 * out[:, lc:]
    return _rmsnorm(x, final_g)
```

```python
import functools

import jax
import jax.numpy as jnp
from jax import lax
from jax.experimental import pallas as pl
from jax.experimental.pallas import tpu as pltpu

D = 1024
GRID_W = 64
CONV_WIDTH = 31
CONV_PAD = CONV_WIDTH // 2
HALO = 16
HEAD_DIM = 64
N_HEADS = D // HEAD_DIM
N_KV_HEADS = N_HEADS // 4
GROUP = N_HEADS // N_KV_HEADS
KV_DIM = N_KV_HEADS * HEAD_DIM
WINDOW = 128
QBLK = 128
ROPE_THETA = 10000.0
ROPE_PAIRS = HEAD_DIM // 4
N_EXPERTS = 16
N_GROUPS = 4
EXPERTS_PER_GROUP = N_EXPERTS // N_GROUPS
N_PAIRS = 6
N_CLASSES = N_GROUPS * N_PAIRS
D_EXPERT = D // 2
EPS = 1e-6
NEG_INF = -1e30
LANES = 128
AUX = LANES
ROW_W = D + AUX
COND_ROWS = 16
VMEM_LIMIT = 56 * 1024 * 1024

F32 = jnp.float32
BF16 = jnp.bfloat16
HIGHEST = lax.Precision.HIGHEST


def _cparams(sem, **kw):
    return pltpu.CompilerParams(dimension_semantics=sem, vmem_limit_bytes=VMEM_LIMIT, **kw)


def _sigmoid(v):
    return jax.nn.sigmoid(v)


def _silu(v):
    return v * jax.nn.sigmoid(v)


def _rms_mod(xr, gm, sh):
    r = lax.rsqrt(jnp.mean(xr * xr, axis=-1, keepdims=True) + EPS)
    return xr * r * gm + sh


def _adaln_kernel(cond_ref, w_ref, b_ref, out_ref):
    cnd = cond_ref[...]
    act = _silu(cnd)
    out_ref[0] = jnp.dot(act, w_ref[0], precision=HIGHEST, preferred_element_type=F32) + b_ref[0]


def _adaln(cond, w_mod, b_mod):
    depth = w_mod.shape[0]
    tn = 1536
    return pl.pallas_call(
        _adaln_kernel,
        out_shape=jax.ShapeDtypeStruct((depth, COND_ROWS, 6 * D), F32),
        grid=(depth, 6 * D // tn),
        in_specs=[
            pl.BlockSpec((COND_ROWS, D), lambda l, n: (0, 0)),
            pl.BlockSpec((1, D, tn), lambda l, n: (l, 0, n)),
            pl.BlockSpec((1, 1, tn), lambda l, n: (l, 0, n)),
        ],
        out_specs=pl.BlockSpec((1, COND_ROWS, tn), lambda l, n: (l, 0, n)),
        compiler_params=_cparams(("arbitrary", "arbitrary")),
        name="adaln",
    )(cond, w_mod, b_mod.reshape(depth, 1, 6 * D))


def _route_rows(h2, rwt_ref, rb_ref):
    lt = lax.dot_general(rwt_ref[...], h2, (((1,), (1,)), ((), ())),
                         precision=HIGHEST, preferred_element_type=F32)
    s = _sigmoid(lt)
    sel = s + rb_ref[...]
    srow = [s[e:e + 1, :] for e in range(N_EXPERTS)]
    vrow = [sel[e:e + 1, :] for e in range(N_EXPERTS)]
    gscore = []
    for g in range(N_GROUPS):
        a, b, c, d = vrow[4 * g:4 * g + 4]
        gscore.append(jnp.maximum(jnp.maximum(jnp.maximum(a + b, a + c), jnp.maximum(a + d, b + c)),
                                  jnp.maximum(b + d, c + d)))
    bg = jnp.zeros_like(gscore[0], dtype=jnp.int32)
    best = gscore[0]
    for g in range(1, N_GROUPS):
        better = gscore[g] > best
        bg = jnp.where(better, g, bg)
        best = jnp.where(better, gscore[g], best)
    v = []
    sv = []
    for j in range(EXPERTS_PER_GROUP):
        vj = vrow[j]
        sj = srow[j]
        for g in range(1, N_GROUPS):
            vj = jnp.where(bg == g, vrow[4 * g + j], vj)
            sj = jnp.where(bg == g, srow[4 * g + j], sj)
        v.append(vj)
        sv.append(sj)
    i1 = jnp.zeros_like(bg)
    m1 = v[0]
    for j in range(1, EXPERTS_PER_GROUP):
        gt = v[j] > m1
        i1 = jnp.where(gt, j, i1)
        m1 = jnp.where(gt, v[j], m1)
    v2 = [jnp.where(i1 == j, -jnp.inf, v[j]) for j in range(EXPERTS_PER_GROUP)]
    i2 = jnp.zeros_like(bg)
    m2 = v2[0]
    for j in range(1, EXPERTS_PER_GROUP):
        gt = v2[j] > m2
        i2 = jnp.where(gt, j, i2)
        m2 = jnp.where(gt, v2[j], m2)
    lo = jnp.minimum(i1, i2)
    hi = jnp.maximum(i1, i2)
    s_lo = sv[0]
    s_hi = sv[0]
    for j in range(1, EXPERTS_PER_GROUP):
        s_lo = jnp.where(lo == j, sv[j], s_lo)
        s_hi = jnp.where(hi == j, sv[j], s_hi)
    den = s_lo + s_hi
    pair = jnp.where(lo == 0, 0, jnp.where(lo == 1, 3, 5)) + (hi - lo - 1)
    cls = N_PAIRS * bg + pair
    return s_lo / den, s_hi / den, cls


def _epilogue(x_in, y, g1, sh2, gm2, rwt_ref, rb_ref, x_out_ref, hrow_ref, cls_ref):
    x1 = x_in + g1 * y
    x_out_ref[0] = x1
    h2 = _rms_mod(x1, gm2, sh2)
    ga, gb, cls = _route_rows(h2, rwt_ref, rb_ref)
    ts = h2.shape[0]
    cls_ref[...] = cls
    g8 = jnp.concatenate([ga, gb, jnp.zeros((6, ts), F32)], axis=0)
    g128 = jnp.concatenate([g8, jnp.zeros((AUX - 8, ts), F32)], axis=0)
    hrow_ref[:, :D] = h2
    hrow_ref[:, D:] = jnp.transpose(g128)


def _conv_layer_kernel(xm_ref, xp_ref, xn_ref, mod_ref, n1g_ref, n2g_ref, w1_ref, b1_ref, wdw_ref, bdw_ref,
                       lng_ref, lnb_ref, w2_ref, b2_ref, rwt_ref, rb_ref, *rest, ts, aliased):
    if aliased:
        rest = rest[2:]
    x_out_ref, hrow_ref, cls_ref, u_scr, v_scr = rest
    i = pl.program_id(1)
    nt = pl.num_programs(1)
    mod = mod_ref[0]
    sh1, sc1, g1, sh2, sc2 = (mod[k:k + 1, :] for k in range(5))
    gm1 = n1g_ref[...] * (1.0 + sc1)
    gm2 = n2g_ref[...] * (1.0 + sc2)
    cw = 512

    def glu(hb, c0):
        a = jnp.dot(hb, w1_ref[:, c0:c0 + cw], preferred_element_type=F32) + b1_ref[:, c0:c0 + cw]
        g = jnp.dot(hb, w1_ref[:, D + c0:D + c0 + cw], preferred_element_type=F32) + b1_ref[:, D + c0:D + c0 + cw]
        return a * _sigmoid(g)

    xm = xm_ref[0]
    hb_main = _rms_mod(xm, gm1, sh1).astype(BF16)
    halo = jnp.concatenate([xp_ref[0, 0], xn_ref[0, 0]], axis=0)
    hb_halo = _rms_mod(halo, gm1, sh1).astype(BF16)
    keep_prev = (i > 0).astype(F32)
    keep_next = (i < nt - 1).astype(F32)
    for c0 in range(0, D, cw):
        u_scr[HALO:HALO + ts, c0:c0 + cw] = glu(hb_main, c0)
        uh = glu(hb_halo, c0)
        u_scr[0:HALO, c0:c0 + cw] = uh[:HALO] * keep_prev
        u_scr[HALO + ts:2 * HALO + ts, c0:c0 + cw] = uh[HALO:] * keep_next

    rr = 32
    cc = 256

    def conv_chunk(rc, carry):
        r0 = pl.multiple_of(rc * rr, rr)
        parts = []
        for c0 in range(0, D, cc):
            acc = jnp.zeros((rr, cc), F32)
            win = u_scr[pl.ds(r0, rr + 2 * HALO), c0:c0 + cc]
            for k in range(CONV_WIDTH):
                k0 = k + HALO - CONV_PAD
                acc = acc + win[k0:k0 + rr, :] * wdw_ref[k:k + 1, c0:c0 + cc]
            parts.append(acc)
        cv = jnp.concatenate(parts, axis=1) + bdw_ref[...]
        mu = jnp.mean(cv, axis=-1, keepdims=True)
        dv = cv - mu
        var = jnp.mean(dv * dv, axis=-1, keepdims=True)
        ln = dv * lax.rsqrt(var + EPS) * lng_ref[...] + lnb_ref[...]
        v_scr[pl.ds(r0, rr), :] = _silu(ln).astype(BF16)
        return carry

    lax.fori_loop(0, ts // rr, conv_chunk, 0)
    y = jnp.dot(v_scr[...], w2_ref[...], preferred_element_type=F32) + b2_ref[...]
    _epilogue(xm, y, g1, sh2, gm2, rwt_ref, rb_ref, x_out_ref, hrow_ref, cls_ref)


def _conv_layer(xs, mod, n1g, n2g, w1, b1, wdw, bdw, lng, lnb, w2, b2, rwt, rb, *, ts, t_all, row_off,
                hrow=None, cls=None):
    bsz, s, _ = xs.shape
    nt = s // ts
    nh = s // HALO
    hpt = ts // HALO
    x4 = xs.reshape(bsz, nh, HALO, D)
    per_batch = mod.shape[0] > 1
    blk_off = row_off // ts
    aliased = hrow is not None
    const = lambda b, i: (0, 0)
    in_specs = [
        pl.BlockSpec((1, ts, D), lambda b, i: (b, i, 0)),
        pl.BlockSpec((1, 1, HALO, D), lambda b, i: (b, jnp.maximum(i * hpt - 1, 0), 0, 0)),
        pl.BlockSpec((1, 1, HALO, D), lambda b, i: (b, jnp.minimum((i + 1) * hpt, nh - 1), 0, 0)),
        pl.BlockSpec((1, 6, D), (lambda b, i: (b, 0, 0)) if per_batch else (lambda b, i: (0, 0, 0))),
        pl.BlockSpec((1, D), const), pl.BlockSpec((1, D), const),
        pl.BlockSpec((D, 2 * D), const), pl.BlockSpec((1, 2 * D), const),
        pl.BlockSpec((CONV_WIDTH, D), const), pl.BlockSpec((1, D), const),
        pl.BlockSpec((1, D), const), pl.BlockSpec((1, D), const),
        pl.BlockSpec((D, D), const), pl.BlockSpec((1, D), const),
        pl.BlockSpec((N_EXPERTS, D), const), pl.BlockSpec((N_EXPERTS, 1), const),
    ]
    args = [xs, x4, x4, mod, n1g, n2g, w1, b1, wdw, bdw, lng, lnb, w2, b2, rwt, rb]
    aliases = {}
    if aliased:
        in_specs += [pl.BlockSpec(memory_space=pl.ANY), pl.BlockSpec(memory_space=pl.ANY)]
        aliases = {len(args): 1, len(args) + 1: 2}
        args += [hrow, cls]
    return pl.pallas_call(
        functools.partial(_conv_layer_kernel, ts=ts, aliased=aliased),
        out_shape=(jax.ShapeDtypeStruct((bsz, s, D), F32),
                   jax.ShapeDtypeStruct((t_all, ROW_W), F32),
                   jax.ShapeDtypeStruct((1, t_all), jnp.int32)),
        grid=(bsz, nt),
        in_specs=in_specs,
        out_specs=(pl.BlockSpec((1, ts, D), lambda b, i: (b, i, 0)),
                   pl.BlockSpec((ts, ROW_W), lambda b, i: (blk_off + b * nt + i, 0)),
                   pl.BlockSpec((1, ts), lambda b, i: (0, blk_off + b * nt + i))),
        scratch_shapes=[pltpu.VMEM((ts + 2 * HALO, D), F32), pltpu.VMEM((ts, D), BF16)],
        input_output_aliases=aliases,
        compiler_params=_cparams(("arbitrary", "arbitrary")),
        name="conv_layer",
    )(*args)


def _plan_kernel(cls_ref, rank_ref, cnt_ref, carry_scr, *, tb):
    @pl.when(pl.program_id(0) == 0)
    def _():
        carry_scr[...] = jnp.zeros_like(carry_scr)

    cls = cls_ref[...]
    onehot = (lax.broadcasted_iota(jnp.int32, (32, tb), 0) == cls).astype(F32)
    upper = (lax.broadcasted_iota(jnp.int32, (tb, tb), 0) <= lax.broadcasted_iota(jnp.int32, (tb, tb), 1))
    prefix = jnp.dot(onehot.astype(BF16), upper.astype(BF16), preferred_element_type=F32)
    carry = carry_scr[:, 0:1]
    rank = jnp.sum(onehot * (prefix + carry), axis=0, keepdims=True) - 1.0
    rank_ref[...] = rank.astype(jnp.int32)
    total = carry + jnp.sum(onehot, axis=1, keepdims=True)
    carry_scr[...] = jnp.broadcast_to(total, carry_scr.shape)
    cnt_ref[...] = jnp.broadcast_to(total, cnt_ref.shape)


def _plan(cls, *, tb=512):
    t_all = cls.shape[1]
    return pl.pallas_call(
        functools.partial(_plan_kernel, tb=tb),
        out_shape=(jax.ShapeDtypeStruct((1, t_all), jnp.int32), jax.ShapeDtypeStruct((32, LANES), F32)),
        grid=(t_all // tb,),
        in_specs=[pl.BlockSpec((1, tb), lambda i: (0, i))],
        out_specs=(pl.BlockSpec((1, tb), lambda i: (0, i)), pl.BlockSpec((32, LANES), lambda i: (0, 0))),
        scratch_shapes=[pltpu.VMEM((32, LANES), F32)],
        compiler_params=_cparams(("arbitrary",)),
        name="route_plan",
    )(cls)


def _dispatch_tables(cls, rank, counts, *, tm):
    t_all = cls.shape[1]
    n_tiles = t_all // tm + N_CLASSES
    cnt = counts[:N_CLASSES, 0].astype(jnp.int32)
    padded = ((cnt + tm - 1) // tm) * tm
    ends = jnp.cumsum(padded)
    offs = ends - padded
    pos = offs[cls[0]] + rank[0]
    src = jnp.zeros((n_tiles * tm,), jnp.int32).at[pos].set(jnp.arange(t_all, dtype=jnp.int32))
    n_used = ends[-1] // tm
    tile_start = jnp.minimum(jnp.arange(n_tiles, dtype=jnp.int32), n_used - 1) * tm
    tile_cls = jnp.minimum(jnp.searchsorted(ends, tile_start, side="right"), N_CLASSES - 1).astype(jnp.int32)
    pair_lo = jnp.array([0, 0, 0, 1, 1, 2], jnp.int32)
    pair_hi = jnp.array([1, 2, 3, 2, 3, 3], jnp.int32)
    grp = tile_cls // N_PAIRS
    ea = EXPERTS_PER_GROUP * grp + pair_lo[tile_cls % N_PAIRS]
    eb = EXPERTS_PER_GROUP * grp + pair_hi[tile_cls % N_PAIRS]
    return pos, src, ea, eb, n_used.reshape(1).astype(jnp.int32)


def _moe_kernel(src_ref, ea_ref, eb_ref, nused_ref, hrow_ref, wga_ref, wua_ref, wda_ref, wgb_ref, wub_ref, wdb_ref,
                ys_ref, gbuf, sem, *, tm):
    j = pl.program_id(0)
    nused = nused_ref[0]
    slot = j % 2

    def issue(tile, sl):
        def body(r, carry):
            tok = src_ref[tile * tm + r]
            pltpu.make_async_copy(hrow_ref.at[pl.ds(tok, 1), :], gbuf.at[sl, pl.ds(r, 1), :], sem.at[sl]).start()
            return carry
        lax.fori_loop(0, tm, body, 0)

    @pl.when(j == 0)
    def _():
        issue(0, 0)

    @pl.when(j + 1 < nused)
    def _():
        issue(j + 1, 1 - slot)

    @pl.when(j < nused)
    def _():
        pltpu.make_async_copy(hrow_ref.at[pl.ds(0, tm), :], gbuf.at[slot], sem.at[slot]).wait()
        rows = gbuf[slot]
        h = rows[:, :D].astype(BF16)
        ga = rows[:, D:D + 1]
        gb = rows[:, D + 1:D + 2]

        def expert(wg, wu, wd):
            a = jnp.dot(h, wg[0], preferred_element_type=F32)
            u = jnp.dot(h, wu[0], preferred_element_type=F32)
            act = (_silu(a) * u).astype(BF16)
            return jnp.dot(act, wd[0], preferred_element_type=F32)

        ys_ref[...] = ga * expert(wga_ref, wua_ref, wda_ref) + gb * expert(wgb_ref, wub_ref, wdb_ref)

    @pl.when(j >= nused)
    def _():
        ys_ref[...] = jnp.zeros_like(ys_ref)


def _moe(hrow, src, ea, eb, nused, wg, wu, wd, *, tm):
    n_tiles = ea.shape[0]
    f = wg.shape[-1]
    wa = lambda j, src, ea, eb, nu: (ea[j], 0, 0)
    wb = lambda j, src, ea, eb, nu: (eb[j], 0, 0)
    return pl.pallas_call(
        functools.partial(_moe_kernel, tm=tm),
        out_shape=jax.ShapeDtypeStruct((n_tiles * tm, D), F32),
        grid_spec=pltpu.PrefetchScalarGridSpec(
            num_scalar_prefetch=4,
            grid=(n_tiles,),
            in_specs=[
                pl.BlockSpec(memory_space=pl.ANY),
                pl.BlockSpec((1, D, f), wa), pl.BlockSpec((1, D, f), wa), pl.BlockSpec((1, f, D), wa),
                pl.BlockSpec((1, D, f), wb), pl.BlockSpec((1, D, f), wb), pl.BlockSpec((1, f, D), wb),
            ],
            out_specs=pl.BlockSpec((tm, D), lambda j, src, ea, eb, nu: (j, 0)),
            scratch_shapes=[pltpu.VMEM((2, tm, ROW_W), F32), pltpu.SemaphoreType.DMA((2,))],
        ),
        compiler_params=_cparams(("arbitrary",)),
        name="moe_experts",
    )(src, ea, eb, nused, hrow, wg, wu, wd, wg, wu, wd)


def _gather_rows(pos_ref, ys_ref, ybuf, sem, base, ts):
    def body(r, carry):
        p = pos_ref[base + r]
        pltpu.make_async_copy(ys_ref.at[pl.ds(p, 1), :], ybuf.at[pl.ds(r, 1), :], sem.at[0]).start()
        return carry
    lax.fori_loop(0, ts, body, 0)
    pltpu.make_async_copy(ys_ref.at[pl.ds(0, ts), :], ybuf, sem.at[0]).wait()
    return ybuf[...]


def _rope(v, cos, sin):
    lane = lax.broadcasted_iota(jnp.int32, v.shape, 1)
    first = (lane % (2 * ROPE_PAIRS)) < ROPE_PAIRS
    partner = jnp.where(first, pltpu.roll(v, LANES - ROPE_PAIRS, 1), pltpu.roll(v, ROPE_PAIRS, 1))
    return v * cos + partner * sin


def _qkv_kernel(pos_ref, x_ref, ys_ref, modp_ref, mod_ref, n1g_ref, w_ref, cq_ref, sq_ref, ck_ref, sk_ref,
                x_out_ref, q_ref, kd_ref, vd_ref, ybuf, sem, *, ts, seq, row_off):
    b = pl.program_id(0)
    i = pl.program_id(1)
    base = row_off + b * seq + i * ts
    y = _gather_rows(pos_ref, ys_ref, ybuf, sem, base, ts)
    g2_prev = modp_ref[0][5:6, :]
    x2 = x_ref[0] + g2_prev * y
    x_out_ref[0] = x2
    mod = mod_ref[0]
    sh1 = mod[0:1, :]
    gm1 = n1g_ref[...] * (1.0 + mod[1:2, :])
    hb = _rms_mod(x2, gm1, sh1).astype(BF16)
    cq, sq, ck, sk = cq_ref[...], sq_ref[...], ck_ref[...], sk_ref[...]
    for c0 in range(0, D, LANES):
        qc = jnp.dot(hb, w_ref[:, c0:c0 + LANES], preferred_element_type=F32)
        q_ref[0, :, c0:c0 + LANES] = _rope(qc, cq, sq).astype(BF16)
    for c0 in range(0, 2 * KV_DIM, LANES):
        kc = jnp.dot(hb, w_ref[:, D + c0:D + c0 + LANES], preferred_element_type=F32)
        kd_ref[0, :, c0:c0 + LANES] = _rope(kc, ck, sk).astype(BF16)
    vd_ref[0] = jnp.dot(hb, w_ref[:, D + 2 * KV_DIM:], preferred_element_type=F32).astype(BF16)


def _qkv(pos, xs, ys, modp, mod, n1g, w_ext, cq, sq, ck, sk, *, ts, row_off):
    bsz, s, _ = xs.shape
    nt = s // ts
    per_batch = mod.shape[0] > 1
    mod_map = (lambda b, i, p: (b, 0, 0)) if per_batch else (lambda b, i, p: (0, 0, 0))
    const = lambda b, i, p: (0, 0)
    tok = lambda b, i, p: (b, i, 0)
    tab = lambda b, i, p: (i, 0)
    return pl.pallas_call(
        functools.partial(_qkv_kernel, ts=ts, seq=s, row_off=row_off),
        out_shape=(jax.ShapeDtypeStruct((bsz, s, D), F32), jax.ShapeDtypeStruct((bsz, s, D), BF16),
                   jax.ShapeDtypeStruct((bsz, s, 2 * KV_DIM), BF16), jax.ShapeDtypeStruct((bsz, s, 2 * KV_DIM), BF16)),
        grid_spec=pltpu.PrefetchScalarGridSpec(
            num_scalar_prefetch=1,
            grid=(bsz, nt),
            in_specs=[
                pl.BlockSpec((1, ts, D), tok),
                pl.BlockSpec(memory_space=pl.ANY),
                pl.BlockSpec((1, 6, D), mod_map), pl.BlockSpec((1, 6, D), mod_map),
                pl.BlockSpec((1, D), const),
                pl.BlockSpec((D, D + 4 * KV_DIM), const),
                pl.BlockSpec((ts, LANES), tab), pl.BlockSpec((ts, LANES), tab),
                pl.BlockSpec((ts, LANES), tab), pl.BlockSpec((ts, LANES), tab),
            ],
            out_specs=(pl.BlockSpec((1, ts, D), tok), pl.BlockSpec((1, ts, D), tok),
                       pl.BlockSpec((1, ts, 2 * KV_DIM), tok), pl.BlockSpec((1, ts, 2 * KV_DIM), tok)),
            scratch_shapes=[pltpu.VMEM((ts, D), F32), pltpu.SemaphoreType.DMA((1,))],
        ),
        compiler_params=_cparams(("arbitrary", "arbitrary")),
        name="combine_qkv",
    )(pos, xs, ys, modp, mod, n1g, w_ext, cq, sq, ck, sk)


def _attn_kernel(q_ref, kp_ref, kc_ref, kn_ref, vp_ref, vc_ref, vn_ref, kx_ref, vx_ref, sink_ref, o_ref, *, lc):
    i = pl.program_id(1)
    nb = pl.num_programs(1)
    nband = 3 * QBLK
    qrow = lax.broadcasted_iota(jnp.int32, (GROUP * QBLK, nband), 0) % QBLK
    kcol = lax.broadcasted_iota(jnp.int32, (GROUP * QBLK, nband), 1)
    lo_lim = jnp.where(i > 0, 0, QBLK)
    hi_lim = jnp.where(i < nb - 1, nband, 2 * QBLK - 1)
    ok_band = (kcol >= jnp.maximum(qrow, lo_lim)) & (kcol <= jnp.minimum(qrow + 2 * WINDOW, hi_lim))
    lane = lax.broadcasted_iota(jnp.int32, (1, LANES), 1)
    low = lane < HEAD_DIM
    for h in range(N_KV_HEADS):
        hs = slice(h * LANES, (h + 1) * LANES)
        k_all = jnp.concatenate([kp_ref[0, :, hs], kc_ref[0, :, hs], kn_ref[0, :, hs], kx_ref[0, :, hs]], axis=0)
        v_all = jnp.concatenate([vp_ref[0, :, hs], vc_ref[0, :, hs], vn_ref[0, :, hs], vx_ref[0, :, hs]], axis=0)
        v_lo = jnp.where(low, v_all, jnp.zeros_like(v_all))
        v_hi = jnp.where(low, jnp.zeros_like(v_all), v_all)
        qs = []
        for j in range(GROUP):
            c0 = h * GROUP * HEAD_DIM + (j // 2) * LANES
            qh = q_ref[0, :, c0:c0 + LANES]
            keep = low if j % 2 == 0 else jnp.logical_not(low)
            qs.append(jnp.where(keep, qh, jnp.zeros_like(qh)))
        qm = jnp.concatenate(qs, axis=0)
        s = lax.dot_general(qm, k_all, (((1,), (1,)), ((), ())), preferred_element_type=F32)
        s_band = jnp.where(ok_band, s[:, :nband], NEG_INF)
        s_ctx = s[:, nband:]
        sink = jnp.concatenate(
            [jnp.broadcast_to(sink_ref[:, h * GROUP + j:h * GROUP + j + 1], (QBLK, 1)) for j in range(GROUP)], axis=0)
        m = jnp.maximum(jnp.maximum(jnp.max(s_band, axis=1, keepdims=True), jnp.max(s_ctx, axis=1, keepdims=True)),
                        sink)
        p_band = jnp.exp(s_band - m)
        p_ctx = jnp.exp(s_ctx - m)
        den = jnp.sum(p_band, axis=1, keepdims=True) + jnp.sum(p_ctx, axis=1, keepdims=True) + jnp.exp(sink - m)
        inv = 1.0 / den
        p = jnp.concatenate([p_band, p_ctx], axis=1).astype(BF16)
        for pr in range(GROUP // 2):
            r0 = 2 * pr * QBLK
            o = (jnp.dot(p[r0:r0 + QBLK], v_lo, preferred_element_type=F32) * inv[r0:r0 + QBLK]
                 + jnp.dot(p[r0 + QBLK:r0 + 2 * QBLK], v_hi, preferred_element_type=F32) * inv[r0 + QBLK:r0 + 2 * QBLK])
            c0 = h * GROUP * HEAD_DIM + pr * LANES
            o_ref[0, :, c0:c0 + LANES] = o.astype(BF16)


def _attention(q, kd, vd, kxd, vxd, sink):
    bsz, s, _ = q.shape
    lc = kxd.shape[1]
    nb = s // QBLK
    cur = lambda b, i: (b, i, 0)
    prev = lambda b, i: (b, jnp.maximum(i - 1, 0), 0)
    nxt = lambda b, i: (b, jnp.minimum(i + 1, nb - 1), 0)
    ctx = lambda b, i: (b, 0, 0)
    kvw = 2 * KV_DIM
    return pl.pallas_call(
        functools.partial(_attn_kernel, lc=lc),
        out_shape=jax.ShapeDtypeStruct((bsz, s, D), BF16),
        grid=(bsz, nb),
        in_specs=[
            pl.BlockSpec((1, QBLK, D), cur),
            pl.BlockSpec((1, QBLK, kvw), prev), pl.BlockSpec((1, QBLK, kvw), cur), pl.BlockSpec((1, QBLK, kvw), nxt),
            pl.BlockSpec((1, QBLK, kvw), prev), pl.BlockSpec((1, QBLK, kvw), cur), pl.BlockSpec((1, QBLK, kvw), nxt),
            pl.BlockSpec((1, lc, kvw), ctx), pl.BlockSpec((1, lc, kvw), ctx),
            pl.BlockSpec((1, N_HEADS), lambda b, i: (0, 0)),
        ],
        out_specs=pl.BlockSpec((1, QBLK, D), cur),
        compiler_params=_cparams(("arbitrary", "arbitrary")),
        name="window_attention",
    )(q, kd, kd, kd, vd, vd, vd, kxd, vxd, sink)


def _attn_out_kernel(a_ref, x_ref, mod_ref, n2g_ref, wo_ref, rwt_ref, rb_ref, x_out_ref, hrow_ref, cls_ref):
    mod = mod_ref[0]
    g1, sh2, sc2 = mod[2:3, :], mod[3:4, :], mod[4:5, :]
    gm2 = n2g_ref[...] * (1.0 + sc2)
    y = jnp.dot(a_ref[0], wo_ref[...], preferred_element_type=F32)
    _epilogue(x_ref[0], y, g1, sh2, gm2, rwt_ref, rb_ref, x_out_ref, hrow_ref, cls_ref)


def _attn_out(attn, xs, mod, n2g, wo, rwt, rb, *, ts):
    bsz, s, _ = xs.shape
    nt = s // ts
    t_all = bsz * s
    const = lambda b, i: (0, 0)
    tok = lambda b, i: (b, i, 0)
    return pl.pallas_call(
        _attn_out_kernel,
        out_shape=(jax.ShapeDtypeStruct((bsz, s, D), F32),
                   jax.ShapeDtypeStruct((t_all, ROW_W), F32),
                   jax.ShapeDtypeStruct((1, t_all), jnp.int32)),
        grid=(bsz, nt),
        in_specs=[
            pl.BlockSpec((1, ts, D), tok), pl.BlockSpec((1, ts, D), tok),
            pl.BlockSpec((1, 6, D), lambda b, i: (b, 0, 0)),
            pl.BlockSpec((1, D), const), pl.BlockSpec((D, D), const),
            pl.BlockSpec((N_EXPERTS, D), const), pl.BlockSpec((N_EXPERTS, 1), const),
        ],
        out_specs=(pl.BlockSpec((1, ts, D), tok),
                   pl.BlockSpec((ts, ROW_W), lambda b, i: (b * nt + i, 0)),
                   pl.BlockSpec((1, ts), lambda b, i: (0, b * nt + i))),
        compiler_params=_cparams(("arbitrary", "arbitrary")),
        name="attn_out",
    )(attn, xs, mod, n2g, wo, rwt, rb)


def _final_kernel(pos_ref, x_ref, ys_ref, mod_ref, fg_ref, o_ref, ybuf, sem, *, ts, seq):
    base = pl.program_id(0) * seq + pl.program_id(1) * ts
    y = _gather_rows(pos_ref, ys_ref, ybuf, sem, base, ts)
    x4 = x_ref[0] + mod_ref[0][5:6, :] * y
    r = lax.rsqrt(jnp.mean(x4 * x4, axis=-1, keepdims=True) + EPS)
    o_ref[0] = x4 * r * fg_ref[...]


def _final(pos, xs, ys, mod, fg, *, ts):
    bsz, s, _ = xs.shape
    tok = lambda b, i, p: (b, i, 0)
    return pl.pallas_call(
        functools.partial(_final_kernel, ts=ts, seq=s),
        out_shape=jax.ShapeDtypeStruct((bsz, s, D), F32),
        grid_spec=pltpu.PrefetchScalarGridSpec(
            num_scalar_prefetch=1,
            grid=(bsz, s // ts),
            in_specs=[pl.BlockSpec((1, ts, D), tok), pl.BlockSpec(memory_space=pl.ANY),
                      pl.BlockSpec((1, 6, D), lambda b, i, p: (b, 0, 0)), pl.BlockSpec((1, D), lambda b, i, p: (0, 0))],
            out_specs=pl.BlockSpec((1, ts, D), tok),
            scratch_shapes=[pltpu.VMEM((ts, D), F32), pltpu.SemaphoreType.DMA((1,))],
        ),
        compiler_params=_cparams(("arbitrary", "arbitrary")),
        name="combine_final",
    )(pos, xs, ys, mod, fg)


def _rope_tables(seq):
    t = jnp.arange(seq, dtype=jnp.int32)
    inv_freq = jnp.power(jnp.float32(ROPE_THETA), -jnp.arange(ROPE_PAIRS, dtype=F32) / ROPE_PAIRS)
    ang_r = (t // GRID_W).astype(F32)[:, None] * inv_freq
    ang_c = (t % GRID_W).astype(F32)[:, None] * inv_freq
    cos = jnp.concatenate([jnp.cos(ang_r)] * 2 + [jnp.cos(ang_c)] * 2, axis=1)
    sin = jnp.concatenate([-jnp.sin(ang_r), jnp.sin(ang_r), -jnp.sin(ang_c), jnp.sin(ang_c)], axis=1)
    cos = jnp.concatenate([cos, cos], axis=1)
    sin = jnp.concatenate([sin, sin], axis=1)
    scale = HEAD_DIM ** -0.5
    return cos * scale, sin * scale, cos, sin


def kernel(x, c, ctx, c_ctx, w_mod, b_mod, norm1_g, norm2_g, conv_w_pw1, conv_b_pw1, conv_w_dw, conv_b_dw,
           conv_ln_g, conv_ln_b, conv_w_pw2, conv_b_pw2, attn_w_qkv, attn_w_o, attn_sink, router_w, router_b,
           moe_w_gate, moe_w_up, moe_w_down, final_g):
    bsz, s, d = x.shape
    lc = ctx.shape[1]
    assert d == D and w_mod.shape[0] == 2 and bsz + 1 <= COND_ROWS
    assert s % 512 == 0 and lc % 256 == 0 and s % QBLK == 0
    ts = 512
    ts_ctx = 256
    tm = 256
    t_lat = bsz * s
    t_all = t_lat + bsz * lc

    cond = jnp.concatenate([c, c_ctx[None, :], jnp.zeros((COND_ROWS - bsz - 1, D), F32)], axis=0)
    mods = _adaln(cond, w_mod, b_mod).reshape(2, COND_ROWS, 6, D)
    mod_lat = [mods[l, :bsz] for l in range(2)]
    mod_ctx = [mods[l, bsz:bsz + 1] for l in range(2)]

    rwt = router_w.T
    rb = router_b.reshape(N_EXPERTS, 1)
    row = lambda v: v.reshape(1, -1)
    wg = moe_w_gate.astype(BF16)
    wu = moe_w_up.astype(BF16)
    wd = moe_w_down.astype(BF16)

    conv_args = (row(norm1_g[0]), row(norm2_g[0]), conv_w_pw1[0].astype(BF16), row(conv_b_pw1[0]), conv_w_dw[0],
                 row(conv_b_dw[0]), row(conv_ln_g[0]), row(conv_ln_b[0]), conv_w_pw2[0].astype(BF16),
                 row(conv_b_pw2[0]), rwt, rb)
    x1, hrow, cls = _conv_layer(x, mod_lat[0], *conv_args, ts=ts, t_all=t_all, row_off=0)
    ctx1, hrow, cls = _conv_layer(ctx, mod_ctx[0], *conv_args, ts=ts_ctx, t_all=t_all, row_off=t_lat,
                                  hrow=hrow, cls=cls)
    rank, counts = _plan(cls)
    pos, src, ea, eb, nused = _dispatch_tables(cls, rank, counts, tm=tm)
    ys = _moe(hrow, src, ea, eb, nused, wg[0], wu[0], wd[0], tm=tm)

    wq = attn_w_qkv[0][:, :D]
    wk = attn_w_qkv[0][:, D:D + KV_DIM].reshape(D, N_KV_HEADS, 1, HEAD_DIM)
    wv = attn_w_qkv[0][:, D + KV_DIM:].reshape(D, N_KV_HEADS, 1, HEAD_DIM)
    dup = lambda w: jnp.broadcast_to(w, (D, N_KV_HEADS, 2, HEAD_DIM)).reshape(D, 2 * KV_DIM)
    w_ext = jnp.concatenate([wq, dup(wk), dup(wv)], axis=1).astype(BF16)
    cq, sq, ck, sk = _rope_tables(s)
    ones = jnp.ones((lc, LANES), F32)
    zeros = jnp.zeros((lc, LANES), F32)
    n1g = row(norm1_g[1])
    x2, q, kd, vd = _qkv(pos, x1, ys, mod_lat[0], mod_lat[1], n1g, w_ext, cq, sq, ck, sk, ts=ts, row_off=0)
    _, _, kxd, vxd = _qkv(pos, ctx1, ys, mod_ctx[0], mod_ctx[1], n1g, w_ext, ones, zeros, ones, zeros,
                          ts=ts_ctx, row_off=t_lat)
    attn = _attention(q, kd, vd, kxd, vxd, attn_sink[0].reshape(1, N_HEADS))
    x3, hrow1, cls1 = _attn_out(attn, x2, mod_lat[1], row(norm2_g[1]), attn_w_o[0].astype(BF16), rwt, rb, ts=ts)
    rank1, counts1 = _plan(cls1)
    pos1, src1, ea1, eb1, nused1 = _dispatch_tables(cls1, rank1, counts1, tm=tm)
    ys1 = _moe(hrow1, src1, ea1, eb1, nused1, wg[1], wu[1], wd[1], tm=tm)
    return _final(pos1, x3, ys1, mod_lat[1], row(final_g), ts=ts)
```

```python
import functools

import jax
import jax.numpy as jnp
from jax import lax
from jax.experimental import pallas as pl
from jax.experimental.pallas import tpu as pltpu

D = 1024
GRID_W = 64
CONV_WIDTH = 31
CONV_PAD = CONV_WIDTH // 2
SUBLANES = 8
HALO = 16
HEAD_DIM = 64
N_HEADS = D // HEAD_DIM
N_KV_HEADS = N_HEADS // 4
GROUP = N_HEADS // N_KV_HEADS
KV_DIM = N_KV_HEADS * HEAD_DIM
WINDOW = 128
QBLK = 128
ROPE_THETA = 10000.0
ROPE_PAIRS = HEAD_DIM // 4
N_EXPERTS = 16
N_GROUPS = 4
EXPERTS_PER_GROUP = N_EXPERTS // N_GROUPS
N_PAIRS = 6
N_CLASSES = N_GROUPS * N_PAIRS
D_EXPERT = D // 2
EPS = 1e-6
NEG_INF = -1e30
LANES = 128
AUX = LANES
ROW_W = D + AUX
COND_ROWS = 16
VMEM_LIMIT = 56 * 1024 * 1024

F32 = jnp.float32
BF16 = jnp.bfloat16
HIGHEST = lax.Precision.HIGHEST


def _cparams(sem, **kw):
    return pltpu.CompilerParams(dimension_semantics=sem, vmem_limit_bytes=VMEM_LIMIT, **kw)


def _sigmoid(v):
    return jax.nn.sigmoid(v)


def _silu(v):
    return v * jax.nn.sigmoid(v)


def _rms_mod(xr, gm, sh):
    r = lax.rsqrt(jnp.mean(xr * xr, axis=-1, keepdims=True) + EPS)
    return xr * r * gm + sh


def _adaln_kernel(cond_ref, w_ref, b_ref, out_ref):
    cnd = cond_ref[...]
    act = _silu(cnd)
    out_ref[0] = jnp.dot(act, w_ref[0], precision=HIGHEST, preferred_element_type=F32) + b_ref[0]


def _adaln(cond, w_mod, b_mod):
    depth = w_mod.shape[0]
    tn = 1536
    return pl.pallas_call(
        _adaln_kernel,
        out_shape=jax.ShapeDtypeStruct((depth, COND_ROWS, 6 * D), F32),
        grid=(depth, 6 * D // tn),
        in_specs=[
            pl.BlockSpec((COND_ROWS, D), lambda l, n: (0, 0)),
            pl.BlockSpec((1, D, tn), lambda l, n: (l, 0, n)),
            pl.BlockSpec((1, 1, tn), lambda l, n: (l, 0, n)),
        ],
        out_specs=pl.BlockSpec((1, COND_ROWS, tn), lambda l, n: (l, 0, n)),
        compiler_params=_cparams(("arbitrary", "arbitrary")),
        name="adaln",
    )(cond, w_mod, b_mod.reshape(depth, 1, 6 * D))


def _route_rows(h2, rwt_ref, rb_ref):
    lt = lax.dot_general(rwt_ref[...], h2, (((1,), (1,)), ((), ())),
                         precision=HIGHEST, preferred_element_type=F32)
    s = _sigmoid(lt)
    sel = s + rb_ref[...]
    srow = [s[e:e + 1, :] for e in range(N_EXPERTS)]
    vrow = [sel[e:e + 1, :] for e in range(N_EXPERTS)]
    gscore = []
    for g in range(N_GROUPS):
        a, b, c, d = vrow[4 * g:4 * g + 4]
        gscore.append(jnp.maximum(jnp.maximum(jnp.maximum(a + b, a + c), jnp.maximum(a + d, b + c)),
                                  jnp.maximum(b + d, c + d)))
    bg = jnp.zeros_like(gscore[0], dtype=jnp.int32)
    best = gscore[0]
    for g in range(1, N_GROUPS):
        better = gscore[g] > best
        bg = jnp.where(better, g, bg)
        best = jnp.where(better, gscore[g], best)
    v = []
    sv = []
    for j in range(EXPERTS_PER_GROUP):
        vj = vrow[j]
        sj = srow[j]
        for g in range(1, N_GROUPS):
            vj = jnp.where(bg == g, vrow[4 * g + j], vj)
            sj = jnp.where(bg == g, srow[4 * g + j], sj)
        v.append(vj)
        sv.append(sj)
    i1 = jnp.zeros_like(bg)
    m1 = v[0]
    for j in range(1, EXPERTS_PER_GROUP):
        gt = v[j] > m1
        i1 = jnp.where(gt, j, i1)
        m1 = jnp.where(gt, v[j], m1)
    v2 = [jnp.where(i1 == j, -jnp.inf, v[j]) for j in range(EXPERTS_PER_GROUP)]
    i2 = jnp.zeros_like(bg)
    m2 = v2[0]
    for j in range(1, EXPERTS_PER_GROUP):
        gt = v2[j] > m2
        i2 = jnp.where(gt, j, i2)
        m2 = jnp.where(gt, v2[j], m2)
    lo = jnp.minimum(i1, i2)
    hi = jnp.maximum(i1, i2)
    s_lo = sv[0]
    s_hi = sv[0]
    for j in range(1, EXPERTS_PER_GROUP):
        s_lo = jnp.where(lo == j, sv[j], s_lo)
        s_hi = jnp.where(hi == j, sv[j], s_hi)
    den = s_lo + s_hi
    pair = jnp.where(lo == 0, 0, jnp.where(lo == 1, 3, 5)) + (hi - lo - 1)
    cls = N_PAIRS * bg + pair
    return s_lo / den, s_hi / den, cls


def _epilogue(x_in, y, g1, sh2, gm2, rwt_ref, rb_ref, hrow_ref, cls_ref):
    x1 = x_in + g1 * y
    h2 = _rms_mod(x1, gm2, sh2)
    ga, gb, cls = _route_rows(h2, rwt_ref, rb_ref)
    ts = h2.shape[0]
    cls_ref[...] = cls
    g8 = jnp.concatenate([ga, gb, jnp.zeros((6, ts), F32)], axis=0)
    g128 = jnp.concatenate([g8, jnp.zeros((AUX - 8, ts), F32)], axis=0)
    hrow_ref[:, :D] = h2
    hrow_ref[:, D:] = jnp.transpose(g128)
    return x1


def _conv_layer_kernel(x_ref, c_ref, xp_ref, xn_ref, modl_ref, modc_ref, n1g_ref, n2g_ref, w1_ref, b1_ref, wdw_ref,
                       bdw_ref, lng_ref, lnb_ref, w2_ref, b2_ref, rwt_ref, rb_ref,
                       x_out_ref, c_out_ref, hrow_ref, cls_ref, u_scr, cv_scr, v_scr, *, ts):
    i = pl.program_id(1)
    nt = pl.num_programs(1) - 1
    is_ctx = jnp.full((1, 1), i, jnp.int32) == nt
    mod = jnp.where(is_ctx, modc_ref[0], modl_ref[0])
    sh1, sc1, g1, sh2, sc2 = (mod[k:k + 1, :] for k in range(5))
    gm1 = n1g_ref[...] * (1.0 + sc1)
    gm2 = n2g_ref[...] * (1.0 + sc2)
    cw = 512

    def glu(hb, c0):
        a = jnp.dot(hb, w1_ref[:, c0:c0 + cw], preferred_element_type=F32) + b1_ref[:, c0:c0 + cw]
        g = jnp.dot(hb, w1_ref[:, D + c0:D + c0 + cw], preferred_element_type=F32) + b1_ref[:, D + c0:D + c0 + cw]
        return a * _sigmoid(g)

    xm = jnp.where(is_ctx, c_ref[0], x_ref[0])
    hb_main = _rms_mod(xm, gm1, sh1).astype(BF16)
    halo = jnp.concatenate([xp_ref[0, 0], xn_ref[0, 0]], axis=0)
    hb_halo = _rms_mod(halo, gm1, sh1).astype(BF16)
    keep_prev = jnp.logical_and(i > 0, i < nt).astype(F32)
    keep_next = (i < nt - 1).astype(F32)
    for c0 in range(0, D, cw):
        u_scr[HALO:HALO + ts, c0:c0 + cw] = glu(hb_main, c0)
        uh = glu(hb_halo, c0)
        u_scr[0:HALO, c0:c0 + cw] = uh[:HALO] * keep_prev
        u_scr[HALO + ts:2 * HALO + ts, c0:c0 + cw] = uh[HALO:] * keep_next

    rr = 64
    taps_off = HALO - CONV_PAD

    def conv_chunk(rc, carry):
        r0 = pl.multiple_of(rc * rr, rr)
        for c0 in range(0, D, LANES):
            win = u_scr[pl.ds(r0, rr + 2 * HALO), c0:c0 + LANES]
            out = None
            for r in range(SUBLANES):
                nrow = rr + SUBLANES if r else rr
                part = None
                for q in range((CONV_WIDTH + taps_off) // SUBLANES + 1):
                    k = SUBLANES * q + r - taps_off
                    if 0 <= k < CONV_WIDTH:
                        term = win[SUBLANES * q:SUBLANES * q + nrow, :] * wdw_ref[k:k + 1, c0:c0 + LANES]
                        part = term if part is None else part + term
                shifted = part[r:r + rr, :]
                out = shifted if out is None else out + shifted
            cv_scr[pl.ds(r0, rr), c0:c0 + LANES] = out
        cv = cv_scr[pl.ds(r0, rr), :] + bdw_ref[...]
        mu = jnp.mean(cv, axis=-1, keepdims=True)
        dv = cv - mu
        var = jnp.mean(dv * dv, axis=-1, keepdims=True)
        ln = dv * lax.rsqrt(var + EPS) * lng_ref[...] + lnb_ref[...]
        v_scr[pl.ds(r0, rr), :] = _silu(ln).astype(BF16)
        return carry

    lax.fori_loop(0, ts // rr, conv_chunk, 0)
    y = jnp.dot(v_scr[...], w2_ref[...], preferred_element_type=F32) + b2_ref[...]
    x1 = _epilogue(xm, y, g1, sh2, gm2, rwt_ref, rb_ref, hrow_ref, cls_ref)

    @pl.when(i < nt)
    def _():
        x_out_ref[0] = x1

    @pl.when(i == nt)
    def _():
        c_out_ref[0] = x1


def _conv_layer(xs, cs, mod_lat, mod_ctx, n1g, n2g, w1, b1, wdw, bdw, lng, lnb, w2, b2, rwt, rb, *, ts):
    bsz, s, _ = xs.shape
    assert cs.shape[1] == ts
    nt = s // ts
    nh = s // HALO
    hpt = ts // HALO
    x4 = xs.reshape(bsz, nh, HALO, D)
    t_all = bsz * (s + ts)
    const = lambda b, i: (0, 0)
    lat = lambda b, i: (b, jnp.minimum(i, nt - 1), 0)
    row_blk = lambda b, i: (b * (nt + 1) + i, 0)
    return pl.pallas_call(
        functools.partial(_conv_layer_kernel, ts=ts),
        out_shape=(jax.ShapeDtypeStruct((bsz, s, D), F32),
                   jax.ShapeDtypeStruct((bsz, ts, D), F32),
                   jax.ShapeDtypeStruct((t_all, ROW_W), F32),
                   jax.ShapeDtypeStruct((1, t_all), jnp.int32)),
        grid=(bsz, nt + 1),
        in_specs=[
            pl.BlockSpec((1, ts, D), lat),
            pl.BlockSpec((1, ts, D), lambda b, i: (b, 0, 0)),
            pl.BlockSpec((1, 1, HALO, D), lambda b, i: (b, jnp.clip(i * hpt - 1, 0, nh - 1), 0, 0)),
            pl.BlockSpec((1, 1, HALO, D), lambda b, i: (b, jnp.minimum((i + 1) * hpt, nh - 1), 0, 0)),
            pl.BlockSpec((1, 6, D), lambda b, i: (b, 0, 0)),
            pl.BlockSpec((1, 6, D), lambda b, i: (0, 0, 0)),
            pl.BlockSpec((1, D), const), pl.BlockSpec((1, D), const),
            pl.BlockSpec((D, 2 * D), const), pl.BlockSpec((1, 2 * D), const),
            pl.BlockSpec((CONV_WIDTH, D), const), pl.BlockSpec((1, D), const),
            pl.BlockSpec((1, D), const), pl.BlockSpec((1, D), const),
            pl.BlockSpec((D, D), const), pl.BlockSpec((1, D), const),
            pl.BlockSpec((N_EXPERTS, D), const), pl.BlockSpec((N_EXPERTS, 1), const),
        ],
        out_specs=(pl.BlockSpec((1, ts, D), lat),
                   pl.BlockSpec((1, ts, D), lambda b, i: (b, 0, 0)),
                   pl.BlockSpec((ts, ROW_W), row_blk),
                   pl.BlockSpec((1, ts), lambda b, i: (0, b * (nt + 1) + i))),
        scratch_shapes=[pltpu.VMEM((ts + 2 * HALO, D), F32), pltpu.VMEM((ts, D), F32), pltpu.VMEM((ts, D), BF16)],
        compiler_params=_cparams(("arbitrary", "arbitrary")),
        name="conv_layer",
    )(xs, cs, x4, x4, mod_lat, mod_ctx, n1g, n2g, w1, b1, wdw, bdw, lng, lnb, w2, b2, rwt, rb)


def _plan_kernel(cls_ref, rank_ref, cnt_ref, carry_scr, *, tb):
    @pl.when(pl.program_id(0) == 0)
    def _():
        carry_scr[...] = jnp.zeros_like(carry_scr)

    cls = cls_ref[...]
    onehot = (lax.broadcasted_iota(jnp.int32, (32, tb), 0) == cls).astype(F32)
    upper = (lax.broadcasted_iota(jnp.int32, (tb, tb), 0) <= lax.broadcasted_iota(jnp.int32, (tb, tb), 1))
    prefix = jnp.dot(onehot.astype(BF16), upper.astype(BF16), preferred_element_type=F32)
    carry = carry_scr[:, 0:1]
    rank = jnp.sum(onehot * (prefix + carry), axis=0, keepdims=True) - 1.0
    rank_ref[...] = rank.astype(jnp.int32)
    total = carry + jnp.sum(onehot, axis=1, keepdims=True)
    carry_scr[...] = jnp.broadcast_to(total, carry_scr.shape)
    cnt_ref[...] = jnp.broadcast_to(total, cnt_ref.shape)


def _plan(cls, *, tb=512):
    t_all = cls.shape[1]
    return pl.pallas_call(
        functools.partial(_plan_kernel, tb=tb),
        out_shape=(jax.ShapeDtypeStruct((1, t_all), jnp.int32), jax.ShapeDtypeStruct((32, LANES), F32)),
        grid=(t_all // tb,),
        in_specs=[pl.BlockSpec((1, tb), lambda i: (0, i))],
        out_specs=(pl.BlockSpec((1, tb), lambda i: (0, i)), pl.BlockSpec((32, LANES), lambda i: (0, 0))),
        scratch_shapes=[pltpu.VMEM((32, LANES), F32)],
        compiler_params=_cparams(("arbitrary",)),
        name="route_plan",
    )(cls)


def _invert_kernel(pos_ref, src_ref, *, n_tok, n_slot):
    unroll = 8

    def zero(i, carry):
        for u in range(unroll):
            src_ref[i * unroll + u] = 0
        return carry
    lax.fori_loop(0, n_slot // unroll, zero, 0)

    def put(i, carry):
        for u in range(unroll):
            t = i * unroll + u
            src_ref[pos_ref[t]] = t
        return carry
    lax.fori_loop(0, n_tok // unroll, put, 0)


def _invert(pos, n_slot):
    n_tok = pos.shape[0]
    return pl.pallas_call(
        functools.partial(_invert_kernel, n_tok=n_tok, n_slot=n_slot),
        out_shape=jax.ShapeDtypeStruct((n_slot,), jnp.int32),
        in_specs=[pl.BlockSpec(memory_space=pltpu.SMEM)],
        out_specs=pl.BlockSpec(memory_space=pltpu.SMEM),
        name="invert_slots",
    )(pos)


def _dispatch_tables(cls, rank, counts, *, tm):
    t_all = cls.shape[1]
    n_tiles = t_all // tm + N_CLASSES
    cnt = counts[:N_CLASSES, 0].astype(jnp.int32)
    padded = ((cnt + tm - 1) // tm) * tm
    ends = jnp.cumsum(padded)
    offs = ends - padded
    pos = offs[cls[0]] + rank[0]
    src = _invert(pos, n_tiles * tm)
    n_used = ends[-1] // tm
    tile_start = jnp.minimum(jnp.arange(n_tiles, dtype=jnp.int32), n_used - 1) * tm
    tile_cls = jnp.minimum(jnp.sum(ends[None, :] <= tile_start[:, None], axis=1), N_CLASSES - 1).astype(jnp.int32)
    pair_lo = jnp.array([0, 0, 0, 1, 1, 2], jnp.int32)
    pair_hi = jnp.array([1, 2, 3, 2, 3, 3], jnp.int32)
    grp = tile_cls // N_PAIRS
    ea = EXPERTS_PER_GROUP * grp + pair_lo[tile_cls % N_PAIRS]
    eb = EXPERTS_PER_GROUP * grp + pair_hi[tile_cls % N_PAIRS]
    return pos, src, ea, eb, n_used.reshape(1).astype(jnp.int32)


def _start_row_gather(idx_ref, idx_base, src_hbm, dst, sem, n, unroll):
    def one(r):
        row = idx_ref[idx_base + r]
        pltpu.make_async_copy(src_hbm.at[pl.ds(row, 1), :], dst.at[pl.ds(r, 1), :], sem).start()

    if unroll:
        for r in range(n):
            one(r)
    else:
        def body(r, carry):
            one(r)
            return carry
        lax.fori_loop(0, n, body, 0)


def _wait_row_gather(src_hbm, dst, sem, n):
    pltpu.make_async_copy(src_hbm.at[pl.ds(0, n), :], dst, sem).wait()


def _moe_kernel(src_ref, ea_ref, eb_ref, nused_ref, hrow_ref, wga_ref, wua_ref, wda_ref, wgb_ref, wub_ref, wdb_ref,
                ys_ref, gbuf, hbuf, gates, sem, *, tm):
    j = pl.program_id(0)
    nused = nused_ref[0]
    slot = j % 2

    @pl.when(j == 0)
    def _():
        _start_row_gather(src_ref, 0, hrow_ref, gbuf.at[0], sem.at[0], tm, unroll=False)

    @pl.when(j < nused)
    def _():
        _wait_row_gather(hrow_ref, gbuf.at[slot], sem.at[slot], tm)
        hbuf[...] = gbuf[slot, :, :D].astype(BF16)
        gates[...] = gbuf[slot, :, D:]
        nxt = jnp.minimum(j + 1, nused - 1)
        _start_row_gather(src_ref, nxt * tm, hrow_ref, gbuf.at[1 - slot], sem.at[1 - slot], tm, unroll=True)
        h = hbuf[...]
        ga = gates[:, 0:1]
        gb = gates[:, 1:2]

        def expert(wg, wu, wd):
            a = jnp.dot(h, wg[0], preferred_element_type=F32)
            u = jnp.dot(h, wu[0], preferred_element_type=F32)
            act = (_silu(a) * u).astype(BF16)
            return jnp.dot(act, wd[0], preferred_element_type=F32)

        ys_ref[...] = ga * expert(wga_ref, wua_ref, wda_ref) + gb * expert(wgb_ref, wub_ref, wdb_ref)

    @pl.when(j == nused - 1)
    def _():
        _wait_row_gather(hrow_ref, gbuf.at[1 - slot], sem.at[1 - slot], tm)

    @pl.when(j >= nused)
    def _():
        ys_ref[...] = jnp.zeros_like(ys_ref)


def _moe(hrow, src, ea, eb, nused, wg, wu, wd, *, tm):
    n_tiles = ea.shape[0]
    f = wg.shape[-1]
    wa = lambda j, src, ea, eb, nu: (ea[j], 0, 0)
    wb = lambda j, src, ea, eb, nu: (eb[j], 0, 0)
    return pl.pallas_call(
        functools.partial(_moe_kernel, tm=tm),
        out_shape=jax.ShapeDtypeStruct((n_tiles * tm, D), F32),
        grid_spec=pltpu.PrefetchScalarGridSpec(
            num_scalar_prefetch=4,
            grid=(n_tiles,),
            in_specs=[
                pl.BlockSpec(memory_space=pl.ANY),
                pl.BlockSpec((1, D, f), wa), pl.BlockSpec((1, D, f), wa), pl.BlockSpec((1, f, D), wa),
                pl.BlockSpec((1, D, f), wb), pl.BlockSpec((1, D, f), wb), pl.BlockSpec((1, f, D), wb),
            ],
            out_specs=pl.BlockSpec((tm, D), lambda j, src, ea, eb, nu: (j, 0)),
            scratch_shapes=[pltpu.VMEM((2, tm, ROW_W), F32), pltpu.VMEM((tm, D), BF16), pltpu.VMEM((tm, AUX), F32),
                            pltpu.SemaphoreType.DMA((2,))],
        ),
        compiler_params=_cparams(("arbitrary",), disable_bounds_checks=True),
        name="moe_experts",
    )(src, ea, eb, nused, hrow, wg, wu, wd, wg, wu, wd)


def _combine_base(step, *, ts, tok_stride, tok_off):
    nt = pl.num_programs(1)
    return (step // nt) * tok_stride + tok_off + (step % nt) * ts


def _combine_wait(pos_ref, ys_ref, ybuf, sem, *, ts, tok_stride, tok_off):
    n = pl.program_id(0) * pl.num_programs(1) + pl.program_id(1)

    @pl.when(n == 0)
    def _():
        _start_row_gather(pos_ref, _combine_base(0, ts=ts, tok_stride=tok_stride, tok_off=tok_off), ys_ref,
                          ybuf.at[0], sem.at[0], ts, unroll=False)

    _wait_row_gather(ys_ref, ybuf.at[n % 2], sem.at[n % 2], ts)
    return ybuf[n % 2]


def _combine_prefetch(pos_ref, ys_ref, ybuf, sem, *, ts, tok_stride, tok_off):
    nt = pl.num_programs(1)
    n = pl.program_id(0) * nt + pl.program_id(1)
    nxt = jnp.minimum(n + 1, pl.num_programs(0) * nt - 1)
    _start_row_gather(pos_ref, _combine_base(nxt, ts=ts, tok_stride=tok_stride, tok_off=tok_off), ys_ref,
                      ybuf.at[1 - n % 2], sem.at[1 - n % 2], ts, unroll=True)


def _combine_drain(ys_ref, ybuf, sem, ts):
    nt = pl.num_programs(1)
    n = pl.program_id(0) * nt + pl.program_id(1)

    @pl.when(n == pl.num_programs(0) * nt - 1)
    def _():
        _wait_row_gather(ys_ref, ybuf.at[1 - n % 2], sem.at[1 - n % 2], ts)


def _rope(v, cos, sin):
    lane = lax.broadcasted_iota(jnp.int32, v.shape, 1)
    first = (lane % (2 * ROPE_PAIRS)) < ROPE_PAIRS
    partner = jnp.where(first, pltpu.roll(v, LANES - ROPE_PAIRS, 1), pltpu.roll(v, ROPE_PAIRS, 1))
    return v * cos + partner * sin


def _qkv_kernel(pos_ref, x_ref, ys_ref, modp_ref, mod_ref, n1g_ref, w_ref, cq_ref, sq_ref, ck_ref, sk_ref,
                x_out_ref, q_ref, kd_ref, vd_ref, ybuf, sem, *, ts, seq, tok_stride, tok_off):
    where = dict(ts=ts, tok_stride=tok_stride, tok_off=tok_off)
    y = _combine_wait(pos_ref, ys_ref, ybuf, sem, **where)
    g2_prev = modp_ref[0][5:6, :]
    x_out_ref[0] = x_ref[0] + g2_prev * y
    _combine_prefetch(pos_ref, ys_ref, ybuf, sem, **where)
    x2 = x_out_ref[0]
    mod = mod_ref[0]
    sh1 = mod[0:1, :]
    gm1 = n1g_ref[...] * (1.0 + mod[1:2, :])
    hb = _rms_mod(x2, gm1, sh1).astype(BF16)
    cq, sq, ck, sk = cq_ref[...], sq_ref[...], ck_ref[...], sk_ref[...]
    for c0 in range(0, D, LANES):
        qc = jnp.dot(hb, w_ref[:, c0:c0 + LANES], preferred_element_type=F32)
        q_ref[0, :, c0:c0 + LANES] = _rope(qc, cq, sq).astype(BF16)
    for c0 in range(0, 2 * KV_DIM, LANES):
        kc = jnp.dot(hb, w_ref[:, D + c0:D + c0 + LANES], preferred_element_type=F32)
        kd_ref[0, :, c0:c0 + LANES] = _rope(kc, ck, sk).astype(BF16)
    vd_ref[0] = jnp.dot(hb, w_ref[:, D + 2 * KV_DIM:], preferred_element_type=F32).astype(BF16)
    _combine_drain(ys_ref, ybuf, sem, ts)


def _qkv(pos, xs, ys, modp, mod, n1g, w_ext, cq, sq, ck, sk, *, ts, tok_stride, tok_off):
    bsz, s, _ = xs.shape
    nt = s // ts
    per_batch = mod.shape[0] > 1
    mod_map = (lambda b, i, p: (b, 0, 0)) if per_batch else (lambda b, i, p: (0, 0, 0))
    const = lambda b, i, p: (0, 0)
    tok = lambda b, i, p: (b, i, 0)
    tab = lambda b, i, p: (i, 0)
    return pl.pallas_call(
        functools.partial(_qkv_kernel, ts=ts, seq=s, tok_stride=tok_stride, tok_off=tok_off),
        out_shape=(jax.ShapeDtypeStruct((bsz, s, D), F32), jax.ShapeDtypeStruct((bsz, s, D), BF16),
                   jax.ShapeDtypeStruct((bsz, s, 2 * KV_DIM), BF16), jax.ShapeDtypeStruct((bsz, s, 2 * KV_DIM), BF16)),
        grid_spec=pltpu.PrefetchScalarGridSpec(
            num_scalar_prefetch=1,
            grid=(bsz, nt),
            in_specs=[
                pl.BlockSpec((1, ts, D), tok),
                pl.BlockSpec(memory_space=pl.ANY),
                pl.BlockSpec((1, 6, D), mod_map), pl.BlockSpec((1, 6, D), mod_map),
                pl.BlockSpec((1, D), const),
                pl.BlockSpec((D, D + 4 * KV_DIM), const),
                pl.BlockSpec((ts, LANES), tab), pl.BlockSpec((ts, LANES), tab),
                pl.BlockSpec((ts, LANES), tab), pl.BlockSpec((ts, LANES), tab),
            ],
            out_specs=(pl.BlockSpec((1, ts, D), tok), pl.BlockSpec((1, ts, D), tok),
                       pl.BlockSpec((1, ts, 2 * KV_DIM), tok), pl.BlockSpec((1, ts, 2 * KV_DIM), tok)),
            scratch_shapes=[pltpu.VMEM((2, ts, D), F32), pltpu.SemaphoreType.DMA((2,))],
        ),
        compiler_params=_cparams(("arbitrary", "arbitrary"), disable_bounds_checks=True),
        name="combine_qkv",
    )(pos, xs, ys, modp, mod, n1g, w_ext, cq, sq, ck, sk)


def _attn_kernel(q_ref, kp_ref, kc_ref, kn_ref, vp_ref, vc_ref, vn_ref, kx_ref, vx_ref, sink_ref, o_ref, *, lc):
    i = pl.program_id(1)
    nb = pl.num_programs(1)
    nband = 3 * QBLK
    qrow = lax.broadcasted_iota(jnp.int32, (GROUP * QBLK, nband), 0) % QBLK
    kcol = lax.broadcasted_iota(jnp.int32, (GROUP * QBLK, nband), 1)
    lo_lim = jnp.where(i > 0, 0, QBLK)
    hi_lim = jnp.where(i < nb - 1, nband, 2 * QBLK - 1)
    ok_band = (kcol >= jnp.maximum(qrow, lo_lim)) & (kcol <= jnp.minimum(qrow + 2 * WINDOW, hi_lim))
    lane = lax.broadcasted_iota(jnp.int32, (1, LANES), 1)
    low = lane < HEAD_DIM
    for h in range(N_KV_HEADS):
        hs = slice(h * LANES, (h + 1) * LANES)
        k_all = jnp.concatenate([kp_ref[0, :, hs], kc_ref[0, :, hs], kn_ref[0, :, hs], kx_ref[0, :, hs]], axis=0)
        v_all = jnp.concatenate([vp_ref[0, :, hs], vc_ref[0, :, hs], vn_ref[0, :, hs], vx_ref[0, :, hs]], axis=0)
        v_lo = jnp.where(low, v_all, jnp.zeros_like(v_all))
        v_hi = jnp.where(low, jnp.zeros_like(v_all), v_all)
        qs = []
        for j in range(GROUP):
            c0 = h * GROUP * HEAD_DIM + (j // 2) * LANES
            qh = q_ref[0, :, c0:c0 + LANES]
            keep = low if j % 2 == 0 else jnp.logical_not(low)
            qs.append(jnp.where(keep, qh, jnp.zeros_like(qh)))
        qm = jnp.concatenate(qs, axis=0)
        s = lax.dot_general(qm, k_all, (((1,), (1,)), ((), ())), preferred_element_type=F32)
        s_band = jnp.where(ok_band, s[:, :nband], NEG_INF)
        s_ctx = s[:, nband:]
        sink = jnp.concatenate(
            [jnp.broadcast_to(sink_ref[:, h * GROUP + j:h * GROUP + j + 1], (QBLK, 1)) for j in range(GROUP)], axis=0)
        m = jnp.maximum(jnp.maximum(jnp.max(s_band, axis=1, keepdims=True), jnp.max(s_ctx, axis=1, keepdims=True)),
                        sink)
        p_band = jnp.exp(s_band - m)
        p_ctx = jnp.exp(s_ctx - m)
        den = jnp.sum(p_band, axis=1, keepdims=True) + jnp.sum(p_ctx, axis=1, keepdims=True) + jnp.exp(sink - m)
        inv = 1.0 / den
        p = jnp.concatenate([p_band, p_ctx], axis=1).astype(BF16)
        for pr in range(GROUP // 2):
            r0 = 2 * pr * QBLK
            o = (jnp.dot(p[r0:r0 + QBLK], v_lo, preferred_element_type=F32) * inv[r0:r0 + QBLK]
                 + jnp.dot(p[r0 + QBLK:r0 + 2 * QBLK], v_hi, preferred_element_type=F32) * inv[r0 + QBLK:r0 + 2 * QBLK])
            c0 = h * GROUP * HEAD_DIM + pr * LANES
            o_ref[0, :, c0:c0 + LANES] = o.astype(BF16)


def _attention(q, kd, vd, kxd, vxd, sink):
    bsz, s, _ = q.shape
    lc = kxd.shape[1]
    nb = s // QBLK
    cur = lambda b, i: (b, i, 0)
    prev = lambda b, i: (b, jnp.maximum(i - 1, 0), 0)
    nxt = lambda b, i: (b, jnp.minimum(i + 1, nb - 1), 0)
    ctx = lambda b, i: (b, 0, 0)
    kvw = 2 * KV_DIM
    return pl.pallas_call(
        functools.partial(_attn_kernel, lc=lc),
        out_shape=jax.ShapeDtypeStruct((bsz, s, D), BF16),
        grid=(bsz, nb),
        in_specs=[
            pl.BlockSpec((1, QBLK, D), cur),
            pl.BlockSpec((1, QBLK, kvw), prev), pl.BlockSpec((1, QBLK, kvw), cur), pl.BlockSpec((1, QBLK, kvw), nxt),
            pl.BlockSpec((1, QBLK, kvw), prev), pl.BlockSpec((1, QBLK, kvw), cur), pl.BlockSpec((1, QBLK, kvw), nxt),
            pl.BlockSpec((1, lc, kvw), ctx), pl.BlockSpec((1, lc, kvw), ctx),
            pl.BlockSpec((1, N_HEADS), lambda b, i: (0, 0)),
        ],
        out_specs=pl.BlockSpec((1, QBLK, D), cur),
        compiler_params=_cparams(("arbitrary", "arbitrary")),
        name="window_attention",
    )(q, kd, kd, kd, vd, vd, vd, kxd, vxd, sink)


def _attn_out_kernel(a_ref, x_ref, mod_ref, n2g_ref, wo_ref, rwt_ref, rb_ref, x_out_ref, hrow_ref, cls_ref):
    mod = mod_ref[0]
    g1, sh2, sc2 = mod[2:3, :], mod[3:4, :], mod[4:5, :]
    gm2 = n2g_ref[...] * (1.0 + sc2)
    y = jnp.dot(a_ref[0], wo_ref[...], preferred_element_type=F32)
    x_out_ref[0] = _epilogue(x_ref[0], y, g1, sh2, gm2, rwt_ref, rb_ref, hrow_ref, cls_ref)


def _attn_out(attn, xs, mod, n2g, wo, rwt, rb, *, ts):
    bsz, s, _ = xs.shape
    nt = s // ts
    t_all = bsz * s
    const = lambda b, i: (0, 0)
    tok = lambda b, i: (b, i, 0)
    return pl.pallas_call(
        _attn_out_kernel,
        out_shape=(jax.ShapeDtypeStruct((bsz, s, D), F32),
                   jax.ShapeDtypeStruct((t_all, ROW_W), F32),
                   jax.ShapeDtypeStruct((1, t_all), jnp.int32)),
        grid=(bsz, nt),
        in_specs=[
            pl.BlockSpec((1, ts, D), tok), pl.BlockSpec((1, ts, D), tok),
            pl.BlockSpec((1, 6, D), lambda b, i: (b, 0, 0)),
            pl.BlockSpec((1, D), const), pl.BlockSpec((D, D), const),
            pl.BlockSpec((N_EXPERTS, D), const), pl.BlockSpec((N_EXPERTS, 1), const),
        ],
        out_specs=(pl.BlockSpec((1, ts, D), tok),
                   pl.BlockSpec((ts, ROW_W), lambda b, i: (b * nt + i, 0)),
                   pl.BlockSpec((1, ts), lambda b, i: (0, b * nt + i))),
        compiler_params=_cparams(("arbitrary", "arbitrary")),
        name="attn_out",
    )(attn, xs, mod, n2g, wo, rwt, rb)


def _final_kernel(pos_ref, x_ref, ys_ref, mod_ref, fg_ref, o_ref, ybuf, sem, *, ts, seq):
    where = dict(ts=ts, tok_stride=seq, tok_off=0)
    y = _combine_wait(pos_ref, ys_ref, ybuf, sem, **where)
    x4 = x_ref[0] + mod_ref[0][5:6, :] * y
    r = lax.rsqrt(jnp.mean(x4 * x4, axis=-1, keepdims=True) + EPS)
    o_ref[0] = x4 * r * fg_ref[...]
    _combine_prefetch(pos_ref, ys_ref, ybuf, sem, **where)
    _combine_drain(ys_ref, ybuf, sem, ts)


def _final(pos, xs, ys, mod, fg, *, ts):
    bsz, s, _ = xs.shape
    tok = lambda b, i, p: (b, i, 0)
    return pl.pallas_call(
        functools.partial(_final_kernel, ts=ts, seq=s),
        out_shape=jax.ShapeDtypeStruct((bsz, s, D), F32),
        grid_spec=pltpu.PrefetchScalarGridSpec(
            num_scalar_prefetch=1,
            grid=(bsz, s // ts),
            in_specs=[pl.BlockSpec((1, ts, D), tok), pl.BlockSpec(memory_space=pl.ANY),
                      pl.BlockSpec((1, 6, D), lambda b, i, p: (b, 0, 0)), pl.BlockSpec((1, D), lambda b, i, p: (0, 0))],
            out_specs=pl.BlockSpec((1, ts, D), tok),
            scratch_shapes=[pltpu.VMEM((2, ts, D), F32), pltpu.SemaphoreType.DMA((2,))],
        ),
        compiler_params=_cparams(("arbitrary", "arbitrary"), disable_bounds_checks=True),
        name="combine_final",
    )(pos, xs, ys, mod, fg)


def _rope_tables(seq):
    t = jnp.arange(seq, dtype=jnp.int32)
    inv_freq = jnp.power(jnp.float32(ROPE_THETA), -jnp.arange(ROPE_PAIRS, dtype=F32) / ROPE_PAIRS)
    ang_r = (t // GRID_W).astype(F32)[:, None] * inv_freq
    ang_c = (t % GRID_W).astype(F32)[:, None] * inv_freq
    cos = jnp.concatenate([jnp.cos(ang_r)] * 2 + [jnp.cos(ang_c)] * 2, axis=1)
    sin = jnp.concatenate([-jnp.sin(ang_r), jnp.sin(ang_r), -jnp.sin(ang_c), jnp.sin(ang_c)], axis=1)
    cos = jnp.concatenate([cos, cos], axis=1)
    sin = jnp.concatenate([sin, sin], axis=1)
    scale = HEAD_DIM ** -0.5
    return cos * scale, sin * scale, cos, sin


def kernel(x, c, ctx, c_ctx, w_mod, b_mod, norm1_g, norm2_g, conv_w_pw1, conv_b_pw1, conv_w_dw, conv_b_dw,
           conv_ln_g, conv_ln_b, conv_w_pw2, conv_b_pw2, attn_w_qkv, attn_w_o, attn_sink, router_w, router_b,
           moe_w_gate, moe_w_up, moe_w_down, final_g):
    bsz, s, d = x.shape
    lc = ctx.shape[1]
    ts0 = 256
    ts = 512
    tm = 256
    assert d == D and w_mod.shape[0] == 2 and bsz + 1 <= COND_ROWS
    assert lc == ts0 and s % ts == 0 and s % QBLK == 0

    cond = jnp.concatenate([c, c_ctx[None, :], jnp.zeros((COND_ROWS - bsz - 1, D), F32)], axis=0)
    mods = _adaln(cond, w_mod, b_mod).reshape(2, COND_ROWS, 6, D)
    mod_lat = [mods[l, :bsz] for l in range(2)]
    mod_ctx = [mods[l, bsz:bsz + 1] for l in range(2)]

    rwt = router_w.T
    rb = router_b.reshape(N_EXPERTS, 1)
    row = lambda v: v.reshape(1, -1)
    wg = moe_w_gate.astype(BF16)
    wu = moe_w_up.astype(BF16)
    wd = moe_w_down.astype(BF16)

    x1, ctx1, hrow, cls = _conv_layer(
        x, ctx, mod_lat[0], mod_ctx[0], row(norm1_g[0]), row(norm2_g[0]), conv_w_pw1[0].astype(BF16),
        row(conv_b_pw1[0]), conv_w_dw[0], row(conv_b_dw[0]), row(conv_ln_g[0]), row(conv_ln_b[0]),
        conv_w_pw2[0].astype(BF16), row(conv_b_pw2[0]), rwt, rb, ts=ts0)
    rank, counts = _plan(cls)
    pos, src, ea, eb, nused = _dispatch_tables(cls, rank, counts, tm=tm)
    ys = _moe(hrow, src, ea, eb, nused, wg[0], wu[0], wd[0], tm=tm)

    wq = attn_w_qkv[0][:, :D]
    wk = attn_w_qkv[0][:, D:D + KV_DIM].reshape(D, N_KV_HEADS, 1, HEAD_DIM)
    wv = attn_w_qkv[0][:, D + KV_DIM:].reshape(D, N_KV_HEADS, 1, HEAD_DIM)
    dup = lambda w: jnp.broadcast_to(w, (D, N_KV_HEADS, 2, HEAD_DIM)).reshape(D, 2 * KV_DIM)
    w_ext = jnp.concatenate([wq, dup(wk), dup(wv)], axis=1).astype(BF16)
    cq, sq, ck, sk = _rope_tables(s)
    ones = jnp.ones((lc, LANES), F32)
    zeros = jnp.zeros((lc, LANES), F32)
    n1g = row(norm1_g[1])
    x2, q, kd, vd = _qkv(pos, x1, ys, mod_lat[0], mod_lat[1], n1g, w_ext, cq, sq, ck, sk,
                         ts=ts, tok_stride=s + lc, tok_off=0)
    _, _, kxd, vxd = _qkv(pos, ctx1, ys, mod_ctx[0], mod_ctx[1], n1g, w_ext, ones, zeros, ones, zeros,
                          ts=lc, tok_stride=s + lc, tok_off=s)
    attn = _attention(q, kd, vd, kxd, vxd, attn_sink[0].reshape(1, N_HEADS))
    x3, hrow1, cls1 = _attn_out(attn, x2, mod_lat[1], row(norm2_g[1]), attn_w_o[0].astype(BF16), rwt, rb, ts=ts)
    rank1, counts1 = _plan(cls1)
    pos1, src1, ea1, eb1, nused1 = _dispatch_tables(cls1, rank1, counts1, tm=tm)
    ys1 = _moe(hrow1, src1, ea1, eb1, nused1, wg[1], wu[1], wd[1], tm=tm)
    return _final(pos1, x3, ys1, mod_lat[1], row(final_g), ts=ts)
```

```python
import functools
import math

import jax
import jax.numpy as jnp
from jax import lax
from jax.experimental import pallas as pl
from jax.experimental.pallas import tpu as pltpu

D = 1024
GRID_W = 64
CONV_WIDTH = 31
CONV_PAD = CONV_WIDTH // 2
SUBLANES = 8
HALO = 16
HEAD_DIM = 64
N_HEADS = D // HEAD_DIM
N_KV_HEADS = N_HEADS // 4
GROUP = N_HEADS // N_KV_HEADS
KV_DIM = N_KV_HEADS * HEAD_DIM
WINDOW = 128
QBLK = 128
ROPE_THETA = 10000.0
ROPE_PAIRS = HEAD_DIM // 4
N_EXPERTS = 16
N_GROUPS = 4
EXPERTS_PER_GROUP = N_EXPERTS // N_GROUPS
N_PAIRS = 6
N_CLASSES = N_GROUPS * N_PAIRS
D_EXPERT = D // 2
EPS = 1e-6
NEG_INF = -1e30
LOG2E = 1.4426950408889634
LANES = 128
AUX = LANES
ROW_W = D + AUX
COND_ROWS = 16
VMEM_LIMIT = 56 * 1024 * 1024

F32 = jnp.float32
BF16 = jnp.bfloat16
HIGHEST = lax.Precision.HIGHEST


def _cparams(sem, **kw):
    return pltpu.CompilerParams(dimension_semantics=sem, vmem_limit_bytes=VMEM_LIMIT, **kw)


def _sigmoid(v):
    return jax.nn.sigmoid(v)


def _silu(v):
    return v * jax.nn.sigmoid(v)


def _rms_mod(xr, gm, sh):
    r = lax.rsqrt(jnp.mean(xr * xr, axis=-1, keepdims=True) + EPS)
    return xr * r * gm + sh


def _adaln_kernel(cond_ref, w_ref, b_ref, out_ref):
    cnd = cond_ref[...]
    act = _silu(cnd)
    out_ref[0] = jnp.dot(act, w_ref[0], precision=HIGHEST, preferred_element_type=F32) + b_ref[0]


def _adaln(cond, w_mod, b_mod):
    depth = w_mod.shape[0]
    tn = 1536
    return pl.pallas_call(
        _adaln_kernel,
        out_shape=jax.ShapeDtypeStruct((depth, COND_ROWS, 6 * D), F32),
        grid=(depth, 6 * D // tn),
        in_specs=[
            pl.BlockSpec((COND_ROWS, D), lambda l, n: (0, 0)),
            pl.BlockSpec((1, D, tn), lambda l, n: (l, 0, n)),
            pl.BlockSpec((1, 1, tn), lambda l, n: (l, 0, n)),
        ],
        out_specs=pl.BlockSpec((1, COND_ROWS, tn), lambda l, n: (l, 0, n)),
        compiler_params=_cparams(("arbitrary", "arbitrary")),
        name="adaln",
    )(cond, w_mod, b_mod.reshape(depth, 1, 6 * D))


def _route_rows(h2, rwt_ref, rb_ref):
    nt_dims = (((1,), (1,)), ((), ()))
    h_hi = h2.astype(BF16)
    h_lo = (h2 - h_hi.astype(F32)).astype(BF16)
    w_hi = rwt_ref[0]
    w_lo = rwt_ref[1]
    lt = (lax.dot_general(w_hi, h_hi, nt_dims, preferred_element_type=F32)
          + lax.dot_general(w_hi, h_lo, nt_dims, preferred_element_type=F32)
          + lax.dot_general(w_lo, h_hi, nt_dims, preferred_element_type=F32))
    s = _sigmoid(lt)
    sel = s + rb_ref[...]
    srow = [s[e:e + 1, :] for e in range(N_EXPERTS)]
    vrow = [sel[e:e + 1, :] for e in range(N_EXPERTS)]
    gscore = []
    for g in range(N_GROUPS):
        a, b, c, d = vrow[4 * g:4 * g + 4]
        gscore.append(jnp.maximum(jnp.maximum(jnp.maximum(a + b, a + c), jnp.maximum(a + d, b + c)),
                                  jnp.maximum(b + d, c + d)))
    bg = jnp.zeros_like(gscore[0], dtype=jnp.int32)
    best = gscore[0]
    for g in range(1, N_GROUPS):
        better = gscore[g] > best
        bg = jnp.where(better, g, bg)
        best = jnp.where(better, gscore[g], best)
    v = []
    sv = []
    for j in range(EXPERTS_PER_GROUP):
        vj = vrow[j]
        sj = srow[j]
        for g in range(1, N_GROUPS):
            vj = jnp.where(bg == g, vrow[4 * g + j], vj)
            sj = jnp.where(bg == g, srow[4 * g + j], sj)
        v.append(vj)
        sv.append(sj)
    i1 = jnp.zeros_like(bg)
    m1 = v[0]
    for j in range(1, EXPERTS_PER_GROUP):
        gt = v[j] > m1
        i1 = jnp.where(gt, j, i1)
        m1 = jnp.where(gt, v[j], m1)
    v2 = [jnp.where(i1 == j, -jnp.inf, v[j]) for j in range(EXPERTS_PER_GROUP)]
    i2 = jnp.zeros_like(bg)
    m2 = v2[0]
    for j in range(1, EXPERTS_PER_GROUP):
        gt = v2[j] > m2
        i2 = jnp.where(gt, j, i2)
        m2 = jnp.where(gt, v2[j], m2)
    lo = jnp.minimum(i1, i2)
    hi = jnp.maximum(i1, i2)
    s_lo = sv[0]
    s_hi = sv[0]
    for j in range(1, EXPERTS_PER_GROUP):
        s_lo = jnp.where(lo == j, sv[j], s_lo)
        s_hi = jnp.where(hi == j, sv[j], s_hi)
    den = s_lo + s_hi
    pair = jnp.where(lo == 0, 0, jnp.where(lo == 1, 3, 5)) + (hi - lo - 1)
    cls = N_PAIRS * bg + pair
    return s_lo / den, s_hi / den, cls


def _epilogue(x_in, y, g1, sh2, gm2, rwt_ref, rb_ref, hrow_ref, cls_ref):
    x1 = x_in + g1 * y
    h2 = _rms_mod(x1, gm2, sh2)
    ga, gb, cls = _route_rows(h2, rwt_ref, rb_ref)
    ts = h2.shape[0]
    cls_ref[...] = cls
    g8 = jnp.concatenate([ga, gb, jnp.zeros((6, ts), F32)], axis=0)
    g128 = jnp.concatenate([g8, jnp.zeros((AUX - 8, ts), F32)], axis=0)
    hrow_ref[:, :D] = h2
    hrow_ref[:, D:] = jnp.transpose(g128)
    return x1


def _conv_layer_kernel(x_ref, c_ref, xp_ref, xn_ref, modl_ref, modc_ref, n1g_ref, n2g_ref, w1_ref, b1_ref, wdw_ref,
                       bdw_ref, lng_ref, lnb_ref, w2_ref, b2_ref, rwt_ref, rb_ref,
                       x_out_ref, c_out_ref, hrow_ref, cls_ref, u_scr, cv_scr, v_scr, *, ts):
    i = pl.program_id(1)
    nt = pl.num_programs(1) - 1
    is_ctx = jnp.full((1, 1), i, jnp.int32) == nt
    mod = jnp.where(is_ctx, modc_ref[0], modl_ref[0])
    sh1, sc1, g1, sh2, sc2 = (mod[k:k + 1, :] for k in range(5))
    gm1 = n1g_ref[...] * (1.0 + sc1)
    gm2 = n2g_ref[...] * (1.0 + sc2)
    cw = 512

    def glu(hb, c0):
        a = jnp.dot(hb, w1_ref[:, c0:c0 + cw], preferred_element_type=F32) + b1_ref[:, c0:c0 + cw]
        g = jnp.dot(hb, w1_ref[:, D + c0:D + c0 + cw], preferred_element_type=F32) + b1_ref[:, D + c0:D + c0 + cw]
        return a * _sigmoid(g)

    xm = jnp.where(is_ctx, c_ref[0], x_ref[0])
    hb_main = _rms_mod(xm, gm1, sh1).astype(BF16)
    halo = jnp.concatenate([xp_ref[0, 0], xn_ref[0, 0]], axis=0)
    hb_halo = _rms_mod(halo, gm1, sh1).astype(BF16)
    keep_prev = jnp.logical_and(i > 0, i < nt).astype(F32)
    keep_next = (i < nt - 1).astype(F32)
    for c0 in range(0, D, cw):
        u_scr[HALO:HALO + ts, c0:c0 + cw] = glu(hb_main, c0)
        uh = glu(hb_halo, c0)
        u_scr[0:HALO, c0:c0 + cw] = uh[:HALO] * keep_prev
        u_scr[HALO + ts:2 * HALO + ts, c0:c0 + cw] = uh[HALO:] * keep_next

    rr = 64
    taps_off = HALO - CONV_PAD

    def conv_chunk(rc, carry):
        r0 = pl.multiple_of(rc * rr, rr)
        for c0 in range(0, D, LANES):
            win = u_scr[pl.ds(r0, rr + 2 * HALO), c0:c0 + LANES]
            out = None
            for r in range(SUBLANES):
                nrow = rr + SUBLANES if r else rr
                part = None
                for q in range((CONV_WIDTH + taps_off) // SUBLANES + 1):
                    k = SUBLANES * q + r - taps_off
                    if 0 <= k < CONV_WIDTH:
                        term = win[SUBLANES * q:SUBLANES * q + nrow, :] * wdw_ref[k:k + 1, c0:c0 + LANES]
                        part = term if part is None else part + term
                shifted = part[r:r + rr, :]
                out = shifted if out is None else out + shifted
            cv_scr[pl.ds(r0, rr), c0:c0 + LANES] = out
        cv = cv_scr[pl.ds(r0, rr), :] + bdw_ref[...]
        mu = jnp.mean(cv, axis=-1, keepdims=True)
        dv = cv - mu
        var = jnp.mean(dv * dv, axis=-1, keepdims=True)
        ln = dv * lax.rsqrt(var + EPS) * lng_ref[...] + lnb_ref[...]
        v_scr[pl.ds(r0, rr), :] = _silu(ln).astype(BF16)
        return carry

    lax.fori_loop(0, ts // rr, conv_chunk, 0)
    y = jnp.dot(v_scr[...], w2_ref[...], preferred_element_type=F32) + b2_ref[...]
    x1 = _epilogue(xm, y, g1, sh2, gm2, rwt_ref, rb_ref, hrow_ref, cls_ref)

    @pl.when(i < nt)
    def _():
        x_out_ref[0] = x1

    @pl.when(i == nt)
    def _():
        c_out_ref[0] = x1


def _conv_layer(xs, cs, mod_lat, mod_ctx, n1g, n2g, w1, b1, wdw, bdw, lng, lnb, w2, b2, rwt, rb, *, ts):
    bsz, s, _ = xs.shape
    assert cs.shape[1] == ts
    nt = s // ts
    nh = s // HALO
    hpt = ts // HALO
    x4 = xs.reshape(bsz, nh, HALO, D)
    t_all = bsz * (s + ts)
    const = lambda b, i: (0, 0)
    lat = lambda b, i: (b, jnp.minimum(i, nt - 1), 0)
    row_blk = lambda b, i: (b * (nt + 1) + i, 0)
    return pl.pallas_call(
        functools.partial(_conv_layer_kernel, ts=ts),
        out_shape=(jax.ShapeDtypeStruct((bsz, s, D), F32),
                   jax.ShapeDtypeStruct((bsz, ts, D), F32),
                   jax.ShapeDtypeStruct((t_all, ROW_W), F32),
                   jax.ShapeDtypeStruct((1, t_all), jnp.int32)),
        grid=(bsz, nt + 1),
        in_specs=[
            pl.BlockSpec((1, ts, D), lat),
            pl.BlockSpec((1, ts, D), lambda b, i: (b, 0, 0)),
            pl.BlockSpec((1, 1, HALO, D), lambda b, i: (b, jnp.clip(i * hpt - 1, 0, nh - 1), 0, 0)),
            pl.BlockSpec((1, 1, HALO, D), lambda b, i: (b, jnp.minimum((i + 1) * hpt, nh - 1), 0, 0)),
            pl.BlockSpec((1, 6, D), lambda b, i: (b, 0, 0)),
            pl.BlockSpec((1, 6, D), lambda b, i: (0, 0, 0)),
            pl.BlockSpec((1, D), const), pl.BlockSpec((1, D), const),
            pl.BlockSpec((D, 2 * D), const), pl.BlockSpec((1, 2 * D), const),
            pl.BlockSpec((CONV_WIDTH, D), const), pl.BlockSpec((1, D), const),
            pl.BlockSpec((1, D), const), pl.BlockSpec((1, D), const),
            pl.BlockSpec((D, D), const), pl.BlockSpec((1, D), const),
            pl.BlockSpec((2, N_EXPERTS, D), lambda b, i: (0, 0, 0)), pl.BlockSpec((N_EXPERTS, 1), const),
        ],
        out_specs=(pl.BlockSpec((1, ts, D), lat),
                   pl.BlockSpec((1, ts, D), lambda b, i: (b, 0, 0)),
                   pl.BlockSpec((ts, ROW_W), row_blk),
                   pl.BlockSpec((1, ts), lambda b, i: (0, b * (nt + 1) + i))),
        scratch_shapes=[pltpu.VMEM((ts + 2 * HALO, D), F32), pltpu.VMEM((ts, D), F32), pltpu.VMEM((ts, D), BF16)],
        compiler_params=_cparams(("arbitrary", "arbitrary")),
        name="conv_layer",
    )(xs, cs, x4, x4, mod_lat, mod_ctx, n1g, n2g, w1, b1, wdw, bdw, lng, lnb, w2, b2, rwt, rb)


def _plan_kernel(cls_ref, rank_ref, cnt_ref, carry_scr, *, tb):
    @pl.when(pl.program_id(0) == 0)
    def _():
        carry_scr[...] = jnp.zeros_like(carry_scr)

    cls = cls_ref[...]
    onehot = (lax.broadcasted_iota(jnp.int32, (32, tb), 0) == cls).astype(F32)
    upper = (lax.broadcasted_iota(jnp.int32, (tb, tb), 0) <= lax.broadcasted_iota(jnp.int32, (tb, tb), 1))
    prefix = jnp.dot(onehot.astype(BF16), upper.astype(BF16), preferred_element_type=F32)
    carry = carry_scr[:, 0:1]
    rank = jnp.sum(onehot * (prefix + carry), axis=0, keepdims=True) - 1.0
    rank_ref[...] = rank.astype(jnp.int32)
    total = carry + jnp.sum(onehot, axis=1, keepdims=True)
    carry_scr[...] = jnp.broadcast_to(total, carry_scr.shape)
    cnt_ref[...] = jnp.broadcast_to(total, cnt_ref.shape)


def _plan(cls):
    t_all = cls.shape[1]
    tb = math.gcd(t_all, 1024)
    return pl.pallas_call(
        functools.partial(_plan_kernel, tb=tb),
        out_shape=(jax.ShapeDtypeStruct((1, t_all), jnp.int32), jax.ShapeDtypeStruct((32, LANES), F32)),
        grid=(t_all // tb,),
        in_specs=[pl.BlockSpec((1, tb), lambda i: (0, i))],
        out_specs=(pl.BlockSpec((1, tb), lambda i: (0, i)), pl.BlockSpec((32, LANES), lambda i: (0, 0))),
        scratch_shapes=[pltpu.VMEM((32, LANES), F32)],
        compiler_params=_cparams(("arbitrary",)),
        name="route_plan",
    )(cls)


def _invert_kernel(pos_ref, src_ref, *, n_tok, n_slot):
    unroll = 8

    def zero(i, carry):
        for u in range(unroll):
            src_ref[i * unroll + u] = 0
        return carry
    lax.fori_loop(0, n_slot // unroll, zero, 0)

    def put(i, carry):
        for u in range(unroll):
            t = i * unroll + u
            src_ref[pos_ref[t]] = t
        return carry
    lax.fori_loop(0, n_tok // unroll, put, 0)


def _invert(pos, n_slot):
    n_tok = pos.shape[0]
    return pl.pallas_call(
        functools.partial(_invert_kernel, n_tok=n_tok, n_slot=n_slot),
        out_shape=jax.ShapeDtypeStruct((n_slot,), jnp.int32),
        in_specs=[pl.BlockSpec(memory_space=pltpu.SMEM)],
        out_specs=pl.BlockSpec(memory_space=pltpu.SMEM),
        name="invert_slots",
    )(pos)


def _dispatch_tables(cls, rank, counts, *, tm):
    t_all = cls.shape[1]
    n_tiles = t_all // tm + N_CLASSES
    cnt = counts[:N_CLASSES, 0].astype(jnp.int32)
    padded = ((cnt + tm - 1) // tm) * tm
    ends = jnp.cumsum(padded)
    offs = ends - padded
    pos = offs[cls[0]] + rank[0]
    src = _invert(pos, n_tiles * tm)
    n_used = ends[-1] // tm
    tile_start = jnp.minimum(jnp.arange(n_tiles, dtype=jnp.int32), n_used - 1) * tm
    tile_cls = jnp.minimum(jnp.sum(ends[None, :] <= tile_start[:, None], axis=1), N_CLASSES - 1).astype(jnp.int32)
    pair_lo = jnp.array([0, 0, 0, 1, 1, 2], jnp.int32)
    pair_hi = jnp.array([1, 2, 3, 2, 3, 3], jnp.int32)
    grp = tile_cls // N_PAIRS
    ea = EXPERTS_PER_GROUP * grp + pair_lo[tile_cls % N_PAIRS]
    eb = EXPERTS_PER_GROUP * grp + pair_hi[tile_cls % N_PAIRS]
    return pos, src, ea, eb, n_used.reshape(1).astype(jnp.int32)


def _start_row_gather(idx_ref, idx_base, src_hbm, dst, sem, n, unroll):
    def one(r):
        row = idx_ref[idx_base + r]
        pltpu.make_async_copy(src_hbm.at[pl.ds(row, 1), :], dst.at[pl.ds(r, 1), :], sem).start()

    if unroll:
        for r in range(n):
            one(r)
    else:
        def body(r, carry):
            one(r)
            return carry
        lax.fori_loop(0, n, body, 0)


def _wait_row_gather(src_hbm, dst, sem, n):
    pltpu.make_async_copy(src_hbm.at[pl.ds(0, n), :], dst, sem).wait()


def _moe_kernel(src_ref, ea_ref, eb_ref, nused_ref, hrow_ref, wga_ref, wua_ref, wda_ref, wgb_ref, wub_ref, wdb_ref,
                ys_ref, gbuf, sem, *, tm):
    j = pl.program_id(0)
    nused = nused_ref[0]
    slot = j % 2

    @pl.when(j == 0)
    def _():
        _start_row_gather(src_ref, 0, hrow_ref, gbuf.at[0], sem.at[0], tm, unroll=False)

    @pl.when(j < nused)
    def _():
        nxt = jnp.minimum(j + 1, nused - 1)
        _start_row_gather(src_ref, nxt * tm, hrow_ref, gbuf.at[1 - slot], sem.at[1 - slot], tm, unroll=True)
        _wait_row_gather(hrow_ref, gbuf.at[slot], sem.at[slot], tm)
        h = gbuf[slot, :, :D].astype(BF16)
        ga = gbuf[slot, :, D:D + 1]
        gb = gbuf[slot, :, D + 1:D + 2]

        def expert(wg, wu, wd):
            a = jnp.dot(h, wg[0], preferred_element_type=F32)
            u = jnp.dot(h, wu[0], preferred_element_type=F32)
            act = (_silu(a) * u).astype(BF16)
            return jnp.dot(act, wd[0], preferred_element_type=F32)

        ys_ref[...] = ga * expert(wga_ref, wua_ref, wda_ref) + gb * expert(wgb_ref, wub_ref, wdb_ref)

    @pl.when(j == nused - 1)
    def _():
        _wait_row_gather(hrow_ref, gbuf.at[1 - slot], sem.at[1 - slot], tm)

    @pl.when(j >= nused)
    def _():
        ys_ref[...] = jnp.zeros_like(ys_ref)


def _moe(hrow, src, ea, eb, nused, wg, wu, wd, *, tm, layer):
    n_tiles = ea.shape[0]
    f = wg.shape[-1]
    e0 = layer * N_EXPERTS
    wa = lambda j, src, ea, eb, nu: (e0 + ea[j], 0, 0)
    wb = lambda j, src, ea, eb, nu: (e0 + eb[j], 0, 0)
    return pl.pallas_call(
        functools.partial(_moe_kernel, tm=tm),
        out_shape=jax.ShapeDtypeStruct((n_tiles * tm, D), F32),
        grid_spec=pltpu.PrefetchScalarGridSpec(
            num_scalar_prefetch=4,
            grid=(n_tiles,),
            in_specs=[
                pl.BlockSpec(memory_space=pl.ANY),
                pl.BlockSpec((1, D, f), wa), pl.BlockSpec((1, D, f), wa), pl.BlockSpec((1, f, D), wa),
                pl.BlockSpec((1, D, f), wb), pl.BlockSpec((1, D, f), wb), pl.BlockSpec((1, f, D), wb),
            ],
            out_specs=pl.BlockSpec((tm, D), lambda j, src, ea, eb, nu: (j, 0)),
            scratch_shapes=[pltpu.VMEM((2, tm, ROW_W), F32), pltpu.SemaphoreType.DMA((2,))],
        ),
        compiler_params=_cparams(("arbitrary",), disable_bounds_checks=True),
        name="moe_experts",
    )(src, ea, eb, nused, hrow, wg, wu, wd, wg, wu, wd)


def _combine_base(step, *, ts, tok_stride, tok_off):
    nt = pl.num_programs(1)
    return (step // nt) * tok_stride + tok_off + (step % nt) * ts


def _combine_gather(pos_ref, ys_ref, ybuf, sem, *, ts, tok_stride, tok_off):
    nt = pl.num_programs(1)
    n = pl.program_id(0) * nt + pl.program_id(1)

    @pl.when(n == 0)
    def _():
        _start_row_gather(pos_ref, _combine_base(0, ts=ts, tok_stride=tok_stride, tok_off=tok_off), ys_ref,
                          ybuf.at[0], sem.at[0], ts, unroll=False)

    nxt = jnp.minimum(n + 1, pl.num_programs(0) * nt - 1)
    _start_row_gather(pos_ref, _combine_base(nxt, ts=ts, tok_stride=tok_stride, tok_off=tok_off), ys_ref,
                      ybuf.at[1 - n % 2], sem.at[1 - n % 2], ts, unroll=True)
    _wait_row_gather(ys_ref, ybuf.at[n % 2], sem.at[n % 2], ts)
    return ybuf[n % 2]


def _combine_drain(ys_ref, ybuf, sem, ts):
    nt = pl.num_programs(1)
    n = pl.program_id(0) * nt + pl.program_id(1)

    @pl.when(n == pl.num_programs(0) * nt - 1)
    def _():
        _wait_row_gather(ys_ref, ybuf.at[1 - n % 2], sem.at[1 - n % 2], ts)


def _rope(v, cos, sin):
    lane = lax.broadcasted_iota(jnp.int32, v.shape, 1)
    first = (lane % (2 * ROPE_PAIRS)) < ROPE_PAIRS
    partner = jnp.where(first, pltpu.roll(v, LANES - ROPE_PAIRS, 1), pltpu.roll(v, ROPE_PAIRS, 1))
    return v * cos + partner * sin


def _qkv_kernel(pos_ref, x_ref, ys_ref, modp_ref, mod_ref, n1g_ref, w_ref, cq_ref, sq_ref, ck_ref, sk_ref,
                x_out_ref, q_ref, kd_ref, vd_ref, ybuf, sem, *, ts, seq, tok_stride, tok_off):
    y = _combine_gather(pos_ref, ys_ref, ybuf, sem, ts=ts, tok_stride=tok_stride, tok_off=tok_off)
    g2_prev = modp_ref[0][5:6, :]
    x2 = x_ref[0] + g2_prev * y
    x_out_ref[0] = x2
    mod = mod_ref[0]
    sh1 = mod[0:1, :]
    gm1 = n1g_ref[...] * (1.0 + mod[1:2, :])
    hb = _rms_mod(x2, gm1, sh1).astype(BF16)
    cq, sq, ck, sk = cq_ref[...], sq_ref[...], ck_ref[...], sk_ref[...]
    for c0 in range(0, D, LANES):
        qc = jnp.dot(hb, w_ref[:, c0:c0 + LANES], preferred_element_type=F32)
        q_ref[0, :, c0:c0 + LANES] = _rope(qc, cq, sq).astype(BF16)
    for c0 in range(0, 2 * KV_DIM, LANES):
        kc = jnp.dot(hb, w_ref[:, D + c0:D + c0 + LANES], preferred_element_type=F32)
        kd_ref[0, :, c0:c0 + LANES] = _rope(kc, ck, sk).astype(BF16)
    vd_ref[0] = jnp.dot(hb, w_ref[:, D + 2 * KV_DIM:], preferred_element_type=F32).astype(BF16)
    _combine_drain(ys_ref, ybuf, sem, ts)


def _qkv(pos, xs, ys, modp, mod, n1g, w_ext, cq, sq, ck, sk, *, ts, tok_stride, tok_off):
    bsz, s, _ = xs.shape
    nt = s // ts
    per_batch = mod.shape[0] > 1
    mod_map = (lambda b, i, p: (b, 0, 0)) if per_batch else (lambda b, i, p: (0, 0, 0))
    const = lambda b, i, p: (0, 0)
    tok = lambda b, i, p: (b, i, 0)
    tab = lambda b, i, p: (i, 0)
    return pl.pallas_call(
        functools.partial(_qkv_kernel, ts=ts, seq=s, tok_stride=tok_stride, tok_off=tok_off),
        out_shape=(jax.ShapeDtypeStruct((bsz, s, D), F32), jax.ShapeDtypeStruct((bsz, s, D), BF16),
                   jax.ShapeDtypeStruct((bsz, s, 2 * KV_DIM), BF16), jax.ShapeDtypeStruct((bsz, s, 2 * KV_DIM), BF16)),
        grid_spec=pltpu.PrefetchScalarGridSpec(
            num_scalar_prefetch=1,
            grid=(bsz, nt),
            in_specs=[
                pl.BlockSpec((1, ts, D), tok),
                pl.BlockSpec(memory_space=pl.ANY),
                pl.BlockSpec((1, 6, D), mod_map), pl.BlockSpec((1, 6, D), mod_map),
                pl.BlockSpec((1, D), const),
                pl.BlockSpec((D, D + 4 * KV_DIM), const),
                pl.BlockSpec((ts, LANES), tab), pl.BlockSpec((ts, LANES), tab),
                pl.BlockSpec((ts, LANES), tab), pl.BlockSpec((ts, LANES), tab),
            ],
            out_specs=(pl.BlockSpec((1, ts, D), tok), pl.BlockSpec((1, ts, D), tok),
                       pl.BlockSpec((1, ts, 2 * KV_DIM), tok), pl.BlockSpec((1, ts, 2 * KV_DIM), tok)),
            scratch_shapes=[pltpu.VMEM((2, ts, D), F32), pltpu.SemaphoreType.DMA((2,))],
        ),
        compiler_params=_cparams(("arbitrary", "arbitrary"), disable_bounds_checks=True),
        name="combine_qkv",
    )(pos, xs, ys, modp, mod, n1g, w_ext, cq, sq, ck, sk)


def _attn_kernel(q_ref, kp_ref, kc_ref, kn_ref, vp_ref, vc_ref, vn_ref, kx_ref, vx_ref, sink_ref, o_ref, *, lc):
    i = pl.program_id(1)
    nb = pl.num_programs(1)
    nband = 3 * QBLK
    qrow = lax.broadcasted_iota(jnp.int32, (GROUP * QBLK, nband), 0) % QBLK
    kcol = lax.broadcasted_iota(jnp.int32, (GROUP * QBLK, nband), 1)
    lo_lim = jnp.where(i > 0, 0, QBLK)
    hi_lim = jnp.where(i < nb - 1, nband, 2 * QBLK - 1)
    ok_band = (kcol >= jnp.maximum(qrow, lo_lim)) & (kcol <= jnp.minimum(qrow + 2 * WINDOW, hi_lim))
    lane = lax.broadcasted_iota(jnp.int32, (1, LANES), 1)
    low = lane < HEAD_DIM
    def scores(h):
        hs = slice(h * LANES, (h + 1) * LANES)
        k_all = jnp.concatenate([kp_ref[0, :, hs], kc_ref[0, :, hs], kn_ref[0, :, hs], kx_ref[0, :, hs]], axis=0)
        qs = []
        for j in range(GROUP):
            c0 = h * GROUP * HEAD_DIM + (j // 2) * LANES
            qh = q_ref[0, :, c0:c0 + LANES]
            keep = low if j % 2 == 0 else jnp.logical_not(low)
            qs.append(jnp.where(keep, qh, jnp.zeros_like(qh)))
        qm = jnp.concatenate(qs, axis=0)
        return lax.dot_general(qm, k_all, (((1,), (1,)), ((), ())), preferred_element_type=F32)

    def softmax(h, s):
        s = jnp.concatenate([jnp.where(ok_band, s[:, :nband], NEG_INF), s[:, nband:]], axis=1)
        sink = LOG2E * jnp.concatenate(
            [jnp.broadcast_to(sink_ref[:, h * GROUP + j:h * GROUP + j + 1], (QBLK, 1)) for j in range(GROUP)], axis=0)
        nblk = s.shape[1] // LANES
        smax = s[:, :LANES]
        for kb in range(1, nblk):
            smax = jnp.maximum(smax, s[:, kb * LANES:(kb + 1) * LANES])
        m = jnp.maximum(jnp.max(smax, axis=1, keepdims=True), sink)
        p = jnp.exp2(s - m)
        psum = p[:, :LANES]
        for kb in range(1, nblk):
            psum = psum + p[:, kb * LANES:(kb + 1) * LANES]
        den = jnp.sum(psum, axis=1, keepdims=True) + jnp.exp2(sink - m)
        return p.astype(BF16), 1.0 / den

    def weighted_values(h, p, inv):
        hs = slice(h * LANES, (h + 1) * LANES)
        v_all = jnp.concatenate([vp_ref[0, :, hs], vc_ref[0, :, hs], vn_ref[0, :, hs], vx_ref[0, :, hs]], axis=0)
        v_lo = jnp.where(low, v_all, jnp.zeros_like(v_all))
        v_hi = jnp.where(low, jnp.zeros_like(v_all), v_all)
        for pr in range(GROUP // 2):
            r0 = 2 * pr * QBLK
            o = (jnp.dot(p[r0:r0 + QBLK], v_lo, preferred_element_type=F32) * inv[r0:r0 + QBLK]
                 + jnp.dot(p[r0 + QBLK:r0 + 2 * QBLK], v_hi, preferred_element_type=F32) * inv[r0 + QBLK:r0 + 2 * QBLK])
            c0 = h * GROUP * HEAD_DIM + pr * LANES
            o_ref[0, :, c0:c0 + LANES] = o.astype(BF16)

    s_all = [scores(0), scores(1)]
    p_cur = softmax(0, s_all[0])
    for h in range(N_KV_HEADS):
        if h + 2 < N_KV_HEADS:
            s_all.append(scores(h + 2))
        p_next = softmax(h + 1, s_all[h + 1]) if h + 1 < N_KV_HEADS else None
        weighted_values(h, *p_cur)
        p_cur = p_next


def _attention(q, kd, vd, kxd, vxd, sink):
    bsz, s, _ = q.shape
    lc = kxd.shape[1]
    nb = s // QBLK
    cur = lambda b, i: (b, i, 0)
    prev = lambda b, i: (b, jnp.maximum(i - 1, 0), 0)
    nxt = lambda b, i: (b, jnp.minimum(i + 1, nb - 1), 0)
    ctx = lambda b, i: (b, 0, 0)
    kvw = 2 * KV_DIM
    return pl.pallas_call(
        functools.partial(_attn_kernel, lc=lc),
        out_shape=jax.ShapeDtypeStruct((bsz, s, D), BF16),
        grid=(bsz, nb),
        in_specs=[
            pl.BlockSpec((1, QBLK, D), cur),
            pl.BlockSpec((1, QBLK, kvw), prev), pl.BlockSpec((1, QBLK, kvw), cur), pl.BlockSpec((1, QBLK, kvw), nxt),
            pl.BlockSpec((1, QBLK, kvw), prev), pl.BlockSpec((1, QBLK, kvw), cur), pl.BlockSpec((1, QBLK, kvw), nxt),
            pl.BlockSpec((1, lc, kvw), ctx), pl.BlockSpec((1, lc, kvw), ctx),
            pl.BlockSpec((1, N_HEADS), lambda b, i: (0, 0)),
        ],
        out_specs=pl.BlockSpec((1, QBLK, D), cur),
        compiler_params=_cparams(("arbitrary", "arbitrary")),
        name="window_attention",
    )(q, kd, kd, kd, vd, vd, vd, kxd, vxd, sink)


def _attn_out_kernel(a_ref, x_ref, mod_ref, n2g_ref, wo_ref, rwt_ref, rb_ref, x_out_ref, hrow_ref, cls_ref):
    mod = mod_ref[0]
    g1, sh2, sc2 = mod[2:3, :], mod[3:4, :], mod[4:5, :]
    gm2 = n2g_ref[...] * (1.0 + sc2)
    y = jnp.dot(a_ref[0], wo_ref[...], preferred_element_type=F32)
    x_out_ref[0] = _epilogue(x_ref[0], y, g1, sh2, gm2, rwt_ref, rb_ref, hrow_ref, cls_ref)


def _attn_out(attn, xs, mod, n2g, wo, rwt, rb, *, ts):
    bsz, s, _ = xs.shape
    nt = s // ts
    t_all = bsz * s
    const = lambda b, i: (0, 0)
    tok = lambda b, i: (b, i, 0)
    return pl.pallas_call(
        _attn_out_kernel,
        out_shape=(jax.ShapeDtypeStruct((bsz, s, D), F32),
                   jax.ShapeDtypeStruct((t_all, ROW_W), F32),
                   jax.ShapeDtypeStruct((1, t_all), jnp.int32)),
        grid=(bsz, nt),
        in_specs=[
            pl.BlockSpec((1, ts, D), tok), pl.BlockSpec((1, ts, D), tok),
            pl.BlockSpec((1, 6, D), lambda b, i: (b, 0, 0)),
            pl.BlockSpec((1, D), const), pl.BlockSpec((D, D), const),
            pl.BlockSpec((2, N_EXPERTS, D), lambda b, i: (0, 0, 0)), pl.BlockSpec((N_EXPERTS, 1), const),
        ],
        out_specs=(pl.BlockSpec((1, ts, D), tok),
                   pl.BlockSpec((ts, ROW_W), lambda b, i: (b * nt + i, 0)),
                   pl.BlockSpec((1, ts), lambda b, i: (0, b * nt + i))),
        compiler_params=_cparams(("arbitrary", "arbitrary")),
        name="attn_out",
    )(attn, xs, mod, n2g, wo, rwt, rb)


def _final_kernel(pos_ref, x_ref, ys_ref, mod_ref, fg_ref, o_ref, ybuf, sem, *, ts, seq):
    y = _combine_gather(pos_ref, ys_ref, ybuf, sem, ts=ts, tok_stride=seq, tok_off=0)
    x4 = x_ref[0] + mod_ref[0][5:6, :] * y
    r = lax.rsqrt(jnp.mean(x4 * x4, axis=-1, keepdims=True) + EPS)
    o_ref[0] = x4 * r * fg_ref[...]
    _combine_drain(ys_ref, ybuf, sem, ts)


def _final(pos, xs, ys, mod, fg, *, ts):
    bsz, s, _ = xs.shape
    tok = lambda b, i, p: (b, i, 0)
    return pl.pallas_call(
        functools.partial(_final_kernel, ts=ts, seq=s),
        out_shape=jax.ShapeDtypeStruct((bsz, s, D), F32),
        grid_spec=pltpu.PrefetchScalarGridSpec(
            num_scalar_prefetch=1,
            grid=(bsz, s // ts),
            in_specs=[pl.BlockSpec((1, ts, D), tok), pl.BlockSpec(memory_space=pl.ANY),
                      pl.BlockSpec((1, 6, D), lambda b, i, p: (b, 0, 0)), pl.BlockSpec((1, D), lambda b, i, p: (0, 0))],
            out_specs=pl.BlockSpec((1, ts, D), tok),
            scratch_shapes=[pltpu.VMEM((2, ts, D), F32), pltpu.SemaphoreType.DMA((2,))],
        ),
        compiler_params=_cparams(("arbitrary", "arbitrary"), disable_bounds_checks=True),
        name="combine_final",
    )(pos, xs, ys, mod, fg)


def _rope_tables(seq):
    rows = seq // GRID_W
    inv_freq = jnp.power(jnp.float32(ROPE_THETA), -jnp.arange(ROPE_PAIRS, dtype=F32) / ROPE_PAIRS)
    ang_r = jnp.arange(rows, dtype=jnp.int32).astype(F32)[:, None] * inv_freq
    ang_c = jnp.arange(GRID_W, dtype=jnp.int32).astype(F32)[:, None] * inv_freq
    by_row = lambda v: jnp.repeat(v, GRID_W, axis=0)
    by_col = lambda v: jnp.tile(v, (rows, 1))
    cos_r, sin_r = by_row(jnp.cos(ang_r)), by_row(jnp.sin(ang_r))
    cos_c, sin_c = by_col(jnp.cos(ang_c)), by_col(jnp.sin(ang_c))
    cos = jnp.concatenate([cos_r, cos_r, cos_c, cos_c], axis=1)
    sin = jnp.concatenate([-sin_r, sin_r, -sin_c, sin_c], axis=1)
    cos = jnp.concatenate([cos, cos], axis=1)
    sin = jnp.concatenate([sin, sin], axis=1)
    scale = HEAD_DIM ** -0.5 * LOG2E
    return cos * scale, sin * scale, cos, sin


def kernel(x, c, ctx, c_ctx, w_mod, b_mod, norm1_g, norm2_g, conv_w_pw1, conv_b_pw1, conv_w_dw, conv_b_dw,
           conv_ln_g, conv_ln_b, conv_w_pw2, conv_b_pw2, attn_w_qkv, attn_w_o, attn_sink, router_w, router_b,
           moe_w_gate, moe_w_up, moe_w_down, final_g):
    bsz, s, d = x.shape
    lc = ctx.shape[1]
    ts0 = 256
    ts = 512
    tm = 256
    assert d == D and w_mod.shape[0] == 2 and bsz + 1 <= COND_ROWS
    assert lc == ts0 and s % ts == 0 and s % QBLK == 0

    cond = jnp.concatenate([c, c_ctx[None, :], jnp.zeros((COND_ROWS - bsz - 1, D), F32)], axis=0)
    mods = _adaln(cond, w_mod, b_mod).reshape(2, COND_ROWS, 6, D)
    mod_lat = [mods[l, :bsz] for l in range(2)]
    mod_ctx = [mods[l, bsz:bsz + 1] for l in range(2)]

    rwt_hi = router_w.T.astype(BF16)
    rwt = jnp.stack([rwt_hi, (router_w.T - rwt_hi.astype(F32)).astype(BF16)])
    rb = router_b.reshape(N_EXPERTS, 1)
    row = lambda v: v.reshape(1, -1)
    wg = moe_w_gate.astype(BF16).reshape(2 * N_EXPERTS, D, D_EXPERT)
    wu = moe_w_up.astype(BF16).reshape(2 * N_EXPERTS, D, D_EXPERT)
    wd = moe_w_down.astype(BF16).reshape(2 * N_EXPERTS, D_EXPERT, D)

    x1, ctx1, hrow, cls = _conv_layer(
        x, ctx, mod_lat[0], mod_ctx[0], row(norm1_g[0]), row(norm2_g[0]), conv_w_pw1[0].astype(BF16),
        row(conv_b_pw1[0]), conv_w_dw[0], row(conv_b_dw[0]), row(conv_ln_g[0]), row(conv_ln_b[0]),
        conv_w_pw2[0].astype(BF16), row(conv_b_pw2[0]), rwt, rb, ts=ts0)
    rank, counts = _plan(cls)
    pos, src, ea, eb, nused = _dispatch_tables(cls, rank, counts, tm=tm)
    ys = _moe(hrow, src, ea, eb, nused, wg, wu, wd, tm=tm, layer=0)

    wq = attn_w_qkv[0][:, :D]
    wk = attn_w_qkv[0][:, D:D + KV_DIM].reshape(D, N_KV_HEADS, 1, HEAD_DIM)
    wv = attn_w_qkv[0][:, D + KV_DIM:].reshape(D, N_KV_HEADS, 1, HEAD_DIM)
    dup = lambda w: jnp.broadcast_to(w, (D, N_KV_HEADS, 2, HEAD_DIM)).reshape(D, 2 * KV_DIM)
    w_ext = jnp.concatenate([wq, dup(wk), dup(wv)], axis=1).astype(BF16)
    cq, sq, ck, sk = _rope_tables(s)
    ones = jnp.ones((lc, LANES), F32)
    zeros = jnp.zeros((lc, LANES), F32)
    n1g = row(norm1_g[1])
    x2, q, kd, vd = _qkv(pos, x1, ys, mod_lat[0], mod_lat[1], n1g, w_ext, cq, sq, ck, sk,
                         ts=ts, tok_stride=s + lc, tok_off=0)
    _, _, kxd, vxd = _qkv(pos, ctx1, ys, mod_ctx[0], mod_ctx[1], n1g, w_ext, ones, zeros, ones, zeros,
                          ts=lc, tok_stride=s + lc, tok_off=s)
    attn = _attention(q, kd, vd, kxd, vxd, attn_sink[0].reshape(1, N_HEADS))
    x3, hrow1, cls1 = _attn_out(attn, x2, mod_lat[1], row(norm2_g[1]), attn_w_o[0].astype(BF16), rwt, rb, ts=ts)
    rank1, counts1 = _plan(cls1)
    pos1, src1, ea1, eb1, nused1 = _dispatch_tables(cls1, rank1, counts1, tm=tm)
    ys1 = _moe(hrow1, src1, ea1, eb1, nused1, wg, wu, wd, tm=tm, layer=1)
    return _final(pos1, x3, ys1, mod_lat[1], row(final_g), ts=ts)
```

```python
import functools
import math

import jax
import jax.numpy as jnp
from jax import lax
from jax.experimental import pallas as pl
from jax.experimental.pallas import tpu as pltpu

D = 1024
GRID_W = 64
CONV_WIDTH = 31
CONV_PAD = CONV_WIDTH // 2
SUBLANES = 8
HALO = 16
HEAD_DIM = 64
N_HEADS = D // HEAD_DIM
N_KV_HEADS = N_HEADS // 4
GROUP = N_HEADS // N_KV_HEADS
KV_DIM = N_KV_HEADS * HEAD_DIM
WINDOW = 128
QBLK = 128
ROPE_THETA = 10000.0
ROPE_PAIRS = HEAD_DIM // 4
N_EXPERTS = 16
N_GROUPS = 4
EXPERTS_PER_GROUP = N_EXPERTS // N_GROUPS
N_PAIRS = 6
N_CLASSES = N_GROUPS * N_PAIRS
D_EXPERT = D // 2
EPS = 1e-6
NEG_INF = -1e30
LOG2E = 1.4426950408889634
LANES = 128
AUX = LANES
ROW_W = D + AUX
COND_ROWS = 16
VMEM_LIMIT = 56 * 1024 * 1024

F32 = jnp.float32
BF16 = jnp.bfloat16
HIGHEST = lax.Precision.HIGHEST


def _cparams(sem, **kw):
    return pltpu.CompilerParams(dimension_semantics=sem, vmem_limit_bytes=VMEM_LIMIT, **kw)


def _sigmoid(v):
    return jax.nn.sigmoid(v)


def _silu(v):
    return v * jax.nn.sigmoid(v)


def _rms_mod(xr, gm, sh):
    r = lax.rsqrt(jnp.mean(xr * xr, axis=-1, keepdims=True) + EPS)
    return xr * r * gm + sh


def _adaln_kernel(cond_ref, w_ref, b_ref, out_ref):
    cnd = cond_ref[...]
    act = _silu(cnd)
    out_ref[0] = jnp.dot(act, w_ref[0], precision=HIGHEST, preferred_element_type=F32) + b_ref[0]


def _adaln(cond, w_mod, b_mod):
    depth = w_mod.shape[0]
    tn = 1536
    return pl.pallas_call(
        _adaln_kernel,
        out_shape=jax.ShapeDtypeStruct((depth, COND_ROWS, 6 * D), F32),
        grid=(depth, 6 * D // tn),
        in_specs=[
            pl.BlockSpec((COND_ROWS, D), lambda l, n: (0, 0)),
            pl.BlockSpec((1, D, tn), lambda l, n: (l, 0, n)),
            pl.BlockSpec((1, 1, tn), lambda l, n: (l, 0, n)),
        ],
        out_specs=pl.BlockSpec((1, COND_ROWS, tn), lambda l, n: (l, 0, n)),
        compiler_params=_cparams(("arbitrary", "arbitrary")),
        name="adaln",
    )(cond, w_mod, b_mod.reshape(depth, 1, 6 * D))


def _route_rows(h2, rwt_ref, rb_ref):
    nt_dims = (((1,), (1,)), ((), ()))
    h_hi = h2.astype(BF16)
    h_lo = (h2 - h_hi.astype(F32)).astype(BF16)
    w_hi = rwt_ref[0]
    w_lo = rwt_ref[1]
    lt = (lax.dot_general(w_hi, h_hi, nt_dims, preferred_element_type=F32)
          + lax.dot_general(w_hi, h_lo, nt_dims, preferred_element_type=F32)
          + lax.dot_general(w_lo, h_hi, nt_dims, preferred_element_type=F32))
    s = _sigmoid(lt)
    sel = s + rb_ref[...]
    srow = [s[e:e + 1, :] for e in range(N_EXPERTS)]
    vrow = [sel[e:e + 1, :] for e in range(N_EXPERTS)]
    gscore = []
    for g in range(N_GROUPS):
        a, b, c, d = vrow[4 * g:4 * g + 4]
        gscore.append(jnp.maximum(jnp.maximum(jnp.maximum(a + b, a + c), jnp.maximum(a + d, b + c)),
                                  jnp.maximum(b + d, c + d)))
    bg = jnp.zeros_like(gscore[0], dtype=jnp.int32)
    best = gscore[0]
    for g in range(1, N_GROUPS):
        better = gscore[g] > best
        bg = jnp.where(better, g, bg)
        best = jnp.where(better, gscore[g], best)
    v = []
    sv = []
    for j in range(EXPERTS_PER_GROUP):
        vj = vrow[j]
        sj = srow[j]
        for g in range(1, N_GROUPS):
            vj = jnp.where(bg == g, vrow[4 * g + j], vj)
            sj = jnp.where(bg == g, srow[4 * g + j], sj)
        v.append(vj)
        sv.append(sj)
    i1 = jnp.zeros_like(bg)
    m1 = v[0]
    for j in range(1, EXPERTS_PER_GROUP):
        gt = v[j] > m1
        i1 = jnp.where(gt, j, i1)
        m1 = jnp.where(gt, v[j], m1)
    v2 = [jnp.where(i1 == j, -jnp.inf, v[j]) for j in range(EXPERTS_PER_GROUP)]
    i2 = jnp.zeros_like(bg)
    m2 = v2[0]
    for j in range(1, EXPERTS_PER_GROUP):
        gt = v2[j] > m2
        i2 = jnp.where(gt, j, i2)
        m2 = jnp.where(gt, v2[j], m2)
    lo = jnp.minimum(i1, i2)
    hi = jnp.maximum(i1, i2)
    s_lo = sv[0]
    s_hi = sv[0]
    for j in range(1, EXPERTS_PER_GROUP):
        s_lo = jnp.where(lo == j, sv[j], s_lo)
        s_hi = jnp.where(hi == j, sv[j], s_hi)
    den = s_lo + s_hi
    pair = jnp.where(lo == 0, 0, jnp.where(lo == 1, 3, 5)) + (hi - lo - 1)
    cls = N_PAIRS * bg + pair
    return s_lo / den, s_hi / den, cls


def _epilogue(x_in, y, g1, sh2, gm2, rwt_ref, rb_ref, hrow_ref, cls_ref):
    x1 = x_in + g1 * y
    h2 = _rms_mod(x1, gm2, sh2)
    ga, gb, cls = _route_rows(h2, rwt_ref, rb_ref)
    ts = h2.shape[0]
    cls_ref[...] = cls
    g8 = jnp.concatenate([ga, gb, jnp.zeros((6, ts), F32)], axis=0)
    g128 = jnp.concatenate([g8, jnp.zeros((AUX - 8, ts), F32)], axis=0)
    hrow_ref[:, :D] = h2
    hrow_ref[:, D:] = jnp.transpose(g128)
    return x1


def _conv_layer_kernel(x_ref, c_ref, xp_ref, xn_ref, modl_ref, modc_ref, n1g_ref, n2g_ref, w1_ref, b1_ref, wdw_ref,
                       bdw_ref, lng_ref, lnb_ref, w2_ref, b2_ref, rwt_ref, rb_ref,
                       x_out_ref, c_out_ref, hrow_ref, cls_ref, u_scr, cv_scr, v_scr, *, ts):
    i = pl.program_id(1)
    nt = pl.num_programs(1) - 1
    is_ctx = jnp.full((1, 1), i, jnp.int32) == nt
    mod = jnp.where(is_ctx, modc_ref[0], modl_ref[0])
    sh1, sc1, g1, sh2, sc2 = (mod[k:k + 1, :] for k in range(5))
    gm1 = n1g_ref[...] * (1.0 + sc1)
    gm2 = n2g_ref[...] * (1.0 + sc2)
    cw = 512

    def glu(hb, c0):
        a = jnp.dot(hb, w1_ref[:, c0:c0 + cw], preferred_element_type=F32) + b1_ref[:, c0:c0 + cw]
        g = jnp.dot(hb, w1_ref[:, D + c0:D + c0 + cw], preferred_element_type=F32) + b1_ref[:, D + c0:D + c0 + cw]
        return a * _sigmoid(g)

    xm = jnp.where(is_ctx, c_ref[0], x_ref[0])
    hb = jnp.concatenate([_rms_mod(xp_ref[0, 0], gm1, sh1).astype(BF16), _rms_mod(xm, gm1, sh1).astype(BF16),
                          _rms_mod(xn_ref[0, 0], gm1, sh1).astype(BF16)], axis=0)
    keep_prev = jnp.logical_and(i > 0, i < nt).astype(F32)
    keep_next = (i < nt - 1).astype(F32)
    for c0 in range(0, D, cw):
        u = glu(hb, c0)
        u_scr[0:HALO, c0:c0 + cw] = u[:HALO] * keep_prev
        u_scr[HALO:HALO + ts, c0:c0 + cw] = u[HALO:HALO + ts]
        u_scr[HALO + ts:2 * HALO + ts, c0:c0 + cw] = u[HALO + ts:] * keep_next

    rr = 64
    taps_off = HALO - CONV_PAD

    def conv_chunk(rc, carry):
        r0 = pl.multiple_of(rc * rr, rr)
        for c0 in range(0, D, LANES):
            win = u_scr[pl.ds(r0, rr + 2 * HALO), c0:c0 + LANES]
            out = None
            for r in range(SUBLANES):
                nrow = rr + SUBLANES if r else rr
                part = None
                for q in range((CONV_WIDTH + taps_off) // SUBLANES + 1):
                    k = SUBLANES * q + r - taps_off
                    if 0 <= k < CONV_WIDTH:
                        term = win[SUBLANES * q:SUBLANES * q + nrow, :] * wdw_ref[k:k + 1, c0:c0 + LANES]
                        part = term if part is None else part + term
                shifted = part[r:r + rr, :]
                out = shifted if out is None else out + shifted
            cv_scr[pl.ds(r0, rr), c0:c0 + LANES] = out
        cv = cv_scr[pl.ds(r0, rr), :] + bdw_ref[...]
        mu = jnp.mean(cv, axis=-1, keepdims=True)
        dv = cv - mu
        var = jnp.mean(dv * dv, axis=-1, keepdims=True)
        ln = dv * lax.rsqrt(var + EPS) * lng_ref[...] + lnb_ref[...]
        v_scr[pl.ds(r0, rr), :] = _silu(ln).astype(BF16)
        return carry

    lax.fori_loop(0, ts // rr, conv_chunk, 0)
    y = jnp.dot(v_scr[...], w2_ref[...], preferred_element_type=F32) + b2_ref[...]
    x1 = _epilogue(xm, y, g1, sh2, gm2, rwt_ref, rb_ref, hrow_ref, cls_ref)

    @pl.when(i < nt)
    def _():
        x_out_ref[0] = x1

    @pl.when(i == nt)
    def _():
        c_out_ref[0] = x1


def _conv_layer(xs, cs, mod_lat, mod_ctx, n1g, n2g, w1, b1, wdw, bdw, lng, lnb, w2, b2, rwt, rb, *, ts):
    bsz, s, _ = xs.shape
    assert cs.shape[1] == ts
    nt = s // ts
    nh = s // HALO
    hpt = ts // HALO
    x4 = xs.reshape(bsz, nh, HALO, D)
    t_all = bsz * (s + ts)
    const = lambda b, i: (0, 0)
    lat = lambda b, i: (b, jnp.minimum(i, nt - 1), 0)
    row_blk = lambda b, i: (b * (nt + 1) + i, 0)
    return pl.pallas_call(
        functools.partial(_conv_layer_kernel, ts=ts),
        out_shape=(jax.ShapeDtypeStruct((bsz, s, D), F32),
                   jax.ShapeDtypeStruct((bsz, ts, D), F32),
                   jax.ShapeDtypeStruct((t_all, ROW_W), F32),
                   jax.ShapeDtypeStruct((1, t_all), jnp.int32)),
        grid=(bsz, nt + 1),
        in_specs=[
            pl.BlockSpec((1, ts, D), lat),
            pl.BlockSpec((1, ts, D), lambda b, i: (b, 0, 0)),
            pl.BlockSpec((1, 1, HALO, D), lambda b, i: (b, jnp.clip(i * hpt - 1, 0, nh - 1), 0, 0)),
            pl.BlockSpec((1, 1, HALO, D), lambda b, i: (b, jnp.minimum((i + 1) * hpt, nh - 1), 0, 0)),
            pl.BlockSpec((1, 6, D), lambda b, i: (b, 0, 0)),
            pl.BlockSpec((1, 6, D), lambda b, i: (0, 0, 0)),
            pl.BlockSpec((1, D), const), pl.BlockSpec((1, D), const),
            pl.BlockSpec((D, 2 * D), const), pl.BlockSpec((1, 2 * D), const),
            pl.BlockSpec((CONV_WIDTH, D), const), pl.BlockSpec((1, D), const),
            pl.BlockSpec((1, D), const), pl.BlockSpec((1, D), const),
            pl.BlockSpec((D, D), const), pl.BlockSpec((1, D), const),
            pl.BlockSpec((2, N_EXPERTS, D), lambda b, i: (0, 0, 0)), pl.BlockSpec((N_EXPERTS, 1), const),
        ],
        out_specs=(pl.BlockSpec((1, ts, D), lat),
                   pl.BlockSpec((1, ts, D), lambda b, i: (b, 0, 0)),
                   pl.BlockSpec((ts, ROW_W), row_blk),
                   pl.BlockSpec((1, ts), lambda b, i: (0, b * (nt + 1) + i))),
        scratch_shapes=[pltpu.VMEM((ts + 2 * HALO, D), F32), pltpu.VMEM((ts, D), F32), pltpu.VMEM((ts, D), BF16)],
        compiler_params=_cparams(("arbitrary", "arbitrary")),
        name="conv_layer",
    )(xs, cs, x4, x4, mod_lat, mod_ctx, n1g, n2g, w1, b1, wdw, bdw, lng, lnb, w2, b2, rwt, rb)


def _plan_kernel(cls_ref, rank_ref, cnt_ref, carry_scr, *, tb):
    @pl.when(pl.program_id(0) == 0)
    def _():
        carry_scr[...] = jnp.zeros_like(carry_scr)

    cls = cls_ref[...]
    onehot = (lax.broadcasted_iota(jnp.int32, (32, tb), 0) == cls).astype(F32)
    upper = (lax.broadcasted_iota(jnp.int32, (tb, tb), 0) <= lax.broadcasted_iota(jnp.int32, (tb, tb), 1))
    prefix = jnp.dot(onehot.astype(BF16), upper.astype(BF16), preferred_element_type=F32)
    carry = carry_scr[:, 0:1]
    rank = jnp.sum(onehot * (prefix + carry), axis=0, keepdims=True) - 1.0
    rank_ref[...] = rank.astype(jnp.int32)
    total = carry + jnp.sum(onehot, axis=1, keepdims=True)
    carry_scr[...] = jnp.broadcast_to(total, carry_scr.shape)
    cnt_ref[...] = jnp.broadcast_to(total, cnt_ref.shape)


def _plan(cls):
    t_all = cls.shape[1]
    tb = math.gcd(t_all, 1024)
    return pl.pallas_call(
        functools.partial(_plan_kernel, tb=tb),
        out_shape=(jax.ShapeDtypeStruct((1, t_all), jnp.int32), jax.ShapeDtypeStruct((32, LANES), F32)),
        grid=(t_all // tb,),
        in_specs=[pl.BlockSpec((1, tb), lambda i: (0, i))],
        out_specs=(pl.BlockSpec((1, tb), lambda i: (0, i)), pl.BlockSpec((32, LANES), lambda i: (0, 0))),
        scratch_shapes=[pltpu.VMEM((32, LANES), F32)],
        compiler_params=_cparams(("arbitrary",)),
        name="route_plan",
    )(cls)


def _invert_kernel(pos_ref, gap_ref, src_ref, *, n_tok, n_gap):
    def zero(i, carry):
        src_ref[i] = 0
        return carry
    for g in range(n_gap):
        lax.fori_loop(gap_ref[2 * g], gap_ref[2 * g + 1], zero, 0)

    unroll = 16

    def put(i, carry):
        for u in range(unroll):
            t = i * unroll + u
            src_ref[pos_ref[t]] = t
        return carry
    lax.fori_loop(0, n_tok // unroll, put, 0)


def _invert(pos, gaps, n_slot):
    n_tok = pos.shape[0]
    return pl.pallas_call(
        functools.partial(_invert_kernel, n_tok=n_tok, n_gap=gaps.shape[0] // 2),
        out_shape=jax.ShapeDtypeStruct((n_slot,), jnp.int32),
        in_specs=[pl.BlockSpec(memory_space=pltpu.SMEM), pl.BlockSpec(memory_space=pltpu.SMEM)],
        out_specs=pl.BlockSpec(memory_space=pltpu.SMEM),
        name="invert_slots",
    )(pos, gaps)


def _dispatch_tables(cls, rank, counts, *, tm):
    t_all = cls.shape[1]
    n_tiles = t_all // tm + N_CLASSES
    cnt = counts[:N_CLASSES, 0].astype(jnp.int32)
    padded = ((cnt + tm - 1) // tm) * tm
    ends = jnp.cumsum(padded)
    offs = ends - padded
    pos = offs[cls[0]] + rank[0]
    gap_lo = jnp.concatenate([offs + cnt, ends[-1:]])
    gap_hi = jnp.concatenate([ends, jnp.full((1,), n_tiles * tm, jnp.int32)])
    src = _invert(pos, jnp.stack([gap_lo, gap_hi], axis=1).reshape(-1).astype(jnp.int32), n_tiles * tm)
    n_used = ends[-1] // tm
    tile_start = jnp.minimum(jnp.arange(n_tiles, dtype=jnp.int32), n_used - 1) * tm
    tile_cls = jnp.minimum(jnp.sum(ends[None, :] <= tile_start[:, None], axis=1), N_CLASSES - 1).astype(jnp.int32)
    pair_lo = jnp.array([0, 0, 0, 1, 1, 2], jnp.int32)
    pair_hi = jnp.array([1, 2, 3, 2, 3, 3], jnp.int32)
    grp = tile_cls // N_PAIRS
    ea = EXPERTS_PER_GROUP * grp + pair_lo[tile_cls % N_PAIRS]
    eb = EXPERTS_PER_GROUP * grp + pair_hi[tile_cls % N_PAIRS]
    return pos, src, ea, eb, n_used.reshape(1).astype(jnp.int32)


def _start_row_gather(idx_ref, idx_base, src_hbm, dst, sem, n, unroll):
    def one(r):
        row = idx_ref[idx_base + r]
        pltpu.make_async_copy(src_hbm.at[pl.ds(row, 1), :], dst.at[pl.ds(r, 1), :], sem).start()

    if unroll:
        for r in range(n):
            one(r)
    else:
        def body(r, carry):
            one(r)
            return carry
        lax.fori_loop(0, n, body, 0)


def _wait_row_gather(src_hbm, dst, sem, n):
    pltpu.make_async_copy(src_hbm.at[pl.ds(0, n), :], dst, sem).wait()


def _moe_kernel(src_ref, ea_ref, eb_ref, nused_ref, hrow_ref, wga_ref, wua_ref, wda_ref, wgb_ref, wub_ref, wdb_ref,
                ys_ref, gbuf, sem, *, tm):
    j = pl.program_id(0)
    nused = nused_ref[0]
    slot = j % 2

    @pl.when(j == 0)
    def _():
        _start_row_gather(src_ref, 0, hrow_ref, gbuf.at[0], sem.at[0], tm, unroll=False)

    @pl.when(j < nused)
    def _():
        nxt = jnp.minimum(j + 1, nused - 1)
        _start_row_gather(src_ref, nxt * tm, hrow_ref, gbuf.at[1 - slot], sem.at[1 - slot], tm, unroll=True)
        _wait_row_gather(hrow_ref, gbuf.at[slot], sem.at[slot], tm)
        h = gbuf[slot, :, :D].astype(BF16)
        ga = gbuf[slot, :, D:D + 1]
        gb = gbuf[slot, :, D + 1:D + 2]

        def expert(wg, wu, wd):
            a = jnp.dot(h, wg[0], preferred_element_type=F32)
            u = jnp.dot(h, wu[0], preferred_element_type=F32)
            act = (_silu(a) * u).astype(BF16)
            return jnp.dot(act, wd[0], preferred_element_type=F32)

        ys_ref[...] = ga * expert(wga_ref, wua_ref, wda_ref) + gb * expert(wgb_ref, wub_ref, wdb_ref)

    @pl.when(j == nused - 1)
    def _():
        _wait_row_gather(hrow_ref, gbuf.at[1 - slot], sem.at[1 - slot], tm)

    @pl.when(j >= nused)
    def _():
        ys_ref[...] = jnp.zeros_like(ys_ref)


def _moe(hrow, src, ea, eb, nused, wg, wu, wd, *, tm, layer):
    n_tiles = ea.shape[0]
    f = wg.shape[-1]
    e0 = layer * N_EXPERTS
    wa = lambda j, src, ea, eb, nu: (e0 + ea[j], 0, 0)
    wb = lambda j, src, ea, eb, nu: (e0 + eb[j], 0, 0)
    return pl.pallas_call(
        functools.partial(_moe_kernel, tm=tm),
        out_shape=jax.ShapeDtypeStruct((n_tiles * tm, D), F32),
        grid_spec=pltpu.PrefetchScalarGridSpec(
            num_scalar_prefetch=4,
            grid=(n_tiles,),
            in_specs=[
                pl.BlockSpec(memory_space=pl.ANY),
                pl.BlockSpec((1, D, f), wa), pl.BlockSpec((1, D, f), wa), pl.BlockSpec((1, f, D), wa),
                pl.BlockSpec((1, D, f), wb), pl.BlockSpec((1, D, f), wb), pl.BlockSpec((1, f, D), wb),
            ],
            out_specs=pl.BlockSpec((tm, D), lambda j, src, ea, eb, nu: (j, 0)),
            scratch_shapes=[pltpu.VMEM((2, tm, ROW_W), F32), pltpu.SemaphoreType.DMA((2,))],
        ),
        compiler_params=_cparams(("arbitrary",), disable_bounds_checks=True),
        name="moe_experts",
    )(src, ea, eb, nused, hrow, wg, wu, wd, wg, wu, wd)


def _combine_base(step, *, ts, tok_stride, tok_off):
    nt = pl.num_programs(1)
    return (step // nt) * tok_stride + tok_off + (step % nt) * ts


def _combine_gather(pos_ref, ys_ref, ybuf, sem, *, ts, tok_stride, tok_off):
    nt = pl.num_programs(1)
    n = pl.program_id(0) * nt + pl.program_id(1)

    @pl.when(n == 0)
    def _():
        _start_row_gather(pos_ref, _combine_base(0, ts=ts, tok_stride=tok_stride, tok_off=tok_off), ys_ref,
                          ybuf.at[0], sem.at[0], ts, unroll=False)

    nxt = jnp.minimum(n + 1, pl.num_programs(0) * nt - 1)
    _start_row_gather(pos_ref, _combine_base(nxt, ts=ts, tok_stride=tok_stride, tok_off=tok_off), ys_ref,
                      ybuf.at[1 - n % 2], sem.at[1 - n % 2], ts, unroll=True)
    _wait_row_gather(ys_ref, ybuf.at[n % 2], sem.at[n % 2], ts)
    return ybuf[n % 2]


def _combine_drain(ys_ref, ybuf, sem, ts):
    nt = pl.num_programs(1)
    n = pl.program_id(0) * nt + pl.program_id(1)

    @pl.when(n == pl.num_programs(0) * nt - 1)
    def _():
        _wait_row_gather(ys_ref, ybuf.at[1 - n % 2], sem.at[1 - n % 2], ts)


def _rope(v, cos, sin):
    lane = lax.broadcasted_iota(jnp.int32, v.shape, 1)
    first = (lane % (2 * ROPE_PAIRS)) < ROPE_PAIRS
    partner = jnp.where(first, pltpu.roll(v, LANES - ROPE_PAIRS, 1), pltpu.roll(v, ROPE_PAIRS, 1))
    return v * cos + partner * sin


def _qkv_kernel(pos_ref, x_ref, ys_ref, modp_ref, mod_ref, n1g_ref, w_ref, cq_ref, sq_ref, ck_ref, sk_ref,
                x_out_ref, q_ref, kd_ref, vd_ref, ybuf, sem, *, ts, seq, tok_stride, tok_off):
    y = _combine_gather(pos_ref, ys_ref, ybuf, sem, ts=ts, tok_stride=tok_stride, tok_off=tok_off)
    g2_prev = modp_ref[0][5:6, :]
    x2 = x_ref[0] + g2_prev * y
    x_out_ref[0] = x2
    mod = mod_ref[0]
    sh1 = mod[0:1, :]
    gm1 = n1g_ref[...] * (1.0 + mod[1:2, :])
    hb = _rms_mod(x2, gm1, sh1).astype(BF16)
    cq, sq, ck, sk = cq_ref[...], sq_ref[...], ck_ref[...], sk_ref[...]
    nw = 2 * LANES
    for c0 in range(0, D, nw):
        qc = jnp.dot(hb, w_ref[:, c0:c0 + nw], preferred_element_type=F32)
        for l0 in range(0, nw, LANES):
            q_ref[0, :, c0 + l0:c0 + l0 + LANES] = _rope(qc[:, l0:l0 + LANES], cq, sq).astype(BF16)
    for c0 in range(0, 2 * KV_DIM, nw):
        kc = jnp.dot(hb, w_ref[:, D + c0:D + c0 + nw], preferred_element_type=F32)
        for l0 in range(0, nw, LANES):
            kd_ref[0, :, c0 + l0:c0 + l0 + LANES] = _rope(kc[:, l0:l0 + LANES], ck, sk).astype(BF16)
    vd_ref[0] = jnp.dot(hb, w_ref[:, D + 2 * KV_DIM:], preferred_element_type=F32).astype(BF16)
    _combine_drain(ys_ref, ybuf, sem, ts)


def _qkv(pos, xs, ys, modp, mod, n1g, w_ext, cq, sq, ck, sk, *, ts, tok_stride, tok_off):
    bsz, s, _ = xs.shape
    nt = s // ts
    per_batch = mod.shape[0] > 1
    mod_map = (lambda b, i, p: (b, 0, 0)) if per_batch else (lambda b, i, p: (0, 0, 0))
    const = lambda b, i, p: (0, 0)
    tok = lambda b, i, p: (b, i, 0)
    tab = lambda b, i, p: (i, 0)
    return pl.pallas_call(
        functools.partial(_qkv_kernel, ts=ts, seq=s, tok_stride=tok_stride, tok_off=tok_off),
        out_shape=(jax.ShapeDtypeStruct((bsz, s, D), F32), jax.ShapeDtypeStruct((bsz, s, D), BF16),
                   jax.ShapeDtypeStruct((bsz, s, 2 * KV_DIM), BF16), jax.ShapeDtypeStruct((bsz, s, 2 * KV_DIM), BF16)),
        grid_spec=pltpu.PrefetchScalarGridSpec(
            num_scalar_prefetch=1,
            grid=(bsz, nt),
            in_specs=[
                pl.BlockSpec((1, ts, D), tok),
                pl.BlockSpec(memory_space=pl.ANY),
                pl.BlockSpec((1, 6, D), mod_map), pl.BlockSpec((1, 6, D), mod_map),
                pl.BlockSpec((1, D), const),
                pl.BlockSpec((D, D + 4 * KV_DIM), const),
                pl.BlockSpec((ts, LANES), tab), pl.BlockSpec((ts, LANES), tab),
                pl.BlockSpec((ts, LANES), tab), pl.BlockSpec((ts, LANES), tab),
            ],
            out_specs=(pl.BlockSpec((1, ts, D), tok), pl.BlockSpec((1, ts, D), tok),
                       pl.BlockSpec((1, ts, 2 * KV_DIM), tok), pl.BlockSpec((1, ts, 2 * KV_DIM), tok)),
            scratch_shapes=[pltpu.VMEM((2, ts, D), F32), pltpu.SemaphoreType.DMA((2,))],
        ),
        compiler_params=_cparams(("arbitrary", "arbitrary"), disable_bounds_checks=True),
        name="combine_qkv",
    )(pos, xs, ys, modp, mod, n1g, w_ext, cq, sq, ck, sk)


def _attn_kernel(q_ref, kp_ref, kc_ref, kn_ref, vp_ref, vc_ref, vn_ref, kx_ref, vx_ref, sink_ref, o_ref, *, lc):
    i = pl.program_id(1)
    nb = pl.num_programs(1)
    nband = 3 * QBLK
    qrow = lax.broadcasted_iota(jnp.int32, (GROUP * QBLK, nband), 0) % QBLK
    kcol = lax.broadcasted_iota(jnp.int32, (GROUP * QBLK, nband), 1)
    lo_lim = jnp.where(i > 0, 0, QBLK)
    hi_lim = jnp.where(i < nb - 1, nband, 2 * QBLK - 1)
    ok_band = (kcol >= jnp.maximum(qrow, lo_lim)) & (kcol <= jnp.minimum(qrow + 2 * WINDOW, hi_lim))
    lane = lax.broadcasted_iota(jnp.int32, (1, LANES), 1)
    low = lane < HEAD_DIM
    def scores(h):
        hs = slice(h * LANES, (h + 1) * LANES)
        k_all = jnp.concatenate([kp_ref[0, :, hs], kc_ref[0, :, hs], kn_ref[0, :, hs], kx_ref[0, :, hs]], axis=0)
        qs = []
        for j in range(GROUP):
            c0 = h * GROUP * HEAD_DIM + (j // 2) * LANES
            qh = q_ref[0, :, c0:c0 + LANES]
            keep = low if j % 2 == 0 else jnp.logical_not(low)
            qs.append(jnp.where(keep, qh, jnp.zeros_like(qh)))
        qm = jnp.concatenate(qs, axis=0)
        return lax.dot_general(qm, k_all, (((1,), (1,)), ((), ())), preferred_element_type=F32)

    def softmax(h, s):
        s = jnp.concatenate([jnp.where(ok_band, s[:, :nband], NEG_INF), s[:, nband:]], axis=1)
        sink = LOG2E * jnp.concatenate(
            [jnp.broadcast_to(sink_ref[:, h * GROUP + j:h * GROUP + j + 1], (QBLK, 1)) for j in range(GROUP)], axis=0)
        nblk = s.shape[1] // LANES
        smax = s[:, :LANES]
        for kb in range(1, nblk):
            smax = jnp.maximum(smax, s[:, kb * LANES:(kb + 1) * LANES])
        m = jnp.maximum(jnp.max(smax, axis=1, keepdims=True), sink)
        return jnp.exp2(s - m).astype(BF16), jnp.exp2(sink - m)

    def weighted_values(h, p, p_sink):
        hs = slice(h * LANES, (h + 1) * LANES)
        v_all = jnp.concatenate([vp_ref[0, :, hs], vc_ref[0, :, hs], vn_ref[0, :, hs], vx_ref[0, :, hs]], axis=0)
        ones_lo = jnp.broadcast_to(jnp.where(lane == HEAD_DIM, 1.0, 0.0).astype(v_all.dtype), v_all.shape)
        ones_hi = jnp.broadcast_to(jnp.where(lane == 0, 1.0, 0.0).astype(v_all.dtype), v_all.shape)
        v_lo = jnp.where(low, v_all, ones_lo)
        v_hi = jnp.where(low, ones_hi, v_all)
        for pr in range(GROUP // 2):
            r0 = 2 * pr * QBLK
            o_lo = jnp.dot(p[r0:r0 + QBLK], v_lo, preferred_element_type=F32)
            o_hi = jnp.dot(p[r0 + QBLK:r0 + 2 * QBLK], v_hi, preferred_element_type=F32)
            inv_lo = 1.0 / (o_lo[:, HEAD_DIM:HEAD_DIM + 1] + p_sink[r0:r0 + QBLK])
            inv_hi = 1.0 / (o_hi[:, 0:1] + p_sink[r0 + QBLK:r0 + 2 * QBLK])
            c0 = h * GROUP * HEAD_DIM + pr * LANES
            o_ref[0, :, c0:c0 + LANES] = jnp.where(low, o_lo * inv_lo, o_hi * inv_hi).astype(BF16)

    s_all = [scores(0), scores(1)]
    p_cur = softmax(0, s_all[0])
    for h in range(N_KV_HEADS):
        if h + 2 < N_KV_HEADS:
            s_all.append(scores(h + 2))
        p_next = softmax(h + 1, s_all[h + 1]) if h + 1 < N_KV_HEADS else None
        weighted_values(h, *p_cur)
        p_cur = p_next


def _attention(q, kd, vd, kxd, vxd, sink):
    bsz, s, _ = q.shape
    lc = kxd.shape[1]
    nb = s // QBLK
    cur = lambda b, i: (b, i, 0)
    prev = lambda b, i: (b, jnp.maximum(i - 1, 0), 0)
    nxt = lambda b, i: (b, jnp.minimum(i + 1, nb - 1), 0)
    ctx = lambda b, i: (b, 0, 0)
    kvw = 2 * KV_DIM
    return pl.pallas_call(
        functools.partial(_attn_kernel, lc=lc),
        out_shape=jax.ShapeDtypeStruct((bsz, s, D), BF16),
        grid=(bsz, nb),
        in_specs=[
            pl.BlockSpec((1, QBLK, D), cur),
            pl.BlockSpec((1, QBLK, kvw), prev), pl.BlockSpec((1, QBLK, kvw), cur), pl.BlockSpec((1, QBLK, kvw), nxt),
            pl.BlockSpec((1, QBLK, kvw), prev), pl.BlockSpec((1, QBLK, kvw), cur), pl.BlockSpec((1, QBLK, kvw), nxt),
            pl.BlockSpec((1, lc, kvw), ctx), pl.BlockSpec((1, lc, kvw), ctx),
            pl.BlockSpec((1, N_HEADS), lambda b, i: (0, 0)),
        ],
        out_specs=pl.BlockSpec((1, QBLK, D), cur),
        compiler_params=_cparams(("arbitrary", "arbitrary")),
        name="window_attention",
    )(q, kd, kd, kd, vd, vd, vd, kxd, vxd, sink)


def _attn_out_kernel(a_ref, x_ref, mod_ref, n2g_ref, wo_ref, rwt_ref, rb_ref, x_out_ref, hrow_ref, cls_ref):
    mod = mod_ref[0]
    g1, sh2, sc2 = mod[2:3, :], mod[3:4, :], mod[4:5, :]
    gm2 = n2g_ref[...] * (1.0 + sc2)
    y = jnp.dot(a_ref[0], wo_ref[...], preferred_element_type=F32)
    x_out_ref[0] = _epilogue(x_ref[0], y, g1, sh2, gm2, rwt_ref, rb_ref, hrow_ref, cls_ref)


def _attn_out(attn, xs, mod, n2g, wo, rwt, rb, *, ts):
    bsz, s, _ = xs.shape
    nt = s // ts
    t_all = bsz * s
    const = lambda b, i: (0, 0)
    tok = lambda b, i: (b, i, 0)
    return pl.pallas_call(
        _attn_out_kernel,
        out_shape=(jax.ShapeDtypeStruct((bsz, s, D), F32),
                   jax.ShapeDtypeStruct((t_all, ROW_W), F32),
                   jax.ShapeDtypeStruct((1, t_all), jnp.int32)),
        grid=(bsz, nt),
        in_specs=[
            pl.BlockSpec((1, ts, D), tok), pl.BlockSpec((1, ts, D), tok),
            pl.BlockSpec((1, 6, D), lambda b, i: (b, 0, 0)),
            pl.BlockSpec((1, D), const), pl.BlockSpec((D, D), const),
            pl.BlockSpec((2, N_EXPERTS, D), lambda b, i: (0, 0, 0)), pl.BlockSpec((N_EXPERTS, 1), const),
        ],
        out_specs=(pl.BlockSpec((1, ts, D), tok),
                   pl.BlockSpec((ts, ROW_W), lambda b, i: (b * nt + i, 0)),
                   pl.BlockSpec((1, ts), lambda b, i: (0, b * nt + i))),
        compiler_params=_cparams(("arbitrary", "arbitrary")),
        name="attn_out",
    )(attn, xs, mod, n2g, wo, rwt, rb)


def _final_kernel(pos_ref, x_ref, ys_ref, mod_ref, fg_ref, o_ref, ybuf, sem, *, ts, seq):
    y = _combine_gather(pos_ref, ys_ref, ybuf, sem, ts=ts, tok_stride=seq, tok_off=0)
    x4 = x_ref[0] + mod_ref[0][5:6, :] * y
    r = lax.rsqrt(jnp.mean(x4 * x4, axis=-1, keepdims=True) + EPS)
    o_ref[0] = x4 * r * fg_ref[...]
    _combine_drain(ys_ref, ybuf, sem, ts)


def _final(pos, xs, ys, mod, fg, *, ts):
    bsz, s, _ = xs.shape
    tok = lambda b, i, p: (b, i, 0)
    return pl.pallas_call(
        functools.partial(_final_kernel, ts=ts, seq=s),
        out_shape=jax.ShapeDtypeStruct((bsz, s, D), F32),
        grid_spec=pltpu.PrefetchScalarGridSpec(
            num_scalar_prefetch=1,
            grid=(bsz, s // ts),
            in_specs=[pl.BlockSpec((1, ts, D), tok), pl.BlockSpec(memory_space=pl.ANY),
                      pl.BlockSpec((1, 6, D), lambda b, i, p: (b, 0, 0)), pl.BlockSpec((1, D), lambda b, i, p: (0, 0))],
            out_specs=pl.BlockSpec((1, ts, D), tok),
            scratch_shapes=[pltpu.VMEM((2, ts, D), F32), pltpu.SemaphoreType.DMA((2,))],
        ),
        compiler_params=_cparams(("arbitrary", "arbitrary"), disable_bounds_checks=True),
        name="combine_final",
    )(pos, xs, ys, mod, fg)


def _rope_tables(seq):
    rows = seq // GRID_W
    inv_freq = jnp.power(jnp.float32(ROPE_THETA), -jnp.arange(ROPE_PAIRS, dtype=F32) / ROPE_PAIRS)
    ang_r = jnp.arange(rows, dtype=jnp.int32).astype(F32)[:, None] * inv_freq
    ang_c = jnp.arange(GRID_W, dtype=jnp.int32).astype(F32)[:, None] * inv_freq
    by_row = lambda v: jnp.repeat(v, GRID_W, axis=0)
    by_col = lambda v: jnp.tile(v, (rows, 1))
    cos_r, sin_r = by_row(jnp.cos(ang_r)), by_row(jnp.sin(ang_r))
    cos_c, sin_c = by_col(jnp.cos(ang_c)), by_col(jnp.sin(ang_c))
    cos = jnp.concatenate([cos_r, cos_r, cos_c, cos_c], axis=1)
    sin = jnp.concatenate([-sin_r, sin_r, -sin_c, sin_c], axis=1)
    cos = jnp.concatenate([cos, cos], axis=1)
    sin = jnp.concatenate([sin, sin], axis=1)
    scale = HEAD_DIM ** -0.5 * LOG2E
    return cos * scale, sin * scale, cos, sin


def kernel(x, c, ctx, c_ctx, w_mod, b_mod, norm1_g, norm2_g, conv_w_pw1, conv_b_pw1, conv_w_dw, conv_b_dw,
           conv_ln_g, conv_ln_b, conv_w_pw2, conv_b_pw2, attn_w_qkv, attn_w_o, attn_sink, router_w, router_b,
           moe_w_gate, moe_w_up, moe_w_down, final_g):
    bsz, s, d = x.shape
    lc = ctx.shape[1]
    ts0 = 256
    ts = 512
    tm = 256
    assert d == D and w_mod.shape[0] == 2 and bsz + 1 <= COND_ROWS
    assert lc == ts0 and s % ts == 0 and s % QBLK == 0

    cond = jnp.concatenate([c, c_ctx[None, :], jnp.zeros((COND_ROWS - bsz - 1, D), F32)], axis=0)
    mods = _adaln(cond, w_mod, b_mod).reshape(2, COND_ROWS, 6, D)
    mod_lat = [mods[l, :bsz] for l in range(2)]
    mod_ctx = [mods[l, bsz:bsz + 1] for l in range(2)]

    rwt_hi = router_w.T.astype(BF16)
    rwt = jnp.stack([rwt_hi, (router_w.T - rwt_hi.astype(F32)).astype(BF16)])
    rb = router_b.reshape(N_EXPERTS, 1)
    row = lambda v: v.reshape(1, -1)
    wg = moe_w_gate.astype(BF16).reshape(2 * N_EXPERTS, D, D_EXPERT)
    wu = moe_w_up.astype(BF16).reshape(2 * N_EXPERTS, D, D_EXPERT)
    wd = moe_w_down.astype(BF16).reshape(2 * N_EXPERTS, D_EXPERT, D)

    x1, ctx1, hrow, cls = _conv_layer(
        x, ctx, mod_lat[0], mod_ctx[0], row(norm1_g[0]), row(norm2_g[0]), conv_w_pw1[0].astype(BF16),
        row(conv_b_pw1[0]), conv_w_dw[0], row(conv_b_dw[0]), row(conv_ln_g[0]), row(conv_ln_b[0]),
        conv_w_pw2[0].astype(BF16), row(conv_b_pw2[0]), rwt, rb, ts=ts0)
    rank, counts = _plan(cls)
    pos, src, ea, eb, nused = _dispatch_tables(cls, rank, counts, tm=tm)
    ys = _moe(hrow, src, ea, eb, nused, wg, wu, wd, tm=tm, layer=0)

    wq = attn_w_qkv[0][:, :D]
    wk = attn_w_qkv[0][:, D:D + KV_DIM].reshape(D, N_KV_HEADS, 1, HEAD_DIM)
    wv = attn_w_qkv[0][:, D + KV_DIM:].reshape(D, N_KV_HEADS, 1, HEAD_DIM)
    dup = lambda w: jnp.broadcast_to(w, (D, N_KV_HEADS, 2, HEAD_DIM)).reshape(D, 2 * KV_DIM)
    w_ext = jnp.concatenate([wq, dup(wk), dup(wv)], axis=1).astype(BF16)
    cq, sq, ck, sk = _rope_tables(s)
    ones = jnp.ones((lc, LANES), F32)
    zeros = jnp.zeros((lc, LANES), F32)
    n1g = row(norm1_g[1])
    x2, q, kd, vd = _qkv(pos, x1, ys, mod_lat[0], mod_lat[1], n1g, w_ext, cq, sq, ck, sk,
                         ts=ts, tok_stride=s + lc, tok_off=0)
    _, _, kxd, vxd = _qkv(pos, ctx1, ys, mod_ctx[0], mod_ctx[1], n1g, w_ext, ones, zeros, ones, zeros,
                          ts=lc, tok_stride=s + lc, tok_off=s)
    attn = _attention(q, kd, vd, kxd, vxd, attn_sink[0].reshape(1, N_HEADS))
    x3, hrow1, cls1 = _attn_out(attn, x2, mod_lat[1], row(norm2_g[1]), attn_w_o[0].astype(BF16), rwt, rb, ts=ts)
    rank1, counts1 = _plan(cls1)
    pos1, src1, ea1, eb1, nused1 = _dispatch_tables(cls1, rank1, counts1, tm=tm)
    ys1 = _moe(hrow1, src1, ea1, eb1, nused1, wg, wu, wd, tm=tm, layer=1)
    return _final(pos1, x3, ys1, mod_lat[1], row(final_g), ts=ts)
```

```python
import functools
import math

import jax
import jax.numpy as jnp
from jax import lax
from jax.experimental import pallas as pl
from jax.experimental.pallas import tpu as pltpu

D = 1024
GRID_W = 64
CONV_WIDTH = 31
CONV_PAD = CONV_WIDTH // 2
SUBLANES = 8
HALO = 16
HEAD_DIM = 64
N_HEADS = D // HEAD_DIM
N_KV_HEADS = N_HEADS // 4
GROUP = N_HEADS // N_KV_HEADS
KV_DIM = N_KV_HEADS * HEAD_DIM
WINDOW = 128
QBLK = 128
ROPE_THETA = 10000.0
ROPE_PAIRS = HEAD_DIM // 4
N_EXPERTS = 16
N_GROUPS = 4
EXPERTS_PER_GROUP = N_EXPERTS // N_GROUPS
N_PAIRS = 6
N_CLASSES = N_GROUPS * N_PAIRS
D_EXPERT = D // 2
EPS = 1e-6
NEG_INF = -1e30
LOG2E = 1.4426950408889634
LANES = 128
AUX = LANES
ROW_W = D + AUX
COND_ROWS = 16
VMEM_LIMIT = 56 * 1024 * 1024

F32 = jnp.float32
BF16 = jnp.bfloat16
HIGHEST = lax.Precision.HIGHEST


def _cparams(sem, **kw):
    return pltpu.CompilerParams(dimension_semantics=sem, vmem_limit_bytes=VMEM_LIMIT, **kw)


def _sigmoid(v):
    return jax.nn.sigmoid(v)


def _silu(v):
    return v * jax.nn.sigmoid(v)


def _rms_mod(xr, gm, sh):
    r = lax.rsqrt(jnp.mean(xr * xr, axis=-1, keepdims=True) + EPS)
    return xr * r * gm + sh


def _adaln_kernel(cond_ref, w_ref, b_ref, out_ref):
    cnd = cond_ref[...]
    act = _silu(cnd)
    out_ref[0] = jnp.dot(act, w_ref[0], precision=HIGHEST, preferred_element_type=F32) + b_ref[0]


def _adaln(cond, w_mod, b_mod):
    depth = w_mod.shape[0]
    tn = 1536
    return pl.pallas_call(
        _adaln_kernel,
        out_shape=jax.ShapeDtypeStruct((depth, COND_ROWS, 6 * D), F32),
        grid=(depth, 6 * D // tn),
        in_specs=[
            pl.BlockSpec((COND_ROWS, D), lambda l, n: (0, 0)),
            pl.BlockSpec((1, D, tn), lambda l, n: (l, 0, n)),
            pl.BlockSpec((1, 1, tn), lambda l, n: (l, 0, n)),
        ],
        out_specs=pl.BlockSpec((1, COND_ROWS, tn), lambda l, n: (l, 0, n)),
        compiler_params=_cparams(("arbitrary", "arbitrary")),
        name="adaln",
    )(cond, w_mod, b_mod.reshape(depth, 1, 6 * D))


def _route_rows(h2, rwt_ref, rb_ref):
    nt_dims = (((1,), (1,)), ((), ()))
    h_hi = h2.astype(BF16)
    h_lo = (h2 - h_hi.astype(F32)).astype(BF16)
    w_hi = rwt_ref[0]
    w_lo = rwt_ref[1]
    lt = (lax.dot_general(w_hi, h_hi, nt_dims, preferred_element_type=F32)
          + lax.dot_general(w_hi, h_lo, nt_dims, preferred_element_type=F32)
          + lax.dot_general(w_lo, h_hi, nt_dims, preferred_element_type=F32))
    s = _sigmoid(lt)
    sel = s + rb_ref[...]
    srow = [s[e:e + 1, :] for e in range(N_EXPERTS)]
    vrow = [sel[e:e + 1, :] for e in range(N_EXPERTS)]
    gscore = []
    for g in range(N_GROUPS):
        a, b, c, d = vrow[4 * g:4 * g + 4]
        gscore.append(jnp.maximum(jnp.maximum(jnp.maximum(a + b, a + c), jnp.maximum(a + d, b + c)),
                                  jnp.maximum(b + d, c + d)))
    bg = jnp.zeros_like(gscore[0], dtype=jnp.int32)
    best = gscore[0]
    for g in range(1, N_GROUPS):
        better = gscore[g] > best
        bg = jnp.where(better, g, bg)
        best = jnp.where(better, gscore[g], best)
    v = []
    sv = []
    for j in range(EXPERTS_PER_GROUP):
        vj = vrow[j]
        sj = srow[j]
        for g in range(1, N_GROUPS):
            vj = jnp.where(bg == g, vrow[4 * g + j], vj)
            sj = jnp.where(bg == g, srow[4 * g + j], sj)
        v.append(vj)
        sv.append(sj)
    i1 = jnp.zeros_like(bg)
    m1 = v[0]
    for j in range(1, EXPERTS_PER_GROUP):
        gt = v[j] > m1
        i1 = jnp.where(gt, j, i1)
        m1 = jnp.where(gt, v[j], m1)
    v2 = [jnp.where(i1 == j, -jnp.inf, v[j]) for j in range(EXPERTS_PER_GROUP)]
    i2 = jnp.zeros_like(bg)
    m2 = v2[0]
    for j in range(1, EXPERTS_PER_GROUP):
        gt = v2[j] > m2
        i2 = jnp.where(gt, j, i2)
        m2 = jnp.where(gt, v2[j], m2)
    lo = jnp.minimum(i1, i2)
    hi = jnp.maximum(i1, i2)
    s_lo = sv[0]
    s_hi = sv[0]
    for j in range(1, EXPERTS_PER_GROUP):
        s_lo = jnp.where(lo == j, sv[j], s_lo)
        s_hi = jnp.where(hi == j, sv[j], s_hi)
    den = s_lo + s_hi
    pair = jnp.where(lo == 0, 0, jnp.where(lo == 1, 3, 5)) + (hi - lo - 1)
    cls = N_PAIRS * bg + pair
    return s_lo / den, s_hi / den, cls


def _epilogue(x_in, y, g1, sh2, gm2, rwt_ref, rb_ref, hrow_ref, cls_ref):
    x1 = x_in + g1 * y
    h2 = _rms_mod(x1, gm2, sh2)
    ga, gb, cls = _route_rows(h2, rwt_ref, rb_ref)
    ts = h2.shape[0]
    cls_ref[...] = cls
    g8 = jnp.concatenate([ga, gb, jnp.zeros((6, ts), F32)], axis=0)
    g128 = jnp.concatenate([g8, jnp.zeros((AUX - 8, ts), F32)], axis=0)
    hrow_ref[:, :D] = h2
    hrow_ref[:, D:] = jnp.transpose(g128)
    return x1


def _conv_layer_kernel(x_ref, c_ref, xp_ref, xn_ref, modl_ref, modc_ref, n1g_ref, n2g_ref, w1_ref, b1_ref, wdw_ref,
                       bdw_ref, lng_ref, lnb_ref, w2_ref, b2_ref, rwt_ref, rb_ref,
                       x_out_ref, c_out_ref, hrow_ref, cls_ref, u_scr, cv_scr, v_scr, *, ts):
    i = pl.program_id(1)
    nt = pl.num_programs(1) - 1
    is_ctx = jnp.full((1, 1), i, jnp.int32) == nt
    mod = jnp.where(is_ctx, modc_ref[0], modl_ref[0])
    sh1, sc1, g1, sh2, sc2 = (mod[k:k + 1, :] for k in range(5))
    gm1 = n1g_ref[...] * (1.0 + sc1)
    gm2 = n2g_ref[...] * (1.0 + sc2)
    cw = 512

    def glu(hb, c0):
        a = jnp.dot(hb, w1_ref[:, c0:c0 + cw], preferred_element_type=F32) + b1_ref[:, c0:c0 + cw]
        g = jnp.dot(hb, w1_ref[:, D + c0:D + c0 + cw], preferred_element_type=F32) + b1_ref[:, D + c0:D + c0 + cw]
        return a * _sigmoid(g)

    xm = jnp.where(is_ctx, c_ref[0], x_ref[0])
    hb = jnp.concatenate([_rms_mod(xp_ref[0, 0], gm1, sh1).astype(BF16), _rms_mod(xm, gm1, sh1).astype(BF16),
                          _rms_mod(xn_ref[0, 0], gm1, sh1).astype(BF16)], axis=0)
    keep_prev = jnp.logical_and(i > 0, i < nt).astype(F32)
    keep_next = (i < nt - 1).astype(F32)
    for c0 in range(0, D, cw):
        u = glu(hb, c0)
        u_scr[0:HALO, c0:c0 + cw] = u[:HALO] * keep_prev
        u_scr[HALO:HALO + ts, c0:c0 + cw] = u[HALO:HALO + ts]
        u_scr[HALO + ts:2 * HALO + ts, c0:c0 + cw] = u[HALO + ts:] * keep_next

    rr = 64
    taps_off = HALO - CONV_PAD

    def conv_chunk(rc, carry):
        r0 = pl.multiple_of(rc * rr, rr)
        for c0 in range(0, D, LANES):
            win = u_scr[pl.ds(r0, rr + 2 * HALO), c0:c0 + LANES]
            out = None
            for r in range(SUBLANES):
                nrow = rr + SUBLANES if r else rr
                part = None
                for q in range((CONV_WIDTH + taps_off) // SUBLANES + 1):
                    k = SUBLANES * q + r - taps_off
                    if 0 <= k < CONV_WIDTH:
                        term = win[SUBLANES * q:SUBLANES * q + nrow, :] * wdw_ref[k:k + 1, c0:c0 + LANES]
                        part = term if part is None else part + term
                shifted = part[r:r + rr, :]
                out = shifted if out is None else out + shifted
            cv_scr[pl.ds(r0, rr), c0:c0 + LANES] = out
        cv = cv_scr[pl.ds(r0, rr), :] + bdw_ref[...]
        mu = jnp.mean(cv, axis=-1, keepdims=True)
        dv = cv - mu
        var = jnp.mean(dv * dv, axis=-1, keepdims=True)
        ln = dv * lax.rsqrt(var + EPS) * lng_ref[...] + lnb_ref[...]
        v_scr[pl.ds(r0, rr), :] = _silu(ln).astype(BF16)
        return carry

    lax.fori_loop(0, ts // rr, conv_chunk, 0)
    y = jnp.dot(v_scr[...], w2_ref[...], preferred_element_type=F32) + b2_ref[...]
    x1 = _epilogue(xm, y, g1, sh2, gm2, rwt_ref, rb_ref, hrow_ref, cls_ref)

    @pl.when(i < nt)
    def _():
        x_out_ref[0] = x1

    @pl.when(i == nt)
    def _():
        c_out_ref[0] = x1


def _conv_layer(xs, cs, mod_lat, mod_ctx, n1g, n2g, w1, b1, wdw, bdw, lng, lnb, w2, b2, rwt, rb, *, ts):
    bsz, s, _ = xs.shape
    assert cs.shape[1] == ts
    nt = s // ts
    nh = s // HALO
    hpt = ts // HALO
    x4 = xs.reshape(bsz, nh, HALO, D)
    t_all = bsz * (s + ts)
    const = lambda b, i: (0, 0)
    lat = lambda b, i: (b, jnp.minimum(i, nt - 1), 0)
    row_blk = lambda b, i: (b * (nt + 1) + i, 0)
    return pl.pallas_call(
        functools.partial(_conv_layer_kernel, ts=ts),
        out_shape=(jax.ShapeDtypeStruct((bsz, s, D), F32),
                   jax.ShapeDtypeStruct((bsz, ts, D), F32),
                   jax.ShapeDtypeStruct((t_all, ROW_W), F32),
                   jax.ShapeDtypeStruct((1, t_all), jnp.int32)),
        grid=(bsz, nt + 1),
        in_specs=[
            pl.BlockSpec((1, ts, D), lat),
            pl.BlockSpec((1, ts, D), lambda b, i: (b, 0, 0)),
            pl.BlockSpec((1, 1, HALO, D), lambda b, i: (b, jnp.clip(i * hpt - 1, 0, nh - 1), 0, 0)),
            pl.BlockSpec((1, 1, HALO, D), lambda b, i: (b, jnp.minimum((i + 1) * hpt, nh - 1), 0, 0)),
            pl.BlockSpec((1, 6, D), lambda b, i: (b, 0, 0)),
            pl.BlockSpec((1, 6, D), lambda b, i: (0, 0, 0)),
            pl.BlockSpec((1, D), const), pl.BlockSpec((1, D), const),
            pl.BlockSpec((D, 2 * D), const), pl.BlockSpec((1, 2 * D), const),
            pl.BlockSpec((CONV_WIDTH, D), const), pl.BlockSpec((1, D), const),
            pl.BlockSpec((1, D), const), pl.BlockSpec((1, D), const),
            pl.BlockSpec((D, D), const), pl.BlockSpec((1, D), const),
            pl.BlockSpec((2, N_EXPERTS, D), lambda b, i: (0, 0, 0)), pl.BlockSpec((N_EXPERTS, 1), const),
        ],
        out_specs=(pl.BlockSpec((1, ts, D), lat),
                   pl.BlockSpec((1, ts, D), lambda b, i: (b, 0, 0)),
                   pl.BlockSpec((ts, ROW_W), row_blk),
                   pl.BlockSpec((1, ts), lambda b, i: (0, b * (nt + 1) + i))),
        scratch_shapes=[pltpu.VMEM((ts + 2 * HALO, D), F32), pltpu.VMEM((ts, D), F32), pltpu.VMEM((ts, D), BF16)],
        compiler_params=_cparams(("arbitrary", "arbitrary")),
        name="conv_layer",
    )(xs, cs, x4, x4, mod_lat, mod_ctx, n1g, n2g, w1, b1, wdw, bdw, lng, lnb, w2, b2, rwt, rb)


def _plan_kernel(cls_ref, rank_ref, cnt_ref, carry_scr, *, tb):
    @pl.when(pl.program_id(0) == 0)
    def _():
        carry_scr[...] = jnp.zeros_like(carry_scr)

    cls = cls_ref[...]
    onehot = (lax.broadcasted_iota(jnp.int32, (32, tb), 0) == cls).astype(F32)
    upper = (lax.broadcasted_iota(jnp.int32, (tb, tb), 0) <= lax.broadcasted_iota(jnp.int32, (tb, tb), 1))
    prefix = jnp.dot(onehot.astype(BF16), upper.astype(BF16), preferred_element_type=F32)
    carry = carry_scr[:, 0:1]
    rank = jnp.sum(onehot * (prefix + carry), axis=0, keepdims=True) - 1.0
    rank_ref[...] = rank.astype(jnp.int32)
    total = carry + jnp.sum(onehot, axis=1, keepdims=True)
    carry_scr[...] = jnp.broadcast_to(total, carry_scr.shape)
    cnt_ref[...] = jnp.broadcast_to(total, cnt_ref.shape)


def _plan(cls):
    t_all = cls.shape[1]
    tb = math.gcd(t_all, 1024)
    return pl.pallas_call(
        functools.partial(_plan_kernel, tb=tb),
        out_shape=(jax.ShapeDtypeStruct((1, t_all), jnp.int32), jax.ShapeDtypeStruct((32, LANES), F32)),
        grid=(t_all // tb,),
        in_specs=[pl.BlockSpec((1, tb), lambda i: (0, i))],
        out_specs=(pl.BlockSpec((1, tb), lambda i: (0, i)), pl.BlockSpec((32, LANES), lambda i: (0, 0))),
        scratch_shapes=[pltpu.VMEM((32, LANES), F32)],
        compiler_params=_cparams(("arbitrary",)),
        name="route_plan",
    )(cls)


def _invert_kernel(pos_ref, gap_ref, src_ref, *, n_tok, n_gap):
    def zero(i, carry):
        src_ref[i] = 0
        return carry
    for g in range(n_gap):
        lax.fori_loop(gap_ref[2 * g], gap_ref[2 * g + 1], zero, 0)

    unroll = 16

    def put(i, carry):
        for u in range(unroll):
            t = i * unroll + u
            src_ref[pos_ref[t]] = t
        return carry
    lax.fori_loop(0, n_tok // unroll, put, 0)


def _invert(pos, gaps, n_slot):
    n_tok = pos.shape[0]
    return pl.pallas_call(
        functools.partial(_invert_kernel, n_tok=n_tok, n_gap=gaps.shape[0] // 2),
        out_shape=jax.ShapeDtypeStruct((n_slot,), jnp.int32),
        in_specs=[pl.BlockSpec(memory_space=pltpu.SMEM), pl.BlockSpec(memory_space=pltpu.SMEM)],
        out_specs=pl.BlockSpec(memory_space=pltpu.SMEM),
        name="invert_slots",
    )(pos, gaps)


def _dispatch_tables(cls, rank, counts, *, tm):
    t_all = cls.shape[1]
    n_tiles = t_all // tm + N_CLASSES
    cnt = counts[:N_CLASSES, 0].astype(jnp.int32)
    padded = ((cnt + tm - 1) // tm) * tm
    ends = jnp.cumsum(padded)
    offs = ends - padded
    pos = offs[cls[0]] + rank[0]
    gap_lo = jnp.concatenate([offs + cnt, ends[-1:]])
    gap_hi = jnp.concatenate([ends, jnp.full((1,), n_tiles * tm, jnp.int32)])
    src = _invert(pos, jnp.stack([gap_lo, gap_hi], axis=1).reshape(-1).astype(jnp.int32), n_tiles * tm)
    n_used = ends[-1] // tm
    tile_start = jnp.minimum(jnp.arange(n_tiles, dtype=jnp.int32), n_used - 1) * tm
    tile_cls = jnp.minimum(jnp.sum(ends[None, :] <= tile_start[:, None], axis=1), N_CLASSES - 1).astype(jnp.int32)
    pair_lo = jnp.array([0, 0, 0, 1, 1, 2], jnp.int32)
    pair_hi = jnp.array([1, 2, 3, 2, 3, 3], jnp.int32)
    grp = tile_cls // N_PAIRS
    ea = EXPERTS_PER_GROUP * grp + pair_lo[tile_cls % N_PAIRS]
    eb = EXPERTS_PER_GROUP * grp + pair_hi[tile_cls % N_PAIRS]
    return pos, src, ea, eb, n_used.reshape(1).astype(jnp.int32)


def _start_row_gather(idx_ref, idx_base, src_hbm, dst, sem, n, unroll):
    def one(r):
        row = idx_ref[idx_base + r]
        pltpu.make_async_copy(src_hbm.at[pl.ds(row, 1), :], dst.at[pl.ds(r, 1), :], sem).start()

    if unroll:
        for r in range(n):
            one(r)
    else:
        def body(r, carry):
            one(r)
            return carry
        lax.fori_loop(0, n, body, 0)


def _wait_row_gather(src_hbm, dst, sem, n):
    pltpu.make_async_copy(src_hbm.at[pl.ds(0, n), :], dst, sem).wait()


MOE_TILES_PER_STEP = 2


def _moe_kernel(src_ref, ea_ref, eb_ref, nused_ref, hrow_ref, *rest, tm):
    k = MOE_TILES_PER_STEP
    weights = [rest[6 * i:6 * i + 6] for i in range(k)]
    ys_ref = rest[6 * k]
    bufs = rest[6 * k + 1:6 * k + 1 + k]
    sem = rest[6 * k + 1 + k]
    j = pl.program_id(0)
    nused = nused_ref[0]

    def gather(tile, i, unroll):
        @pl.when(tile < nused)
        def _():
            _start_row_gather(src_ref, tile * tm, hrow_ref, bufs[i], sem.at[i], tm, unroll=unroll)

    @pl.when(j == 0)
    def _():
        for i in range(k):
            gather(i, i, unroll=False)

    for i in range(k):
        tile = j * k + i
        wga, wua, wda, wgb, wub, wdb = weights[i]
        out = ys_ref.at[pl.ds(i * tm, tm), :]

        @pl.when(tile < nused)
        def _():
            _wait_row_gather(hrow_ref, bufs[i], sem.at[i], tm)
            h = bufs[i][:, :D].astype(BF16)
            ga = bufs[i][:, D:D + 1]
            gb = bufs[i][:, D + 1:D + 2]

            def expert(wg, wu, wd):
                a = jnp.dot(h, wg[0], preferred_element_type=F32)
                u = jnp.dot(h, wu[0], preferred_element_type=F32)
                act = (_silu(a) * u).astype(BF16)
                return jnp.dot(act, wd[0], preferred_element_type=F32)

            out[...] = ga * expert(wga, wua, wda) + gb * expert(wgb, wub, wdb)

        @pl.when(tile >= nused)
        def _():
            out[...] = jnp.zeros((tm, D), F32)

        gather(tile + k, i, unroll=True)


def _moe(hrow, src, ea, eb, nused, wg, wu, wd, *, tm, layer):
    k = MOE_TILES_PER_STEP
    n_tiles = ea.shape[0]
    assert n_tiles % k == 0
    f = wg.shape[-1]
    e0 = layer * N_EXPERTS
    w_specs = []
    for i in range(k):
        wa = lambda j, src, ea, eb, nu, i=i: (e0 + ea[j * k + i], 0, 0)
        wb = lambda j, src, ea, eb, nu, i=i: (e0 + eb[j * k + i], 0, 0)
        w_specs += [pl.BlockSpec((1, D, f), wa), pl.BlockSpec((1, D, f), wa), pl.BlockSpec((1, f, D), wa),
                    pl.BlockSpec((1, D, f), wb), pl.BlockSpec((1, D, f), wb), pl.BlockSpec((1, f, D), wb)]
    return pl.pallas_call(
        functools.partial(_moe_kernel, tm=tm),
        out_shape=jax.ShapeDtypeStruct((n_tiles * tm, D), F32),
        grid_spec=pltpu.PrefetchScalarGridSpec(
            num_scalar_prefetch=4,
            grid=(n_tiles // k,),
            in_specs=[pl.BlockSpec(memory_space=pl.ANY)] + w_specs,
            out_specs=pl.BlockSpec((k * tm, D), lambda j, src, ea, eb, nu: (j, 0)),
            scratch_shapes=[pltpu.VMEM((tm, ROW_W), F32)] * k + [pltpu.SemaphoreType.DMA((k,))],
        ),
        compiler_params=_cparams(("arbitrary",), disable_bounds_checks=True),
        name="moe_experts",
    )(src, ea, eb, nused, hrow, *([wg, wu, wd, wg, wu, wd] * k))


def _combine_base(step, *, ts, tok_stride, tok_off):
    nt = pl.num_programs(1)
    return (step // nt) * tok_stride + tok_off + (step % nt) * ts


def _combine_gather(pos_ref, ys_ref, ybuf, sem, *, ts, tok_stride, tok_off):
    nt = pl.num_programs(1)
    n = pl.program_id(0) * nt + pl.program_id(1)

    @pl.when(n == 0)
    def _():
        _start_row_gather(pos_ref, _combine_base(0, ts=ts, tok_stride=tok_stride, tok_off=tok_off), ys_ref,
                          ybuf.at[0], sem.at[0], ts, unroll=False)

    nxt = jnp.minimum(n + 1, pl.num_programs(0) * nt - 1)
    _start_row_gather(pos_ref, _combine_base(nxt, ts=ts, tok_stride=tok_stride, tok_off=tok_off), ys_ref,
                      ybuf.at[1 - n % 2], sem.at[1 - n % 2], ts, unroll=True)
    _wait_row_gather(ys_ref, ybuf.at[n % 2], sem.at[n % 2], ts)
    return ybuf[n % 2]


def _combine_drain(ys_ref, ybuf, sem, ts):
    nt = pl.num_programs(1)
    n = pl.program_id(0) * nt + pl.program_id(1)

    @pl.when(n == pl.num_programs(0) * nt - 1)
    def _():
        _wait_row_gather(ys_ref, ybuf.at[1 - n % 2], sem.at[1 - n % 2], ts)


def _rope(v, cos, sin):
    lane = lax.broadcasted_iota(jnp.int32, v.shape, 1)
    first = (lane % (2 * ROPE_PAIRS)) < ROPE_PAIRS
    partner = jnp.where(first, pltpu.roll(v, LANES - ROPE_PAIRS, 1), pltpu.roll(v, ROPE_PAIRS, 1))
    return v * cos + partner * sin


def _qkv_kernel(pos_ref, x_ref, ys_ref, modp_ref, mod_ref, n1g_ref, w_ref, cq_ref, sq_ref, ck_ref, sk_ref,
                x_out_ref, q_ref, kd_ref, vd_ref, ybuf, sem, *, ts, seq, tok_stride, tok_off):
    y = _combine_gather(pos_ref, ys_ref, ybuf, sem, ts=ts, tok_stride=tok_stride, tok_off=tok_off)
    g2_prev = modp_ref[0][5:6, :]
    x2 = x_ref[0] + g2_prev * y
    x_out_ref[0] = x2
    mod = mod_ref[0]
    sh1 = mod[0:1, :]
    gm1 = n1g_ref[...] * (1.0 + mod[1:2, :])
    hb = _rms_mod(x2, gm1, sh1).astype(BF16)
    cq, sq, ck, sk = cq_ref[...], sq_ref[...], ck_ref[...], sk_ref[...]
    nw = 2 * LANES
    for c0 in range(0, D, nw):
        qc = jnp.dot(hb, w_ref[:, c0:c0 + nw], preferred_element_type=F32)
        for l0 in range(0, nw, LANES):
            q_ref[0, :, c0 + l0:c0 + l0 + LANES] = _rope(qc[:, l0:l0 + LANES], cq, sq).astype(BF16)
    for c0 in range(0, 2 * KV_DIM, nw):
        kc = jnp.dot(hb, w_ref[:, D + c0:D + c0 + nw], preferred_element_type=F32)
        for l0 in range(0, nw, LANES):
            kd_ref[0, :, c0 + l0:c0 + l0 + LANES] = _rope(kc[:, l0:l0 + LANES], ck, sk).astype(BF16)
    vd_ref[0] = jnp.dot(hb, w_ref[:, D + 2 * KV_DIM:], preferred_element_type=F32).astype(BF16)
    _combine_drain(ys_ref, ybuf, sem, ts)


def _qkv(pos, xs, ys, modp, mod, n1g, w_ext, cq, sq, ck, sk, *, ts, tok_stride, tok_off):
    bsz, s, _ = xs.shape
    nt = s // ts
    per_batch = mod.shape[0] > 1
    mod_map = (lambda b, i, p: (b, 0, 0)) if per_batch else (lambda b, i, p: (0, 0, 0))
    const = lambda b, i, p: (0, 0)
    tok = lambda b, i, p: (b, i, 0)
    tab = lambda b, i, p: (i, 0)
    return pl.pallas_call(
        functools.partial(_qkv_kernel, ts=ts, seq=s, tok_stride=tok_stride, tok_off=tok_off),
        out_shape=(jax.ShapeDtypeStruct((bsz, s, D), F32), jax.ShapeDtypeStruct((bsz, s, D), BF16),
                   jax.ShapeDtypeStruct((bsz, s, 2 * KV_DIM), BF16), jax.ShapeDtypeStruct((bsz, s, 2 * KV_DIM), BF16)),
        grid_spec=pltpu.PrefetchScalarGridSpec(
            num_scalar_prefetch=1,
            grid=(bsz, nt),
            in_specs=[
                pl.BlockSpec((1, ts, D), tok),
                pl.BlockSpec(memory_space=pl.ANY),
                pl.BlockSpec((1, 6, D), mod_map), pl.BlockSpec((1, 6, D), mod_map),
                pl.BlockSpec((1, D), const),
                pl.BlockSpec((D, D + 4 * KV_DIM), const),
                pl.BlockSpec((ts, LANES), tab), pl.BlockSpec((ts, LANES), tab),
                pl.BlockSpec((ts, LANES), tab), pl.BlockSpec((ts, LANES), tab),
            ],
            out_specs=(pl.BlockSpec((1, ts, D), tok), pl.BlockSpec((1, ts, D), tok),
                       pl.BlockSpec((1, ts, 2 * KV_DIM), tok), pl.BlockSpec((1, ts, 2 * KV_DIM), tok)),
            scratch_shapes=[pltpu.VMEM((2, ts, D), F32), pltpu.SemaphoreType.DMA((2,))],
        ),
        compiler_params=_cparams(("arbitrary", "arbitrary"), disable_bounds_checks=True),
        name="combine_qkv",
    )(pos, xs, ys, modp, mod, n1g, w_ext, cq, sq, ck, sk)


ATTN_QBLOCKS = 2


def _attn_kernel(q_ref, *rest, lc):
    nq = ATTN_QBLOCKS
    k_refs = rest[:nq + 2]
    v_refs = rest[nq + 2:2 * nq + 4]
    kx_ref, vx_ref, sink_ref, o_ref = rest[2 * nq + 4:]
    i = pl.program_id(1)
    nsteps = pl.num_programs(1)
    nband = 3 * QBLK
    qrow = lax.broadcasted_iota(jnp.int32, (GROUP * QBLK, nband), 0) % QBLK
    kcol = lax.broadcasted_iota(jnp.int32, (GROUP * QBLK, nband), 1)
    lane = lax.broadcasted_iota(jnp.int32, (1, LANES), 1)
    low = lane < HEAD_DIM
    units = [(qb, h) for qb in range(nq) for h in range(N_KV_HEADS)]

    def band_ok(qb):
        lo_lim = jnp.where(i > 0, 0, QBLK) if qb == 0 else 0
        hi_lim = jnp.where(i < nsteps - 1, nband, 2 * QBLK - 1) if qb == nq - 1 else nband
        return (kcol >= jnp.maximum(qrow, lo_lim)) & (kcol <= jnp.minimum(qrow + 2 * WINDOW, hi_lim))

    ok = [band_ok(qb) for qb in range(nq)]

    def scores(unit):
        qb, h = unit
        hs = slice(h * LANES, (h + 1) * LANES)
        k_all = jnp.concatenate([r[0, :, hs] for r in k_refs[qb:qb + 3]] + [kx_ref[0, :, hs]], axis=0)
        qs = []
        for j in range(GROUP):
            c0 = h * GROUP * HEAD_DIM + (j // 2) * LANES
            qh = q_ref[0, qb * QBLK:(qb + 1) * QBLK, c0:c0 + LANES]
            keep = low if j % 2 == 0 else jnp.logical_not(low)
            qs.append(jnp.where(keep, qh, jnp.zeros_like(qh)))
        qm = jnp.concatenate(qs, axis=0)
        return lax.dot_general(qm, k_all, (((1,), (1,)), ((), ())), preferred_element_type=F32)

    def softmax(unit, s):
        qb, h = unit
        s = jnp.concatenate([jnp.where(ok[qb], s[:, :nband], NEG_INF), s[:, nband:]], axis=1)
        sink = LOG2E * jnp.concatenate(
            [jnp.broadcast_to(sink_ref[:, h * GROUP + j:h * GROUP + j + 1], (QBLK, 1)) for j in range(GROUP)], axis=0)
        nblk = s.shape[1] // LANES
        smax = s[:, :LANES]
        for kb in range(1, nblk):
            smax = jnp.maximum(smax, s[:, kb * LANES:(kb + 1) * LANES])
        m = jnp.maximum(jnp.max(smax, axis=1, keepdims=True), sink)
        return jnp.exp2(s - m).astype(BF16), jnp.exp2(sink - m)

    def weighted_values(unit, p, p_sink):
        qb, h = unit
        hs = slice(h * LANES, (h + 1) * LANES)
        v_all = jnp.concatenate([r[0, :, hs] for r in v_refs[qb:qb + 3]] + [vx_ref[0, :, hs]], axis=0)
        ones_lo = jnp.broadcast_to(jnp.where(lane == HEAD_DIM, 1.0, 0.0).astype(v_all.dtype), v_all.shape)
        ones_hi = jnp.broadcast_to(jnp.where(lane == 0, 1.0, 0.0).astype(v_all.dtype), v_all.shape)
        v_lo = jnp.where(low, v_all, ones_lo)
        v_hi = jnp.where(low, ones_hi, v_all)
        for pr in range(GROUP // 2):
            r0 = 2 * pr * QBLK
            o_lo = jnp.dot(p[r0:r0 + QBLK], v_lo, preferred_element_type=F32)
            o_hi = jnp.dot(p[r0 + QBLK:r0 + 2 * QBLK], v_hi, preferred_element_type=F32)
            inv_lo = 1.0 / (o_lo[:, HEAD_DIM:HEAD_DIM + 1] + p_sink[r0:r0 + QBLK])
            inv_hi = 1.0 / (o_hi[:, 0:1] + p_sink[r0 + QBLK:r0 + 2 * QBLK])
            c0 = h * GROUP * HEAD_DIM + pr * LANES
            o_ref[0, qb * QBLK:(qb + 1) * QBLK, c0:c0 + LANES] = (
                jnp.where(low, o_lo * inv_lo, o_hi * inv_hi).astype(BF16))

    n = len(units)
    s_all = [scores(units[0]), scores(units[1])]
    p_cur = softmax(units[0], s_all[0])
    for u in range(n):
        if u + 2 < n:
            s_all.append(scores(units[u + 2]))
        p_next = softmax(units[u + 1], s_all[u + 1]) if u + 1 < n else None
        weighted_values(units[u], *p_cur)
        p_cur = p_next


def _attention(q, kd, vd, kxd, vxd, sink):
    bsz, s, _ = q.shape
    lc = kxd.shape[1]
    nq = ATTN_QBLOCKS
    nb = s // QBLK
    assert nb % nq == 0
    tok = lambda b, i: (b, i, 0)
    ctx = lambda b, i: (b, 0, 0)
    kvw = 2 * KV_DIM
    band = [pl.BlockSpec((1, QBLK, kvw), lambda b, i, d=d: (b, jnp.clip(nq * i - 1 + d, 0, nb - 1), 0))
            for d in range(nq + 2)]
    return pl.pallas_call(
        functools.partial(_attn_kernel, lc=lc),
        out_shape=jax.ShapeDtypeStruct((bsz, s, D), BF16),
        grid=(bsz, nb // nq),
        in_specs=[pl.BlockSpec((1, nq * QBLK, D), tok)] + band + band + [
            pl.BlockSpec((1, lc, kvw), ctx), pl.BlockSpec((1, lc, kvw), ctx),
            pl.BlockSpec((1, N_HEADS), lambda b, i: (0, 0)),
        ],
        out_specs=pl.BlockSpec((1, nq * QBLK, D), tok),
        compiler_params=_cparams(("arbitrary", "arbitrary")),
        name="window_attention",
    )(q, *([kd] * (nq + 2)), *([vd] * (nq + 2)), kxd, vxd, sink)


def _attn_out_kernel(a_ref, x_ref, mod_ref, n2g_ref, wo_ref, rwt_ref, rb_ref, x_out_ref, hrow_ref, cls_ref):
    mod = mod_ref[0]
    g1, sh2, sc2 = mod[2:3, :], mod[3:4, :], mod[4:5, :]
    gm2 = n2g_ref[...] * (1.0 + sc2)
    y = jnp.dot(a_ref[0], wo_ref[...], preferred_element_type=F32)
    x_out_ref[0] = _epilogue(x_ref[0], y, g1, sh2, gm2, rwt_ref, rb_ref, hrow_ref, cls_ref)


def _attn_out(attn, xs, mod, n2g, wo, rwt, rb, *, ts):
    bsz, s, _ = xs.shape
    nt = s // ts
    t_all = bsz * s
    const = lambda b, i: (0, 0)
    tok = lambda b, i: (b, i, 0)
    return pl.pallas_call(
        _attn_out_kernel,
        out_shape=(jax.ShapeDtypeStruct((bsz, s, D), F32),
                   jax.ShapeDtypeStruct((t_all, ROW_W), F32),
                   jax.ShapeDtypeStruct((1, t_all), jnp.int32)),
        grid=(bsz, nt),
        in_specs=[
            pl.BlockSpec((1, ts, D), tok), pl.BlockSpec((1, ts, D), tok),
            pl.BlockSpec((1, 6, D), lambda b, i: (b, 0, 0)),
            pl.BlockSpec((1, D), const), pl.BlockSpec((D, D), const),
            pl.BlockSpec((2, N_EXPERTS, D), lambda b, i: (0, 0, 0)), pl.BlockSpec((N_EXPERTS, 1), const),
        ],
        out_specs=(pl.BlockSpec((1, ts, D), tok),
                   pl.BlockSpec((ts, ROW_W), lambda b, i: (b * nt + i, 0)),
                   pl.BlockSpec((1, ts), lambda b, i: (0, b * nt + i))),
        compiler_params=_cparams(("arbitrary", "arbitrary")),
        name="attn_out",
    )(attn, xs, mod, n2g, wo, rwt, rb)


def _final_kernel(pos_ref, x_ref, ys_ref, mod_ref, fg_ref, o_ref, ybuf, sem, *, ts, seq):
    y = _combine_gather(pos_ref, ys_ref, ybuf, sem, ts=ts, tok_stride=seq, tok_off=0)
    x4 = x_ref[0] + mod_ref[0][5:6, :] * y
    r = lax.rsqrt(jnp.mean(x4 * x4, axis=-1, keepdims=True) + EPS)
    o_ref[0] = x4 * r * fg_ref[...]
    _combine_drain(ys_ref, ybuf, sem, ts)


def _final(pos, xs, ys, mod, fg, *, ts):
    bsz, s, _ = xs.shape
    tok = lambda b, i, p: (b, i, 0)
    return pl.pallas_call(
        functools.partial(_final_kernel, ts=ts, seq=s),
        out_shape=jax.ShapeDtypeStruct((bsz, s, D), F32),
        grid_spec=pltpu.PrefetchScalarGridSpec(
            num_scalar_prefetch=1,
            grid=(bsz, s // ts),
            in_specs=[pl.BlockSpec((1, ts, D), tok), pl.BlockSpec(memory_space=pl.ANY),
                      pl.BlockSpec((1, 6, D), lambda b, i, p: (b, 0, 0)), pl.BlockSpec((1, D), lambda b, i, p: (0, 0))],
            out_specs=pl.BlockSpec((1, ts, D), tok),
            scratch_shapes=[pltpu.VMEM((2, ts, D), F32), pltpu.SemaphoreType.DMA((2,))],
        ),
        compiler_params=_cparams(("arbitrary", "arbitrary"), disable_bounds_checks=True),
        name="combine_final",
    )(pos, xs, ys, mod, fg)


def _rope_tables(seq):
    rows = seq // GRID_W
    inv_freq = jnp.power(jnp.float32(ROPE_THETA), -jnp.arange(ROPE_PAIRS, dtype=F32) / ROPE_PAIRS)
    ang_r = jnp.arange(rows, dtype=jnp.int32).astype(F32)[:, None] * inv_freq
    ang_c = jnp.arange(GRID_W, dtype=jnp.int32).astype(F32)[:, None] * inv_freq
    by_row = lambda v: jnp.repeat(v, GRID_W, axis=0)
    by_col = lambda v: jnp.tile(v, (rows, 1))
    cos_r, sin_r = by_row(jnp.cos(ang_r)), by_row(jnp.sin(ang_r))
    cos_c, sin_c = by_col(jnp.cos(ang_c)), by_col(jnp.sin(ang_c))
    cos = jnp.concatenate([cos_r, cos_r, cos_c, cos_c], axis=1)
    sin = jnp.concatenate([-sin_r, sin_r, -sin_c, sin_c], axis=1)
    cos = jnp.concatenate([cos, cos], axis=1)
    sin = jnp.concatenate([sin, sin], axis=1)
    scale = HEAD_DIM ** -0.5 * LOG2E
    return cos * scale, sin * scale, cos, sin


def kernel(x, c, ctx, c_ctx, w_mod, b_mod, norm1_g, norm2_g, conv_w_pw1, conv_b_pw1, conv_w_dw, conv_b_dw,
           conv_ln_g, conv_ln_b, conv_w_pw2, conv_b_pw2, attn_w_qkv, attn_w_o, attn_sink, router_w, router_b,
           moe_w_gate, moe_w_up, moe_w_down, final_g):
    bsz, s, d = x.shape
    lc = ctx.shape[1]
    ts0 = 256
    ts = 512
    tm = 256
    assert d == D and w_mod.shape[0] == 2 and bsz + 1 <= COND_ROWS
    assert lc == ts0 and s % ts == 0 and s % QBLK == 0

    cond = jnp.concatenate([c, c_ctx[None, :], jnp.zeros((COND_ROWS - bsz - 1, D), F32)], axis=0)
    mods = _adaln(cond, w_mod, b_mod).reshape(2, COND_ROWS, 6, D)
    mod_lat = [mods[l, :bsz] for l in range(2)]
    mod_ctx = [mods[l, bsz:bsz + 1] for l in range(2)]

    rwt_hi = router_w.T.astype(BF16)
    rwt = jnp.stack([rwt_hi, (router_w.T - rwt_hi.astype(F32)).astype(BF16)])
    rb = router_b.reshape(N_EXPERTS, 1)
    row = lambda v: v.reshape(1, -1)
    wg = moe_w_gate.astype(BF16).reshape(2 * N_EXPERTS, D, D_EXPERT)
    wu = moe_w_up.astype(BF16).reshape(2 * N_EXPERTS, D, D_EXPERT)
    wd = moe_w_down.astype(BF16).reshape(2 * N_EXPERTS, D_EXPERT, D)

    x1, ctx1, hrow, cls = _conv_layer(
        x, ctx, mod_lat[0], mod_ctx[0], row(norm1_g[0]), row(norm2_g[0]), conv_w_pw1[0].astype(BF16),
        row(conv_b_pw1[0]), conv_w_dw[0], row(conv_b_dw[0]), row(conv_ln_g[0]), row(conv_ln_b[0]),
        conv_w_pw2[0].astype(BF16), row(conv_b_pw2[0]), rwt, rb, ts=ts0)
    rank, counts = _plan(cls)
    pos, src, ea, eb, nused = _dispatch_tables(cls, rank, counts, tm=tm)
    ys = _moe(hrow, src, ea, eb, nused, wg, wu, wd, tm=tm, layer=0)

    wq = attn_w_qkv[0][:, :D]
    wk = attn_w_qkv[0][:, D:D + KV_DIM].reshape(D, N_KV_HEADS, 1, HEAD_DIM)
    wv = attn_w_qkv[0][:, D + KV_DIM:].reshape(D, N_KV_HEADS, 1, HEAD_DIM)
    dup = lambda w: jnp.broadcast_to(w, (D, N_KV_HEADS, 2, HEAD_DIM)).reshape(D, 2 * KV_DIM)
    w_ext = jnp.concatenate([wq, dup(wk), dup(wv)], axis=1).astype(BF16)
    cq, sq, ck, sk = _rope_tables(s)
    ones = jnp.ones((lc, LANES), F32)
    zeros = jnp.zeros((lc, LANES), F32)
    n1g = row(norm1_g[1])
    x2, q, kd, vd = _qkv(pos, x1, ys, mod_lat[0], mod_lat[1], n1g, w_ext, cq, sq, ck, sk,
                         ts=ts, tok_stride=s + lc, tok_off=0)
    _, _, kxd, vxd = _qkv(pos, ctx1, ys, mod_ctx[0], mod_ctx[1], n1g, w_ext, ones, zeros, ones, zeros,
                          ts=lc, tok_stride=s + lc, tok_off=s)
    attn = _attention(q, kd, vd, kxd, vxd, attn_sink[0].reshape(1, N_HEADS))
    x3, hrow1, cls1 = _attn_out(attn, x2, mod_lat[1], row(norm2_g[1]), attn_w_o[0].astype(BF16), rwt, rb, ts=ts)
    rank1, counts1 = _plan(cls1)
    pos1, src1, ea1, eb1, nused1 = _dispatch_tables(cls1, rank1, counts1, tm=tm)
    ys1 = _moe(hrow1, src1, ea1, eb1, nused1, wg, wu, wd, tm=tm, layer=1)
    return _final(pos1, x3, ys1, mod_lat[1], row(final_g), ts=ts)
```

```python
import functools
import math

import jax
import jax.numpy as jnp
from jax import lax
from jax.experimental import pallas as pl
from jax.experimental.pallas import tpu as pltpu

D = 1024
GRID_W = 64
CONV_WIDTH = 31
CONV_PAD = CONV_WIDTH // 2
SUBLANES = 8
HALO = 16
HEAD_DIM = 64
N_HEADS = D // HEAD_DIM
N_KV_HEADS = N_HEADS // 4
GROUP = N_HEADS // N_KV_HEADS
KV_DIM = N_KV_HEADS * HEAD_DIM
WINDOW = 128
QBLK = 128
ROPE_THETA = 10000.0
ROPE_PAIRS = HEAD_DIM // 4
N_EXPERTS = 16
N_GROUPS = 4
EXPERTS_PER_GROUP = N_EXPERTS // N_GROUPS
N_PAIRS = 6
N_CLASSES = N_GROUPS * N_PAIRS
CLASS_ROWS = 32
NO_CLASS = CLASS_ROWS - 1
D_EXPERT = D // 2
EPS = 1e-6
NEG_INF = -1e30
LOG2E = 1.4426950408889634
LANES = 128
AUX = LANES
ROW_W = D + AUX
COND_ROWS = 16
VMEM_LIMIT = 56 * 1024 * 1024

F32 = jnp.float32
BF16 = jnp.bfloat16
HIGHEST = lax.Precision.HIGHEST


def _cparams(sem, **kw):
    return pltpu.CompilerParams(dimension_semantics=sem, vmem_limit_bytes=VMEM_LIMIT, **kw)


def _sigmoid(v):
    return jax.nn.sigmoid(v)


def _silu(v):
    return v * jax.nn.sigmoid(v)


def _rms_mod(xr, gm, sh):
    r = lax.rsqrt(jnp.mean(xr * xr, axis=-1, keepdims=True) + EPS)
    return xr * r * gm + sh


def _adaln_kernel(cond_ref, w_ref, b_ref, out_ref):
    cnd = cond_ref[...]
    act = _silu(cnd)
    out_ref[0] = jnp.dot(act, w_ref[0], precision=HIGHEST, preferred_element_type=F32) + b_ref[0]


def _adaln(cond, w_mod, b_mod):
    depth = w_mod.shape[0]
    tn = 1536
    return pl.pallas_call(
        _adaln_kernel,
        out_shape=jax.ShapeDtypeStruct((depth, COND_ROWS, 6 * D), F32),
        grid=(depth, 6 * D // tn),
        in_specs=[
            pl.BlockSpec((COND_ROWS, D), lambda l, n: (0, 0)),
            pl.BlockSpec((1, D, tn), lambda l, n: (l, 0, n)),
            pl.BlockSpec((1, 1, tn), lambda l, n: (l, 0, n)),
        ],
        out_specs=pl.BlockSpec((1, COND_ROWS, tn), lambda l, n: (l, 0, n)),
        compiler_params=_cparams(("arbitrary", "arbitrary")),
        name="adaln",
    )(cond, w_mod, b_mod.reshape(depth, 1, 6 * D))


def _route_rows(h2, rwt_ref, rb_ref):
    nt_dims = (((1,), (1,)), ((), ()))
    h_hi = h2.astype(BF16)
    h_lo = (h2 - h_hi.astype(F32)).astype(BF16)
    w_hi = rwt_ref[0]
    w_lo = rwt_ref[1]
    lt = (lax.dot_general(w_hi, h_hi, nt_dims, preferred_element_type=F32)
          + lax.dot_general(w_hi, h_lo, nt_dims, preferred_element_type=F32)
          + lax.dot_general(w_lo, h_hi, nt_dims, preferred_element_type=F32))
    s = _sigmoid(lt)
    sel = s + rb_ref[...]
    srow = [s[e:e + 1, :] for e in range(N_EXPERTS)]
    vrow = [sel[e:e + 1, :] for e in range(N_EXPERTS)]
    gscore = []
    for g in range(N_GROUPS):
        a, b, c, d = vrow[4 * g:4 * g + 4]
        gscore.append(jnp.maximum(jnp.maximum(jnp.maximum(a + b, a + c), jnp.maximum(a + d, b + c)),
                                  jnp.maximum(b + d, c + d)))
    bg = jnp.zeros_like(gscore[0], dtype=jnp.int32)
    best = gscore[0]
    for g in range(1, N_GROUPS):
        better = gscore[g] > best
        bg = jnp.where(better, g, bg)
        best = jnp.where(better, gscore[g], best)
    v = []
    sv = []
    for j in range(EXPERTS_PER_GROUP):
        vj = vrow[j]
        sj = srow[j]
        for g in range(1, N_GROUPS):
            vj = jnp.where(bg == g, vrow[4 * g + j], vj)
            sj = jnp.where(bg == g, srow[4 * g + j], sj)
        v.append(vj)
        sv.append(sj)
    i1 = jnp.zeros_like(bg)
    m1 = v[0]
    for j in range(1, EXPERTS_PER_GROUP):
        gt = v[j] > m1
        i1 = jnp.where(gt, j, i1)
        m1 = jnp.where(gt, v[j], m1)
    v2 = [jnp.where(i1 == j, -jnp.inf, v[j]) for j in range(EXPERTS_PER_GROUP)]
    i2 = jnp.zeros_like(bg)
    m2 = v2[0]
    for j in range(1, EXPERTS_PER_GROUP):
        gt = v2[j] > m2
        i2 = jnp.where(gt, j, i2)
        m2 = jnp.where(gt, v2[j], m2)
    lo = jnp.minimum(i1, i2)
    hi = jnp.maximum(i1, i2)
    s_lo = sv[0]
    s_hi = sv[0]
    for j in range(1, EXPERTS_PER_GROUP):
        s_lo = jnp.where(lo == j, sv[j], s_lo)
        s_hi = jnp.where(hi == j, sv[j], s_hi)
    den = s_lo + s_hi
    pair = jnp.where(lo == 0, 0, jnp.where(lo == 1, 3, 5)) + (hi - lo - 1)
    cls = N_PAIRS * bg + pair
    return s_lo / den, s_hi / den, cls


def _epilogue(x_in, y, g1, sh2, gm2, rwt_ref, rb_ref, hrow_ref, cls_ref, n_valid=None):
    x1 = x_in + g1 * y
    h2 = _rms_mod(x1, gm2, sh2)
    ga, gb, cls = _route_rows(h2, rwt_ref, rb_ref)
    ts = h2.shape[0]
    if n_valid is not None:
        cls = jnp.where(lax.broadcasted_iota(jnp.int32, cls.shape, 1) < n_valid, cls, NO_CLASS)
    cls_ref[...] = cls
    g8 = jnp.concatenate([ga, gb, jnp.zeros((6, ts), F32)], axis=0)
    g128 = jnp.concatenate([g8, jnp.zeros((AUX - 8, ts), F32)], axis=0)
    hrow_ref[:, :D] = h2
    hrow_ref[:, D:] = jnp.transpose(g128)
    return x1


def _conv_layer_kernel(x_ref, c_ref, xp_ref, xn_ref, modl_ref, modc_ref, n1g_ref, n2g_ref, w1_ref, b1_ref, wdw_ref,
                       bdw_ref, lng_ref, lnb_ref, w2_ref, b2_ref, rwt_ref, rb_ref,
                       x_out_ref, c_out_ref, hrow_ref, cls_ref, u_scr, cv_scr, v_scr, *, ts, n_ctx):
    i = pl.program_id(1)
    nt = pl.num_programs(1) - 1
    is_ctx = jnp.full((1, 1), i, jnp.int32) == nt
    n_valid = jnp.where(i == nt, n_ctx, ts)
    mod = jnp.where(is_ctx, modc_ref[0], modl_ref[0])
    sh1, sc1, g1, sh2, sc2 = (mod[k:k + 1, :] for k in range(5))
    gm1 = n1g_ref[...] * (1.0 + sc1)
    gm2 = n2g_ref[...] * (1.0 + sc2)
    cw = 512

    def glu(hb, c0):
        a = jnp.dot(hb, w1_ref[:, c0:c0 + cw], preferred_element_type=F32) + b1_ref[:, c0:c0 + cw]
        g = jnp.dot(hb, w1_ref[:, D + c0:D + c0 + cw], preferred_element_type=F32) + b1_ref[:, D + c0:D + c0 + cw]
        return a * _sigmoid(g)

    xm = jnp.where(is_ctx, c_ref[0], x_ref[0])
    hb = jnp.concatenate([_rms_mod(xp_ref[0, 0], gm1, sh1).astype(BF16), _rms_mod(xm, gm1, sh1).astype(BF16),
                          _rms_mod(xn_ref[0, 0], gm1, sh1).astype(BF16)], axis=0)
    keep_prev = jnp.logical_and(i > 0, i < nt).astype(F32)
    keep_next = (i < nt - 1).astype(F32)
    keep_row = (lax.broadcasted_iota(jnp.int32, (ts, 1), 0) < n_valid).astype(F32)
    for c0 in range(0, D, cw):
        u = glu(hb, c0)
        u_scr[0:HALO, c0:c0 + cw] = u[:HALO] * keep_prev
        u_scr[HALO:HALO + ts, c0:c0 + cw] = u[HALO:HALO + ts] * keep_row
        u_scr[HALO + ts:2 * HALO + ts, c0:c0 + cw] = u[HALO + ts:] * keep_next

    rr = 64
    taps_off = HALO - CONV_PAD

    def conv_chunk(rc, carry):
        r0 = pl.multiple_of(rc * rr, rr)
        for c0 in range(0, D, LANES):
            win = u_scr[pl.ds(r0, rr + 2 * HALO), c0:c0 + LANES]
            out = None
            for r in range(SUBLANES):
                nrow = rr + SUBLANES if r else rr
                part = None
                for q in range((CONV_WIDTH + taps_off) // SUBLANES + 1):
                    k = SUBLANES * q + r - taps_off
                    if 0 <= k < CONV_WIDTH:
                        term = win[SUBLANES * q:SUBLANES * q + nrow, :] * wdw_ref[k:k + 1, c0:c0 + LANES]
                        part = term if part is None else part + term
                shifted = part[r:r + rr, :]
                out = shifted if out is None else out + shifted
            cv_scr[pl.ds(r0, rr), c0:c0 + LANES] = out
        cv = cv_scr[pl.ds(r0, rr), :] + bdw_ref[...]
        mu = jnp.mean(cv, axis=-1, keepdims=True)
        dv = cv - mu
        var = jnp.mean(dv * dv, axis=-1, keepdims=True)
        ln = dv * lax.rsqrt(var + EPS) * lng_ref[...] + lnb_ref[...]
        v_scr[pl.ds(r0, rr), :] = _silu(ln).astype(BF16)
        return carry

    lax.fori_loop(0, ts // rr, conv_chunk, 0)
    y = jnp.dot(v_scr[...], w2_ref[...], preferred_element_type=F32) + b2_ref[...]
    x1 = _epilogue(xm, y, g1, sh2, gm2, rwt_ref, rb_ref, hrow_ref, cls_ref, n_valid=n_valid)

    @pl.when(i < nt)
    def _():
        x_out_ref[0] = x1

    @pl.when(i == nt)
    def _():
        c_out_ref[0] = x1


def _conv_layer(xs, cs, mod_lat, mod_ctx, n1g, n2g, w1, b1, wdw, bdw, lng, lnb, w2, b2, rwt, rb, *, ts):
    bsz, s, _ = xs.shape
    n_ctx = cs.shape[1]
    assert n_ctx <= ts
    cs = jnp.pad(cs, ((0, 0), (0, ts - n_ctx), (0, 0)))
    nt = s // ts
    nh = s // HALO
    hpt = ts // HALO
    x4 = xs.reshape(bsz, nh, HALO, D)
    t_all = bsz * (s + ts)
    const = lambda b, i: (0, 0)
    lat = lambda b, i: (b, jnp.minimum(i, nt - 1), 0)
    row_blk = lambda b, i: (b * (nt + 1) + i, 0)
    return pl.pallas_call(
        functools.partial(_conv_layer_kernel, ts=ts, n_ctx=n_ctx),
        out_shape=(jax.ShapeDtypeStruct((bsz, s, D), F32),
                   jax.ShapeDtypeStruct((bsz, ts, D), F32),
                   jax.ShapeDtypeStruct((t_all, ROW_W), F32),
                   jax.ShapeDtypeStruct((1, t_all), jnp.int32)),
        grid=(bsz, nt + 1),
        in_specs=[
            pl.BlockSpec((1, ts, D), lat),
            pl.BlockSpec((1, ts, D), lambda b, i: (b, 0, 0)),
            pl.BlockSpec((1, 1, HALO, D), lambda b, i: (b, jnp.clip(i * hpt - 1, 0, nh - 1), 0, 0)),
            pl.BlockSpec((1, 1, HALO, D), lambda b, i: (b, jnp.minimum((i + 1) * hpt, nh - 1), 0, 0)),
            pl.BlockSpec((1, 6, D), lambda b, i: (b, 0, 0)),
            pl.BlockSpec((1, 6, D), lambda b, i: (0, 0, 0)),
            pl.BlockSpec((1, D), const), pl.BlockSpec((1, D), const),
            pl.BlockSpec((D, 2 * D), const), pl.BlockSpec((1, 2 * D), const),
            pl.BlockSpec((CONV_WIDTH, D), const), pl.BlockSpec((1, D), const),
            pl.BlockSpec((1, D), const), pl.BlockSpec((1, D), const),
            pl.BlockSpec((D, D), const), pl.BlockSpec((1, D), const),
            pl.BlockSpec((2, N_EXPERTS, D), lambda b, i: (0, 0, 0)), pl.BlockSpec((N_EXPERTS, 1), const),
        ],
        out_specs=(pl.BlockSpec((1, ts, D), lat),
                   pl.BlockSpec((1, ts, D), lambda b, i: (b, 0, 0)),
                   pl.BlockSpec((ts, ROW_W), row_blk),
                   pl.BlockSpec((1, ts), lambda b, i: (0, b * (nt + 1) + i))),
        scratch_shapes=[pltpu.VMEM((ts + 2 * HALO, D), F32), pltpu.VMEM((ts, D), F32), pltpu.VMEM((ts, D), BF16)],
        compiler_params=_cparams(("arbitrary", "arbitrary")),
        name="conv_layer",
    )(xs, cs, x4, x4, mod_lat, mod_ctx, n1g, n2g, w1, b1, wdw, bdw, lng, lnb, w2, b2, rwt, rb)


def _plan_kernel(cls_ref, rank_ref, cnt_ref, carry_scr, *, tb):
    @pl.when(pl.program_id(0) == 0)
    def _():
        carry_scr[...] = jnp.zeros_like(carry_scr)

    cls = cls_ref[...]
    onehot = (lax.broadcasted_iota(jnp.int32, (CLASS_ROWS, tb), 0) == cls).astype(F32)
    upper = (lax.broadcasted_iota(jnp.int32, (tb, tb), 0) <= lax.broadcasted_iota(jnp.int32, (tb, tb), 1))
    prefix = jnp.dot(onehot.astype(BF16), upper.astype(BF16), preferred_element_type=F32)
    carry = carry_scr[:, 0:1]
    rank = jnp.sum(onehot * (prefix + carry), axis=0, keepdims=True) - 1.0
    rank_ref[...] = rank.astype(jnp.int32)
    total = carry + jnp.sum(onehot, axis=1, keepdims=True)
    carry_scr[...] = jnp.broadcast_to(total, carry_scr.shape)
    cnt_ref[...] = jnp.broadcast_to(total, cnt_ref.shape)


def _plan(cls):
    t_all = cls.shape[1]
    tb = math.gcd(t_all, 1024)
    return pl.pallas_call(
        functools.partial(_plan_kernel, tb=tb),
        out_shape=(jax.ShapeDtypeStruct((1, t_all), jnp.int32), jax.ShapeDtypeStruct((CLASS_ROWS, LANES), F32)),
        grid=(t_all // tb,),
        in_specs=[pl.BlockSpec((1, tb), lambda i: (0, i))],
        out_specs=(pl.BlockSpec((1, tb), lambda i: (0, i)), pl.BlockSpec((CLASS_ROWS, LANES), lambda i: (0, 0))),
        scratch_shapes=[pltpu.VMEM((CLASS_ROWS, LANES), F32)],
        compiler_params=_cparams(("arbitrary",)),
        name="route_plan",
    )(cls)


def _invert_kernel(pos_ref, gap_ref, src_ref, *, n_tok, n_gap):
    def zero(i, carry):
        src_ref[i] = 0
        return carry
    for g in range(n_gap):
        lax.fori_loop(gap_ref[2 * g], gap_ref[2 * g + 1], zero, 0)

    unroll = 16

    def put(i, carry):
        for u in range(unroll):
            t = i * unroll + u
            src_ref[pos_ref[t]] = t
        return carry
    lax.fori_loop(0, n_tok // unroll, put, 0)


def _invert(pos, gaps, n_slot):
    n_tok = pos.shape[0]
    return pl.pallas_call(
        functools.partial(_invert_kernel, n_tok=n_tok, n_gap=gaps.shape[0] // 2),
        out_shape=jax.ShapeDtypeStruct((n_slot,), jnp.int32),
        in_specs=[pl.BlockSpec(memory_space=pltpu.SMEM), pl.BlockSpec(memory_space=pltpu.SMEM)],
        out_specs=pl.BlockSpec(memory_space=pltpu.SMEM),
        name="invert_slots",
    )(pos, gaps)


def _dispatch_tables(cls, rank, counts, *, tm, n_pad=0):
    t_all = cls.shape[1]
    n_tiles = -(-((t_all - n_pad) // tm + N_CLASSES) // MOE_TILES_PER_STEP) * MOE_TILES_PER_STEP
    cnt = counts[:N_CLASSES, 0].astype(jnp.int32)
    padded = ((cnt + tm - 1) // tm) * tm
    ends = jnp.cumsum(padded)
    offs = ends - padded
    first_slot = jnp.zeros((CLASS_ROWS,), jnp.int32).at[:N_CLASSES].set(offs).at[NO_CLASS].set(n_tiles * tm)
    pos = first_slot[cls[0]] + rank[0]
    gap_lo = jnp.concatenate([offs + cnt, ends[-1:]])
    gap_hi = jnp.concatenate([ends, jnp.full((1,), n_tiles * tm, jnp.int32)])
    src = _invert(pos, jnp.stack([gap_lo, gap_hi], axis=1).reshape(-1).astype(jnp.int32), n_tiles * tm + n_pad)
    n_used = ends[-1] // tm
    tile_start = jnp.minimum(jnp.arange(n_tiles, dtype=jnp.int32), n_used - 1) * tm
    tile_cls = jnp.minimum(jnp.sum(ends[None, :] <= tile_start[:, None], axis=1), N_CLASSES - 1).astype(jnp.int32)
    pair_lo = jnp.array([0, 0, 0, 1, 1, 2], jnp.int32)
    pair_hi = jnp.array([1, 2, 3, 2, 3, 3], jnp.int32)
    grp = tile_cls // N_PAIRS
    ea = EXPERTS_PER_GROUP * grp + pair_lo[tile_cls % N_PAIRS]
    eb = EXPERTS_PER_GROUP * grp + pair_hi[tile_cls % N_PAIRS]
    return pos, src, ea, eb, n_used.reshape(1).astype(jnp.int32)


def _start_row_gather(idx_ref, idx_base, src_hbm, dst, sem, n, unroll):
    def one(r):
        row = idx_ref[idx_base + r]
        pltpu.make_async_copy(src_hbm.at[pl.ds(row, 1), :], dst.at[pl.ds(r, 1), :], sem).start()

    if unroll:
        for r in range(n):
            one(r)
    else:
        def body(r, carry):
            one(r)
            return carry
        lax.fori_loop(0, n, body, 0)


def _wait_row_gather(src_hbm, dst, sem, n):
    pltpu.make_async_copy(src_hbm.at[pl.ds(0, n), :], dst, sem).wait()


MOE_TILES_PER_STEP = 2


def _moe_kernel(src_ref, ea_ref, eb_ref, nused_ref, hrow_ref, *rest, tm):
    k = MOE_TILES_PER_STEP
    weights = [rest[6 * i:6 * i + 6] for i in range(k)]
    ys_ref = rest[6 * k]
    bufs = rest[6 * k + 1:6 * k + 1 + k]
    sem = rest[6 * k + 1 + k]
    j = pl.program_id(0)
    nused = nused_ref[0]

    def gather(tile, i, unroll):
        @pl.when(tile < nused)
        def _():
            _start_row_gather(src_ref, tile * tm, hrow_ref, bufs[i], sem.at[i], tm, unroll=unroll)

    @pl.when(j == 0)
    def _():
        for i in range(k):
            gather(i, i, unroll=False)

    for i in range(k):
        tile = j * k + i
        wga, wua, wda, wgb, wub, wdb = weights[i]
        out = ys_ref.at[pl.ds(i * tm, tm), :]

        @pl.when(tile < nused)
        def _():
            _wait_row_gather(hrow_ref, bufs[i], sem.at[i], tm)
            h = bufs[i][:, :D].astype(BF16)
            ga = bufs[i][:, D:D + 1]
            gb = bufs[i][:, D + 1:D + 2]

            def expert(wg, wu, wd):
                a = jnp.dot(h, wg[0], preferred_element_type=F32)
                u = jnp.dot(h, wu[0], preferred_element_type=F32)
                act = (_silu(a) * u).astype(BF16)
                return jnp.dot(act, wd[0], preferred_element_type=F32)

            out[...] = ga * expert(wga, wua, wda) + gb * expert(wgb, wub, wdb)

        @pl.when(tile >= nused)
        def _():
            out[...] = jnp.zeros((tm, D), F32)

        gather(tile + k, i, unroll=True)


def _moe(hrow, src, ea, eb, nused, wg, wu, wd, *, tm, layer):
    k = MOE_TILES_PER_STEP
    n_tiles = ea.shape[0]
    assert n_tiles % k == 0
    f = wg.shape[-1]
    e0 = layer * N_EXPERTS
    w_specs = []
    for i in range(k):
        wa = lambda j, src, ea, eb, nu, i=i: (e0 + ea[j * k + i], 0, 0)
        wb = lambda j, src, ea, eb, nu, i=i: (e0 + eb[j * k + i], 0, 0)
        w_specs += [pl.BlockSpec((1, D, f), wa), pl.BlockSpec((1, D, f), wa), pl.BlockSpec((1, f, D), wa),
                    pl.BlockSpec((1, D, f), wb), pl.BlockSpec((1, D, f), wb), pl.BlockSpec((1, f, D), wb)]
    return pl.pallas_call(
        functools.partial(_moe_kernel, tm=tm),
        out_shape=jax.ShapeDtypeStruct((n_tiles * tm, D), F32),
        grid_spec=pltpu.PrefetchScalarGridSpec(
            num_scalar_prefetch=4,
            grid=(n_tiles // k,),
            in_specs=[pl.BlockSpec(memory_space=pl.ANY)] + w_specs,
            out_specs=pl.BlockSpec((k * tm, D), lambda j, src, ea, eb, nu: (j, 0)),
            scratch_shapes=[pltpu.VMEM((tm, ROW_W), F32)] * k + [pltpu.SemaphoreType.DMA((k,))],
        ),
        compiler_params=_cparams(("arbitrary",), disable_bounds_checks=True),
        name="moe_experts",
    )(src, ea, eb, nused, hrow, *([wg, wu, wd, wg, wu, wd] * k))


def _combine_base(step, *, ts, tok_stride, tok_off):
    nt = pl.num_programs(1)
    return (step // nt) * tok_stride + tok_off + (step % nt) * ts


def _combine_gather(pos_ref, ys_ref, ybuf, sem, *, ts, tok_stride, tok_off):
    nt = pl.num_programs(1)
    n = pl.program_id(0) * nt + pl.program_id(1)

    @pl.when(n == 0)
    def _():
        _start_row_gather(pos_ref, _combine_base(0, ts=ts, tok_stride=tok_stride, tok_off=tok_off), ys_ref,
                          ybuf.at[0], sem.at[0], ts, unroll=False)

    nxt = jnp.minimum(n + 1, pl.num_programs(0) * nt - 1)
    _start_row_gather(pos_ref, _combine_base(nxt, ts=ts, tok_stride=tok_stride, tok_off=tok_off), ys_ref,
                      ybuf.at[1 - n % 2], sem.at[1 - n % 2], ts, unroll=True)
    _wait_row_gather(ys_ref, ybuf.at[n % 2], sem.at[n % 2], ts)
    return ybuf[n % 2]


def _combine_drain(ys_ref, ybuf, sem, ts):
    nt = pl.num_programs(1)
    n = pl.program_id(0) * nt + pl.program_id(1)

    @pl.when(n == pl.num_programs(0) * nt - 1)
    def _():
        _wait_row_gather(ys_ref, ybuf.at[1 - n % 2], sem.at[1 - n % 2], ts)


def _rope(v, cos, sin):
    lane = lax.broadcasted_iota(jnp.int32, v.shape, 1)
    first = (lane % (2 * ROPE_PAIRS)) < ROPE_PAIRS
    partner = jnp.where(first, pltpu.roll(v, LANES - ROPE_PAIRS, 1), pltpu.roll(v, ROPE_PAIRS, 1))
    return v * cos + partner * sin


def _qkv_kernel(pos_ref, x_ref, ys_ref, modp_ref, mod_ref, n1g_ref, w_ref, cq_ref, sq_ref, ck_ref, sk_ref,
                x_out_ref, q_ref, kd_ref, vd_ref, ybuf, sem, *, ts, seq, tok_stride, tok_off):
    y = _combine_gather(pos_ref, ys_ref, ybuf, sem, ts=ts, tok_stride=tok_stride, tok_off=tok_off)
    g2_prev = modp_ref[0][5:6, :]
    x2 = x_ref[0] + g2_prev * y
    x_out_ref[0] = x2
    mod = mod_ref[0]
    sh1 = mod[0:1, :]
    gm1 = n1g_ref[...] * (1.0 + mod[1:2, :])
    hb = _rms_mod(x2, gm1, sh1).astype(BF16)
    cq, sq, ck, sk = cq_ref[...], sq_ref[...], ck_ref[...], sk_ref[...]
    nw = 2 * LANES
    for c0 in range(0, D, nw):
        qc = jnp.dot(hb, w_ref[:, c0:c0 + nw], preferred_element_type=F32)
        for l0 in range(0, nw, LANES):
            q_ref[0, :, c0 + l0:c0 + l0 + LANES] = _rope(qc[:, l0:l0 + LANES], cq, sq).astype(BF16)
    for c0 in range(0, 2 * KV_DIM, nw):
        kc = jnp.dot(hb, w_ref[:, D + c0:D + c0 + nw], preferred_element_type=F32)
        for l0 in range(0, nw, LANES):
            kd_ref[0, :, c0 + l0:c0 + l0 + LANES] = _rope(kc[:, l0:l0 + LANES], ck, sk).astype(BF16)
    vd_ref[0] = jnp.dot(hb, w_ref[:, D + 2 * KV_DIM:], preferred_element_type=F32).astype(BF16)
    _combine_drain(ys_ref, ybuf, sem, ts)


def _qkv(pos, xs, ys, modp, mod, n1g, w_ext, cq, sq, ck, sk, *, ts, tok_stride, tok_off):
    bsz, s, _ = xs.shape
    nt = s // ts
    per_batch = mod.shape[0] > 1
    mod_map = (lambda b, i, p: (b, 0, 0)) if per_batch else (lambda b, i, p: (0, 0, 0))
    const = lambda b, i, p: (0, 0)
    tok = lambda b, i, p: (b, i, 0)
    tab = lambda b, i, p: (i, 0)
    return pl.pallas_call(
        functools.partial(_qkv_kernel, ts=ts, seq=s, tok_stride=tok_stride, tok_off=tok_off),
        out_shape=(jax.ShapeDtypeStruct((bsz, s, D), F32), jax.ShapeDtypeStruct((bsz, s, D), BF16),
                   jax.ShapeDtypeStruct((bsz, s, 2 * KV_DIM), BF16), jax.ShapeDtypeStruct((bsz, s, 2 * KV_DIM), BF16)),
        grid_spec=pltpu.PrefetchScalarGridSpec(
            num_scalar_prefetch=1,
            grid=(bsz, nt),
            in_specs=[
                pl.BlockSpec((1, ts, D), tok),
                pl.BlockSpec(memory_space=pl.ANY),
                pl.BlockSpec((1, 6, D), mod_map), pl.BlockSpec((1, 6, D), mod_map),
                pl.BlockSpec((1, D), const),
                pl.BlockSpec((D, D + 4 * KV_DIM), const),
                pl.BlockSpec((ts, LANES), tab), pl.BlockSpec((ts, LANES), tab),
                pl.BlockSpec((ts, LANES), tab), pl.BlockSpec((ts, LANES), tab),
            ],
            out_specs=(pl.BlockSpec((1, ts, D), tok), pl.BlockSpec((1, ts, D), tok),
                       pl.BlockSpec((1, ts, 2 * KV_DIM), tok), pl.BlockSpec((1, ts, 2 * KV_DIM), tok)),
            scratch_shapes=[pltpu.VMEM((2, ts, D), F32), pltpu.SemaphoreType.DMA((2,))],
        ),
        compiler_params=_cparams(("arbitrary", "arbitrary"), disable_bounds_checks=True),
        name="combine_qkv",
    )(pos, xs, ys, modp, mod, n1g, w_ext, cq, sq, ck, sk)


ATTN_QBLOCKS = 2


def _attn_kernel(q_ref, *rest, lc):
    nq = ATTN_QBLOCKS
    k_refs = rest[:nq + 2]
    v_refs = rest[nq + 2:2 * nq + 4]
    kx_ref, vx_ref, sink_ref, o_ref = rest[2 * nq + 4:]
    i = pl.program_id(1)
    nsteps = pl.num_programs(1)
    nband = 3 * QBLK
    qrow = lax.broadcasted_iota(jnp.int32, (GROUP * QBLK, nband), 0) % QBLK
    kcol = lax.broadcasted_iota(jnp.int32, (GROUP * QBLK, nband), 1)
    lane = lax.broadcasted_iota(jnp.int32, (1, LANES), 1)
    low = lane < HEAD_DIM
    units = [(qb, h) for qb in range(nq) for h in range(N_KV_HEADS)]

    def band_ok(qb):
        lo_lim = jnp.where(i > 0, 0, QBLK) if qb == 0 else 0
        hi_lim = jnp.where(i < nsteps - 1, nband, 2 * QBLK - 1) if qb == nq - 1 else nband
        return (kcol >= jnp.maximum(qrow, lo_lim)) & (kcol <= jnp.minimum(qrow + 2 * WINDOW, hi_lim))

    ok = [band_ok(qb) for qb in range(nq)]

    def scores(unit):
        qb, h = unit
        hs = slice(h * LANES, (h + 1) * LANES)
        k_all = jnp.concatenate([r[0, :, hs] for r in k_refs[qb:qb + 3]] + [kx_ref[0, :, hs]], axis=0)
        qs = []
        for j in range(GROUP):
            c0 = h * GROUP * HEAD_DIM + (j // 2) * LANES
            qh = q_ref[0, qb * QBLK:(qb + 1) * QBLK, c0:c0 + LANES]
            keep = low if j % 2 == 0 else jnp.logical_not(low)
            qs.append(jnp.where(keep, qh, jnp.zeros_like(qh)))
        qm = jnp.concatenate(qs, axis=0)
        return lax.dot_general(qm, k_all, (((1,), (1,)), ((), ())), preferred_element_type=F32)

    def softmax(unit, s):
        qb, h = unit
        s = jnp.concatenate([jnp.where(ok[qb], s[:, :nband], NEG_INF), s[:, nband:]], axis=1)
        sink = LOG2E * jnp.concatenate(
            [jnp.broadcast_to(sink_ref[:, h * GROUP + j:h * GROUP + j + 1], (QBLK, 1)) for j in range(GROUP)], axis=0)
        nblk = s.shape[1] // LANES
        smax = s[:, :LANES]
        for kb in range(1, nblk):
            smax = jnp.maximum(smax, s[:, kb * LANES:(kb + 1) * LANES])
        m = jnp.maximum(jnp.max(smax, axis=1, keepdims=True), sink)
        return jnp.exp2(s - m).astype(BF16), jnp.exp2(sink - m)

    def weighted_values(unit, p, p_sink):
        qb, h = unit
        hs = slice(h * LANES, (h + 1) * LANES)
        v_all = jnp.concatenate([r[0, :, hs] for r in v_refs[qb:qb + 3]] + [vx_ref[0, :, hs]], axis=0)
        ones_lo = jnp.broadcast_to(jnp.where(lane == HEAD_DIM, 1.0, 0.0).astype(v_all.dtype), v_all.shape)
        ones_hi = jnp.broadcast_to(jnp.where(lane == 0, 1.0, 0.0).astype(v_all.dtype), v_all.shape)
        v_lo = jnp.where(low, v_all, ones_lo)
        v_hi = jnp.where(low, ones_hi, v_all)
        for pr in range(GROUP // 2):
            r0 = 2 * pr * QBLK
            o_lo = jnp.dot(p[r0:r0 + QBLK], v_lo, preferred_element_type=F32)
            o_hi = jnp.dot(p[r0 + QBLK:r0 + 2 * QBLK], v_hi, preferred_element_type=F32)
            inv_lo = 1.0 / (o_lo[:, HEAD_DIM:HEAD_DIM + 1] + p_sink[r0:r0 + QBLK])
            inv_hi = 1.0 / (o_hi[:, 0:1] + p_sink[r0 + QBLK:r0 + 2 * QBLK])
            c0 = h * GROUP * HEAD_DIM + pr * LANES
            o_ref[0, qb * QBLK:(qb + 1) * QBLK, c0:c0 + LANES] = (
                jnp.where(low, o_lo * inv_lo, o_hi * inv_hi).astype(BF16))

    n = len(units)
    s_all = [scores(units[0]), scores(units[1])]
    p_cur = softmax(units[0], s_all[0])
    for u in range(n):
        if u + 2 < n:
            s_all.append(scores(units[u + 2]))
        p_next = softmax(units[u + 1], s_all[u + 1]) if u + 1 < n else None
        weighted_values(units[u], *p_cur)
        p_cur = p_next


def _attention(q, kd, vd, kxd, vxd, sink):
    bsz, s, _ = q.shape
    lc = kxd.shape[1]
    nq = ATTN_QBLOCKS
    nb = s // QBLK
    assert nb % nq == 0
    tok = lambda b, i: (b, i, 0)
    ctx = lambda b, i: (b, 0, 0)
    kvw = 2 * KV_DIM
    band = [pl.BlockSpec((1, QBLK, kvw), lambda b, i, d=d: (b, jnp.clip(nq * i - 1 + d, 0, nb - 1), 0))
            for d in range(nq + 2)]
    return pl.pallas_call(
        functools.partial(_attn_kernel, lc=lc),
        out_shape=jax.ShapeDtypeStruct((bsz, s, D), BF16),
        grid=(bsz, nb // nq),
        in_specs=[pl.BlockSpec((1, nq * QBLK, D), tok)] + band + band + [
            pl.BlockSpec((1, lc, kvw), ctx), pl.BlockSpec((1, lc, kvw), ctx),
            pl.BlockSpec((1, N_HEADS), lambda b, i: (0, 0)),
        ],
        out_specs=pl.BlockSpec((1, nq * QBLK, D), tok),
        compiler_params=_cparams(("arbitrary", "arbitrary")),
        name="window_attention",
    )(q, *([kd] * (nq + 2)), *([vd] * (nq + 2)), kxd, vxd, sink)


def _attn_out_kernel(a_ref, x_ref, mod_ref, n2g_ref, wo_ref, rwt_ref, rb_ref, x_out_ref, hrow_ref, cls_ref):
    mod = mod_ref[0]
    g1, sh2, sc2 = mod[2:3, :], mod[3:4, :], mod[4:5, :]
    gm2 = n2g_ref[...] * (1.0 + sc2)
    y = jnp.dot(a_ref[0], wo_ref[...], preferred_element_type=F32)
    x_out_ref[0] = _epilogue(x_ref[0], y, g1, sh2, gm2, rwt_ref, rb_ref, hrow_ref, cls_ref)


def _attn_out(attn, xs, mod, n2g, wo, rwt, rb, *, ts):
    bsz, s, _ = xs.shape
    nt = s // ts
    t_all = bsz * s
    const = lambda b, i: (0, 0)
    tok = lambda b, i: (b, i, 0)
    return pl.pallas_call(
        _attn_out_kernel,
        out_shape=(jax.ShapeDtypeStruct((bsz, s, D), F32),
                   jax.ShapeDtypeStruct((t_all, ROW_W), F32),
                   jax.ShapeDtypeStruct((1, t_all), jnp.int32)),
        grid=(bsz, nt),
        in_specs=[
            pl.BlockSpec((1, ts, D), tok), pl.BlockSpec((1, ts, D), tok),
            pl.BlockSpec((1, 6, D), lambda b, i: (b, 0, 0)),
            pl.BlockSpec((1, D), const), pl.BlockSpec((D, D), const),
            pl.BlockSpec((2, N_EXPERTS, D), lambda b, i: (0, 0, 0)), pl.BlockSpec((N_EXPERTS, 1), const),
        ],
        out_specs=(pl.BlockSpec((1, ts, D), tok),
                   pl.BlockSpec((ts, ROW_W), lambda b, i: (b * nt + i, 0)),
                   pl.BlockSpec((1, ts), lambda b, i: (0, b * nt + i))),
        compiler_params=_cparams(("arbitrary", "arbitrary")),
        name="attn_out",
    )(attn, xs, mod, n2g, wo, rwt, rb)


def _final_kernel(pos_ref, x_ref, ys_ref, mod_ref, fg_ref, o_ref, ybuf, sem, *, ts, seq):
    y = _combine_gather(pos_ref, ys_ref, ybuf, sem, ts=ts, tok_stride=seq, tok_off=0)
    x4 = x_ref[0] + mod_ref[0][5:6, :] * y
    r = lax.rsqrt(jnp.mean(x4 * x4, axis=-1, keepdims=True) + EPS)
    o_ref[0] = x4 * r * fg_ref[...]
    _combine_drain(ys_ref, ybuf, sem, ts)


def _final(pos, xs, ys, mod, fg, *, ts):
    bsz, s, _ = xs.shape
    tok = lambda b, i, p: (b, i, 0)
    return pl.pallas_call(
        functools.partial(_final_kernel, ts=ts, seq=s),
        out_shape=jax.ShapeDtypeStruct((bsz, s, D), F32),
        grid_spec=pltpu.PrefetchScalarGridSpec(
            num_scalar_prefetch=1,
            grid=(bsz, s // ts),
            in_specs=[pl.BlockSpec((1, ts, D), tok), pl.BlockSpec(memory_space=pl.ANY),
                      pl.BlockSpec((1, 6, D), lambda b, i, p: (b, 0, 0)), pl.BlockSpec((1, D), lambda b, i, p: (0, 0))],
            out_specs=pl.BlockSpec((1, ts, D), tok),
            scratch_shapes=[pltpu.VMEM((2, ts, D), F32), pltpu.SemaphoreType.DMA((2,))],
        ),
        compiler_params=_cparams(("arbitrary", "arbitrary"), disable_bounds_checks=True),
        name="combine_final",
    )(pos, xs, ys, mod, fg)


def _rope_tables(seq):
    rows = seq // GRID_W
    inv_freq = jnp.power(jnp.float32(ROPE_THETA), -jnp.arange(ROPE_PAIRS, dtype=F32) / ROPE_PAIRS)
    ang_r = jnp.arange(rows, dtype=jnp.int32).astype(F32)[:, None] * inv_freq
    ang_c = jnp.arange(GRID_W, dtype=jnp.int32).astype(F32)[:, None] * inv_freq
    by_row = lambda v: jnp.repeat(v, GRID_W, axis=0)
    by_col = lambda v: jnp.tile(v, (rows, 1))
    cos_r, sin_r = by_row(jnp.cos(ang_r)), by_row(jnp.sin(ang_r))
    cos_c, sin_c = by_col(jnp.cos(ang_c)), by_col(jnp.sin(ang_c))
    cos = jnp.concatenate([cos_r, cos_r, cos_c, cos_c], axis=1)
    sin = jnp.concatenate([-sin_r, sin_r, -sin_c, sin_c], axis=1)
    cos = jnp.concatenate([cos, cos], axis=1)
    sin = jnp.concatenate([sin, sin], axis=1)
    scale = HEAD_DIM ** -0.5 * LOG2E
    return cos * scale, sin * scale, cos, sin


def kernel(x, c, ctx, c_ctx, w_mod, b_mod, norm1_g, norm2_g, conv_w_pw1, conv_b_pw1, conv_w_dw, conv_b_dw,
           conv_ln_g, conv_ln_b, conv_w_pw2, conv_b_pw2, attn_w_qkv, attn_w_o, attn_sink, router_w, router_b,
           moe_w_gate, moe_w_up, moe_w_down, final_g):
    bsz, s, d = x.shape
    lc = ctx.shape[1]
    ts = 512
    tm = 256
    assert d == D and w_mod.shape[0] == 2 and bsz + 1 <= COND_ROWS
    assert lc <= ts and lc % 256 == 0 and s % ts == 0 and s % (ATTN_QBLOCKS * QBLK) == 0

    cond = jnp.concatenate([c, c_ctx[None, :], jnp.zeros((COND_ROWS - bsz - 1, D), F32)], axis=0)
    mods = _adaln(cond, w_mod, b_mod).reshape(2, COND_ROWS, 6, D)
    mod_lat = [mods[l, :bsz] for l in range(2)]
    mod_ctx = [mods[l, bsz:bsz + 1] for l in range(2)]

    rwt_hi = router_w.T.astype(BF16)
    rwt = jnp.stack([rwt_hi, (router_w.T - rwt_hi.astype(F32)).astype(BF16)])
    rb = router_b.reshape(N_EXPERTS, 1)
    row = lambda v: v.reshape(1, -1)
    wg = moe_w_gate.astype(BF16).reshape(2 * N_EXPERTS, D, D_EXPERT)
    wu = moe_w_up.astype(BF16).reshape(2 * N_EXPERTS, D, D_EXPERT)
    wd = moe_w_down.astype(BF16).reshape(2 * N_EXPERTS, D_EXPERT, D)

    x1, ctx1, hrow, cls = _conv_layer(
        x, ctx, mod_lat[0], mod_ctx[0], row(norm1_g[0]), row(norm2_g[0]), conv_w_pw1[0].astype(BF16),
        row(conv_b_pw1[0]), conv_w_dw[0], row(conv_b_dw[0]), row(conv_ln_g[0]), row(conv_ln_b[0]),
        conv_w_pw2[0].astype(BF16), row(conv_b_pw2[0]), rwt, rb, ts=ts)
    ctx1 = ctx1[:, :lc]
    rank, counts = _plan(cls)
    pos, src, ea, eb, nused = _dispatch_tables(cls, rank, counts, tm=tm, n_pad=bsz * (ts - lc))
    ys = _moe(hrow, src, ea, eb, nused, wg, wu, wd, tm=tm, layer=0)

    wq = attn_w_qkv[0][:, :D]
    wk = attn_w_qkv[0][:, D:D + KV_DIM].reshape(D, N_KV_HEADS, 1, HEAD_DIM)
    wv = attn_w_qkv[0][:, D + KV_DIM:].reshape(D, N_KV_HEADS, 1, HEAD_DIM)
    dup = lambda w: jnp.broadcast_to(w, (D, N_KV_HEADS, 2, HEAD_DIM)).reshape(D, 2 * KV_DIM)
    w_ext = jnp.concatenate([wq, dup(wk), dup(wv)], axis=1).astype(BF16)
    cq, sq, ck, sk = _rope_tables(s)
    ones = jnp.ones((lc, LANES), F32)
    zeros = jnp.zeros((lc, LANES), F32)
    n1g = row(norm1_g[1])
    x2, q, kd, vd = _qkv(pos, x1, ys, mod_lat[0], mod_lat[1], n1g, w_ext, cq, sq, ck, sk,
                         ts=ts, tok_stride=s + ts, tok_off=0)
    _, _, kxd, vxd = _qkv(pos, ctx1, ys, mod_ctx[0], mod_ctx[1], n1g, w_ext, ones, zeros, ones, zeros,
                          ts=lc, tok_stride=s + ts, tok_off=s)
    attn = _attention(q, kd, vd, kxd, vxd, attn_sink[0].reshape(1, N_HEADS))
    x3, hrow1, cls1 = _attn_out(attn, x2, mod_lat[1], row(norm2_g[1]), attn_w_o[0].astype(BF16), rwt, rb, ts=ts)
    rank1, counts1 = _plan(cls1)
    pos1, src1, ea1, eb1, nused1 = _dispatch_tables(cls1, rank1, counts1, tm=tm)
    ys1 = _moe(hrow1, src1, ea1, eb1, nused1, wg, wu, wd, tm=tm, layer=1)
    return _final(pos1, x3, ys1, mod_lat[1], row(final_g), ts=ts)
```

```python
import functools
import math

import jax
import jax.numpy as jnp
from jax import lax
from jax.experimental import pallas as pl
from jax.experimental.pallas import tpu as pltpu

D = 1024
GRID_W = 64
CONV_WIDTH = 31
CONV_PAD = CONV_WIDTH // 2
SUBLANES = 8
HALO = 16
HEAD_DIM = 64
N_HEADS = D // HEAD_DIM
N_KV_HEADS = N_HEADS // 4
GROUP = N_HEADS // N_KV_HEADS
KV_DIM = N_KV_HEADS * HEAD_DIM
WINDOW = 128
QBLK = 128
ROPE_THETA = 10000.0
ROPE_PAIRS = HEAD_DIM // 4
N_EXPERTS = 16
N_GROUPS = 4
EXPERTS_PER_GROUP = N_EXPERTS // N_GROUPS
N_PAIRS = 6
N_CLASSES = N_GROUPS * N_PAIRS
CLASS_ROWS = 32
NO_CLASS = CLASS_ROWS - 1
D_EXPERT = D // 2
EPS = 1e-6
NEG_INF = -1e30
LOG2E = 1.4426950408889634
LANES = 128
AUX = LANES
ROW_W = D + AUX
COND_ROWS = 16
VMEM_LIMIT = 56 * 1024 * 1024

F32 = jnp.float32
BF16 = jnp.bfloat16
HIGHEST = lax.Precision.HIGHEST


def _cparams(sem, **kw):
    return pltpu.CompilerParams(dimension_semantics=sem, vmem_limit_bytes=VMEM_LIMIT, **kw)


def _sigmoid(v):
    return jax.nn.sigmoid(v)


def _silu(v):
    return v * jax.nn.sigmoid(v)


def _rms_mod(xr, gm, sh):
    r = lax.rsqrt(jnp.mean(xr * xr, axis=-1, keepdims=True) + EPS)
    return xr * r * gm + sh


def _adaln_kernel(cond_ref, w_ref, b_ref, out_ref):
    cnd = cond_ref[...]
    act = _silu(cnd)
    out_ref[0] = jnp.dot(act, w_ref[0], precision=HIGHEST, preferred_element_type=F32) + b_ref[0]


def _adaln(cond, w_mod, b_mod):
    depth = w_mod.shape[0]
    tn = 1536
    return pl.pallas_call(
        _adaln_kernel,
        out_shape=jax.ShapeDtypeStruct((depth, COND_ROWS, 6 * D), F32),
        grid=(depth, 6 * D // tn),
        in_specs=[
            pl.BlockSpec((COND_ROWS, D), lambda l, n: (0, 0)),
            pl.BlockSpec((1, D, tn), lambda l, n: (l, 0, n)),
            pl.BlockSpec((1, 1, tn), lambda l, n: (l, 0, n)),
        ],
        out_specs=pl.BlockSpec((1, COND_ROWS, tn), lambda l, n: (l, 0, n)),
        compiler_params=_cparams(("arbitrary", "arbitrary")),
        name="adaln",
    )(cond, w_mod, b_mod.reshape(depth, 1, 6 * D))


def _route_rows(h2, rwt_ref, rb_ref):
    nt_dims = (((1,), (1,)), ((), ()))
    h_hi = h2.astype(BF16)
    h_lo = (h2 - h_hi.astype(F32)).astype(BF16)
    w_hi = rwt_ref[0]
    w_lo = rwt_ref[1]
    lt = (lax.dot_general(w_hi, h_hi, nt_dims, preferred_element_type=F32)
          + lax.dot_general(w_hi, h_lo, nt_dims, preferred_element_type=F32)
          + lax.dot_general(w_lo, h_hi, nt_dims, preferred_element_type=F32))
    s = _sigmoid(lt)
    sel = s + rb_ref[...]
    srow = [s[e:e + 1, :] for e in range(N_EXPERTS)]
    vrow = [sel[e:e + 1, :] for e in range(N_EXPERTS)]
    gscore = []
    for g in range(N_GROUPS):
        a, b, c, d = vrow[4 * g:4 * g + 4]
        gscore.append(jnp.maximum(jnp.maximum(jnp.maximum(a + b, a + c), jnp.maximum(a + d, b + c)),
                                  jnp.maximum(b + d, c + d)))
    bg = jnp.zeros_like(gscore[0], dtype=jnp.int32)
    best = gscore[0]
    for g in range(1, N_GROUPS):
        better = gscore[g] > best
        bg = jnp.where(better, g, bg)
        best = jnp.where(better, gscore[g], best)
    v = []
    sv = []
    for j in range(EXPERTS_PER_GROUP):
        vj = vrow[j]
        sj = srow[j]
        for g in range(1, N_GROUPS):
            vj = jnp.where(bg == g, vrow[4 * g + j], vj)
            sj = jnp.where(bg == g, srow[4 * g + j], sj)
        v.append(vj)
        sv.append(sj)
    i1 = jnp.zeros_like(bg)
    m1 = v[0]
    for j in range(1, EXPERTS_PER_GROUP):
        gt = v[j] > m1
        i1 = jnp.where(gt, j, i1)
        m1 = jnp.where(gt, v[j], m1)
    v2 = [jnp.where(i1 == j, -jnp.inf, v[j]) for j in range(EXPERTS_PER_GROUP)]
    i2 = jnp.zeros_like(bg)
    m2 = v2[0]
    for j in range(1, EXPERTS_PER_GROUP):
        gt = v2[j] > m2
        i2 = jnp.where(gt, j, i2)
        m2 = jnp.where(gt, v2[j], m2)
    lo = jnp.minimum(i1, i2)
    hi = jnp.maximum(i1, i2)
    s_lo = sv[0]
    s_hi = sv[0]
    for j in range(1, EXPERTS_PER_GROUP):
        s_lo = jnp.where(lo == j, sv[j], s_lo)
        s_hi = jnp.where(hi == j, sv[j], s_hi)
    den = s_lo + s_hi
    pair = jnp.where(lo == 0, 0, jnp.where(lo == 1, 3, 5)) + (hi - lo - 1)
    cls = N_PAIRS * bg + pair
    return s_lo / den, s_hi / den, cls


def _epilogue(x_in, y, g1, sh2, gm2, rwt_ref, rb_ref, hrow_ref, cls_ref, n_valid=None):
    x1 = x_in + g1 * y
    h2 = _rms_mod(x1, gm2, sh2)
    ga, gb, cls = _route_rows(h2, rwt_ref, rb_ref)
    ts = h2.shape[0]
    if n_valid is not None:
        cls = jnp.where(lax.broadcasted_iota(jnp.int32, cls.shape, 1) < n_valid, cls, NO_CLASS)
    cls_ref[...] = cls
    g8 = jnp.concatenate([ga, gb, jnp.zeros((6, ts), F32)], axis=0)
    g128 = jnp.concatenate([g8, jnp.zeros((AUX - 8, ts), F32)], axis=0)
    hrow_ref[:, :D] = h2
    hrow_ref[:, D:] = jnp.transpose(g128)
    return x1


def _conv_layer_kernel(x_ref, c_ref, xp_ref, xn_ref, modl_ref, modc_ref, n1g_ref, n2g_ref, w1_ref, b1_ref, wdw_ref,
                       bdw_ref, lng_ref, lnb_ref, w2_ref, b2_ref, rwt_ref, rb_ref,
                       x_out_ref, c_out_ref, hrow_ref, cls_ref, u_scr, cv_scr, v_scr, *, ts, n_ctx):
    i = pl.program_id(1)
    nt = pl.num_programs(1) - 1
    is_ctx = jnp.full((1, 1), i, jnp.int32) == nt
    n_valid = jnp.where(i == nt, n_ctx, ts)
    mod = jnp.where(is_ctx, modc_ref[0], modl_ref[0])
    sh1, sc1, g1, sh2, sc2 = (mod[k:k + 1, :] for k in range(5))
    gm1 = n1g_ref[...] * (1.0 + sc1)
    gm2 = n2g_ref[...] * (1.0 + sc2)
    cw = 512

    def glu(hb, c0):
        a = jnp.dot(hb, w1_ref[:, c0:c0 + cw], preferred_element_type=F32) + b1_ref[:, c0:c0 + cw]
        g = jnp.dot(hb, w1_ref[:, D + c0:D + c0 + cw], preferred_element_type=F32) + b1_ref[:, D + c0:D + c0 + cw]
        return a * _sigmoid(g)

    xm = jnp.where(is_ctx, c_ref[0], x_ref[0])
    hb = jnp.concatenate([_rms_mod(xp_ref[0, 0], gm1, sh1).astype(BF16), _rms_mod(xm, gm1, sh1).astype(BF16),
                          _rms_mod(xn_ref[0, 0], gm1, sh1).astype(BF16)], axis=0)
    keep_prev = jnp.logical_and(i > 0, i < nt).astype(F32)
    keep_next = (i < nt - 1).astype(F32)
    keep_row = (lax.broadcasted_iota(jnp.int32, (ts, 1), 0) < n_valid).astype(F32)
    for c0 in range(0, D, cw):
        u = glu(hb, c0)
        u_scr[0:HALO, c0:c0 + cw] = u[:HALO] * keep_prev
        u_scr[HALO:HALO + ts, c0:c0 + cw] = u[HALO:HALO + ts] * keep_row
        u_scr[HALO + ts:2 * HALO + ts, c0:c0 + cw] = u[HALO + ts:] * keep_next

    rr = 128
    taps_off = HALO - CONV_PAD

    def conv_chunk(rc, carry):
        r0 = pl.multiple_of(rc * rr, rr)
        for c0 in range(0, D, LANES):
            win = u_scr[pl.ds(r0, rr + 2 * HALO), c0:c0 + LANES]
            out = None
            for r in range(SUBLANES):
                nrow = rr + SUBLANES if r else rr
                part = None
                for q in range((CONV_WIDTH + taps_off) // SUBLANES + 1):
                    k = SUBLANES * q + r - taps_off
                    if 0 <= k < CONV_WIDTH:
                        term = win[SUBLANES * q:SUBLANES * q + nrow, :] * wdw_ref[k:k + 1, c0:c0 + LANES]
                        part = term if part is None else part + term
                shifted = part[r:r + rr, :]
                out = shifted if out is None else out + shifted
            cv_scr[pl.ds(r0, rr), c0:c0 + LANES] = out
        cv = cv_scr[pl.ds(r0, rr), :] + bdw_ref[...]
        mu = jnp.mean(cv, axis=-1, keepdims=True)
        dv = cv - mu
        var = jnp.mean(dv * dv, axis=-1, keepdims=True)
        ln = dv * lax.rsqrt(var + EPS) * lng_ref[...] + lnb_ref[...]
        v_scr[pl.ds(r0, rr), :] = _silu(ln).astype(BF16)
        return carry

    lax.fori_loop(0, ts // rr, conv_chunk, 0)
    y = jnp.dot(v_scr[...], w2_ref[...], preferred_element_type=F32) + b2_ref[...]
    x1 = _epilogue(xm, y, g1, sh2, gm2, rwt_ref, rb_ref, hrow_ref, cls_ref, n_valid=n_valid)

    @pl.when(i < nt)
    def _():
        x_out_ref[0] = x1

    @pl.when(i == nt)
    def _():
        c_out_ref[0] = x1


def _conv_layer(xs, cs, mod_lat, mod_ctx, n1g, n2g, w1, b1, wdw, bdw, lng, lnb, w2, b2, rwt, rb, *, ts):
    bsz, s, _ = xs.shape
    n_ctx = cs.shape[1]
    assert n_ctx <= ts
    cs = jnp.pad(cs, ((0, 0), (0, ts - n_ctx), (0, 0)))
    nt = s // ts
    nh = s // HALO
    hpt = ts // HALO
    x4 = xs.reshape(bsz, nh, HALO, D)
    t_all = bsz * (s + ts)
    const = lambda b, i: (0, 0)
    lat = lambda b, i: (b, jnp.minimum(i, nt - 1), 0)
    row_blk = lambda b, i: (b * (nt + 1) + i, 0)
    return pl.pallas_call(
        functools.partial(_conv_layer_kernel, ts=ts, n_ctx=n_ctx),
        out_shape=(jax.ShapeDtypeStruct((bsz, s, D), F32),
                   jax.ShapeDtypeStruct((bsz, ts, D), F32),
                   jax.ShapeDtypeStruct((t_all, ROW_W), F32),
                   jax.ShapeDtypeStruct((1, t_all), jnp.int32)),
        grid=(bsz, nt + 1),
        in_specs=[
            pl.BlockSpec((1, ts, D), lat),
            pl.BlockSpec((1, ts, D), lambda b, i: (b, 0, 0)),
            pl.BlockSpec((1, 1, HALO, D), lambda b, i: (b, jnp.clip(i * hpt - 1, 0, nh - 1), 0, 0)),
            pl.BlockSpec((1, 1, HALO, D), lambda b, i: (b, jnp.minimum((i + 1) * hpt, nh - 1), 0, 0)),
            pl.BlockSpec((1, 6, D), lambda b, i: (b, 0, 0)),
            pl.BlockSpec((1, 6, D), lambda b, i: (0, 0, 0)),
            pl.BlockSpec((1, D), const), pl.BlockSpec((1, D), const),
            pl.BlockSpec((D, 2 * D), const), pl.BlockSpec((1, 2 * D), const),
            pl.BlockSpec((CONV_WIDTH, D), const), pl.BlockSpec((1, D), const),
            pl.BlockSpec((1, D), const), pl.BlockSpec((1, D), const),
            pl.BlockSpec((D, D), const), pl.BlockSpec((1, D), const),
            pl.BlockSpec((2, N_EXPERTS, D), lambda b, i: (0, 0, 0)), pl.BlockSpec((N_EXPERTS, 1), const),
        ],
        out_specs=(pl.BlockSpec((1, ts, D), lat),
                   pl.BlockSpec((1, ts, D), lambda b, i: (b, 0, 0)),
                   pl.BlockSpec((ts, ROW_W), row_blk),
                   pl.BlockSpec((1, ts), lambda b, i: (0, b * (nt + 1) + i))),
        scratch_shapes=[pltpu.VMEM((ts + 2 * HALO, D), F32), pltpu.VMEM((ts, D), F32), pltpu.VMEM((ts, D), BF16)],
        compiler_params=_cparams(("arbitrary", "arbitrary")),
        name="conv_layer",
    )(xs, cs, x4, x4, mod_lat, mod_ctx, n1g, n2g, w1, b1, wdw, bdw, lng, lnb, w2, b2, rwt, rb)


def _plan_kernel(cls_ref, rank_ref, cnt_ref, carry_scr, *, tb):
    @pl.when(pl.program_id(0) == 0)
    def _():
        carry_scr[...] = jnp.zeros_like(carry_scr)

    cls = cls_ref[...]
    onehot = (lax.broadcasted_iota(jnp.int32, (CLASS_ROWS, tb), 0) == cls).astype(F32)
    upper = (lax.broadcasted_iota(jnp.int32, (tb, tb), 0) <= lax.broadcasted_iota(jnp.int32, (tb, tb), 1))
    prefix = jnp.dot(onehot.astype(BF16), upper.astype(BF16), preferred_element_type=F32)
    carry = carry_scr[:, 0:1]
    rank = jnp.sum(onehot * (prefix + carry), axis=0, keepdims=True) - 1.0
    rank_ref[...] = rank.astype(jnp.int32)
    total = carry + jnp.sum(onehot, axis=1, keepdims=True)
    carry_scr[...] = jnp.broadcast_to(total, carry_scr.shape)
    cnt_ref[...] = jnp.broadcast_to(total, cnt_ref.shape)


def _plan(cls):
    t_all = cls.shape[1]
    tb = math.gcd(t_all, 1024)
    return pl.pallas_call(
        functools.partial(_plan_kernel, tb=tb),
        out_shape=(jax.ShapeDtypeStruct((1, t_all), jnp.int32), jax.ShapeDtypeStruct((CLASS_ROWS, LANES), F32)),
        grid=(t_all // tb,),
        in_specs=[pl.BlockSpec((1, tb), lambda i: (0, i))],
        out_specs=(pl.BlockSpec((1, tb), lambda i: (0, i)), pl.BlockSpec((CLASS_ROWS, LANES), lambda i: (0, 0))),
        scratch_shapes=[pltpu.VMEM((CLASS_ROWS, LANES), F32)],
        compiler_params=_cparams(("arbitrary",)),
        name="route_plan",
    )(cls)


def _invert_kernel(pos_ref, gap_ref, src_ref, *, n_tok, n_gap):
    def zero(i, carry):
        src_ref[i] = 0
        return carry
    for g in range(n_gap):
        lax.fori_loop(gap_ref[2 * g], gap_ref[2 * g + 1], zero, 0)

    unroll = 16

    def put(i, carry):
        for u in range(unroll):
            t = i * unroll + u
            src_ref[pos_ref[t]] = t
        return carry
    lax.fori_loop(0, n_tok // unroll, put, 0)


def _invert(pos, gaps, n_slot):
    n_tok = pos.shape[0]
    return pl.pallas_call(
        functools.partial(_invert_kernel, n_tok=n_tok, n_gap=gaps.shape[0] // 2),
        out_shape=jax.ShapeDtypeStruct((n_slot,), jnp.int32),
        in_specs=[pl.BlockSpec(memory_space=pltpu.SMEM), pl.BlockSpec(memory_space=pltpu.SMEM)],
        out_specs=pl.BlockSpec(memory_space=pltpu.SMEM),
        name="invert_slots",
    )(pos, gaps)


def _dispatch_tables(cls, rank, counts, *, tm, n_pad=0):
    t_all = cls.shape[1]
    n_tiles = -(-((t_all - n_pad) // tm + N_CLASSES) // MOE_TILES_PER_STEP) * MOE_TILES_PER_STEP
    cnt = counts[:N_CLASSES, 0].astype(jnp.int32)
    padded = ((cnt + tm - 1) // tm) * tm
    ends = jnp.cumsum(padded)
    offs = ends - padded
    first_slot = jnp.zeros((CLASS_ROWS,), jnp.int32).at[:N_CLASSES].set(offs).at[NO_CLASS].set(n_tiles * tm)
    pos = first_slot[cls[0]] + rank[0]
    gap_lo = jnp.concatenate([offs + cnt, ends[-1:]])
    gap_hi = jnp.concatenate([ends, jnp.full((1,), n_tiles * tm, jnp.int32)])
    src = _invert(pos, jnp.stack([gap_lo, gap_hi], axis=1).reshape(-1).astype(jnp.int32), n_tiles * tm + n_pad)
    n_used = ends[-1] // tm
    tile_start = jnp.minimum(jnp.arange(n_tiles, dtype=jnp.int32), n_used - 1) * tm
    tile_cls = jnp.minimum(jnp.sum(ends[None, :] <= tile_start[:, None], axis=1), N_CLASSES - 1).astype(jnp.int32)
    pair_lo = jnp.array([0, 0, 0, 1, 1, 2], jnp.int32)
    pair_hi = jnp.array([1, 2, 3, 2, 3, 3], jnp.int32)
    grp = tile_cls // N_PAIRS
    ea = EXPERTS_PER_GROUP * grp + pair_lo[tile_cls % N_PAIRS]
    eb = EXPERTS_PER_GROUP * grp + pair_hi[tile_cls % N_PAIRS]
    return pos, src, ea, eb, n_used.reshape(1).astype(jnp.int32)


def _start_row_gather(idx_ref, idx_base, src_hbm, dst, sem, n, unroll):
    def one(r):
        row = idx_ref[idx_base + r]
        pltpu.make_async_copy(src_hbm.at[pl.ds(row, 1), :], dst.at[pl.ds(r, 1), :], sem).start()

    if unroll:
        for r in range(n):
            one(r)
    else:
        def body(r, carry):
            one(r)
            return carry
        lax.fori_loop(0, n, body, 0)


def _wait_row_gather(src_hbm, dst, sem, n):
    pltpu.make_async_copy(src_hbm.at[pl.ds(0, n), :], dst, sem).wait()


MOE_TILES_PER_STEP = 2


def _moe_kernel(src_ref, ea_ref, eb_ref, nused_ref, hrow_ref, *rest, tm):
    k = MOE_TILES_PER_STEP
    weights = [rest[6 * i:6 * i + 6] for i in range(k)]
    ys_ref, gbuf, sem = rest[6 * k:]
    j = pl.program_id(0)
    nused = nused_ref[0]

    def gather(tile, slot, unroll):
        @pl.when(tile < nused)
        def _():
            _start_row_gather(src_ref, tile * tm, hrow_ref, gbuf.at[slot], sem.at[slot], tm, unroll=unroll)

    @pl.when(j == 0)
    def _():
        for t in range(2 * k):
            gather(t, t, unroll=False)

    for i in range(k):
        tile = j * k + i
        slot = (j % 2) * k + i
        wga, wua, wda, wgb, wub, wdb = weights[i]
        out = ys_ref.at[pl.ds(i * tm, tm), :]

        @pl.when(tile < nused)
        def _():
            _wait_row_gather(hrow_ref, gbuf.at[slot], sem.at[slot], tm)
            h = gbuf[slot, :, :D].astype(BF16)
            ga = gbuf[slot, :, D:D + 1]
            gb = gbuf[slot, :, D + 1:D + 2]

            def expert(wg, wu, wd):
                a = jnp.dot(h, wg[0], preferred_element_type=F32)
                u = jnp.dot(h, wu[0], preferred_element_type=F32)
                act = (_silu(a) * u).astype(BF16)
                return jnp.dot(act, wd[0], preferred_element_type=F32)

            out[...] = ga * expert(wga, wua, wda) + gb * expert(wgb, wub, wdb)

        @pl.when(tile >= nused)
        def _():
            out[...] = jnp.zeros((tm, D), F32)

        gather(tile + 2 * k, slot, unroll=True)


def _moe(hrow, src, ea, eb, nused, wg, wu, wd, *, tm, layer):
    k = MOE_TILES_PER_STEP
    n_tiles = ea.shape[0]
    assert n_tiles % k == 0
    f = wg.shape[-1]
    e0 = layer * N_EXPERTS
    w_specs = []
    for i in range(k):
        wa = lambda j, src, ea, eb, nu, i=i: (e0 + ea[j * k + i], 0, 0)
        wb = lambda j, src, ea, eb, nu, i=i: (e0 + eb[j * k + i], 0, 0)
        w_specs += [pl.BlockSpec((1, D, f), wa), pl.BlockSpec((1, D, f), wa), pl.BlockSpec((1, f, D), wa),
                    pl.BlockSpec((1, D, f), wb), pl.BlockSpec((1, D, f), wb), pl.BlockSpec((1, f, D), wb)]
    return pl.pallas_call(
        functools.partial(_moe_kernel, tm=tm),
        out_shape=jax.ShapeDtypeStruct((n_tiles * tm, D), F32),
        grid_spec=pltpu.PrefetchScalarGridSpec(
            num_scalar_prefetch=4,
            grid=(n_tiles // k,),
            in_specs=[pl.BlockSpec(memory_space=pl.ANY)] + w_specs,
            out_specs=pl.BlockSpec((k * tm, D), lambda j, src, ea, eb, nu: (j, 0)),
            scratch_shapes=[pltpu.VMEM((2 * k, tm, ROW_W), F32), pltpu.SemaphoreType.DMA((2 * k,))],
        ),
        compiler_params=_cparams(("arbitrary",), disable_bounds_checks=True),
        name="moe_experts",
    )(src, ea, eb, nused, hrow, *([wg, wu, wd, wg, wu, wd] * k))


def _combine_base(step, *, ts, tok_stride, tok_off):
    nt = pl.num_programs(1)
    return (step // nt) * tok_stride + tok_off + (step % nt) * ts


def _combine_gather(pos_ref, ys_ref, ybuf, sem, *, ts, tok_stride, tok_off):
    nt = pl.num_programs(1)
    n = pl.program_id(0) * nt + pl.program_id(1)

    @pl.when(n == 0)
    def _():
        _start_row_gather(pos_ref, _combine_base(0, ts=ts, tok_stride=tok_stride, tok_off=tok_off), ys_ref,
                          ybuf.at[0], sem.at[0], ts, unroll=False)

    nxt = jnp.minimum(n + 1, pl.num_programs(0) * nt - 1)
    _start_row_gather(pos_ref, _combine_base(nxt, ts=ts, tok_stride=tok_stride, tok_off=tok_off), ys_ref,
                      ybuf.at[1 - n % 2], sem.at[1 - n % 2], ts, unroll=True)
    _wait_row_gather(ys_ref, ybuf.at[n % 2], sem.at[n % 2], ts)
    return ybuf[n % 2]


def _combine_drain(ys_ref, ybuf, sem, ts):
    nt = pl.num_programs(1)
    n = pl.program_id(0) * nt + pl.program_id(1)

    @pl.when(n == pl.num_programs(0) * nt - 1)
    def _():
        _wait_row_gather(ys_ref, ybuf.at[1 - n % 2], sem.at[1 - n % 2], ts)


def _rope(v, cos, sin):
    lane = lax.broadcasted_iota(jnp.int32, v.shape, 1)
    first = (lane % (2 * ROPE_PAIRS)) < ROPE_PAIRS
    partner = jnp.where(first, pltpu.roll(v, LANES - ROPE_PAIRS, 1), pltpu.roll(v, ROPE_PAIRS, 1))
    return v * cos + partner * sin


def _qkv_kernel(pos_ref, x_ref, ys_ref, modp_ref, mod_ref, n1g_ref, w_ref, cq_ref, sq_ref, ck_ref, sk_ref,
                x_out_ref, q_ref, kd_ref, vd_ref, ybuf, sem, *, ts, seq, tok_stride, tok_off):
    y = _combine_gather(pos_ref, ys_ref, ybuf, sem, ts=ts, tok_stride=tok_stride, tok_off=tok_off)
    g2_prev = modp_ref[0][5:6, :]
    x2 = x_ref[0] + g2_prev * y
    x_out_ref[0] = x2
    mod = mod_ref[0]
    sh1 = mod[0:1, :]
    gm1 = n1g_ref[...] * (1.0 + mod[1:2, :])
    hb = _rms_mod(x2, gm1, sh1).astype(BF16)
    cq, sq, ck, sk = cq_ref[...], sq_ref[...], ck_ref[...], sk_ref[...]
    nw = 2 * LANES
    for c0 in range(0, D, nw):
        qc = jnp.dot(hb, w_ref[:, c0:c0 + nw], preferred_element_type=F32)
        for l0 in range(0, nw, LANES):
            q_ref[0, :, c0 + l0:c0 + l0 + LANES] = _rope(qc[:, l0:l0 + LANES], cq, sq).astype(BF16)
    for c0 in range(0, 2 * KV_DIM, nw):
        kc = jnp.dot(hb, w_ref[:, D + c0:D + c0 + nw], preferred_element_type=F32)
        for l0 in range(0, nw, LANES):
            kd_ref[0, :, c0 + l0:c0 + l0 + LANES] = _rope(kc[:, l0:l0 + LANES], ck, sk).astype(BF16)
    vd_ref[0] = jnp.dot(hb, w_ref[:, D + 2 * KV_DIM:], preferred_element_type=F32).astype(BF16)
    _combine_drain(ys_ref, ybuf, sem, ts)


def _qkv(pos, xs, ys, modp, mod, n1g, w_ext, cq, sq, ck, sk, *, ts, tok_stride, tok_off):
    bsz, s, _ = xs.shape
    nt = s // ts
    per_batch = mod.shape[0] > 1
    mod_map = (lambda b, i, p: (b, 0, 0)) if per_batch else (lambda b, i, p: (0, 0, 0))
    const = lambda b, i, p: (0, 0)
    tok = lambda b, i, p: (b, i, 0)
    tab = lambda b, i, p: (i, 0)
    return pl.pallas_call(
        functools.partial(_qkv_kernel, ts=ts, seq=s, tok_stride=tok_stride, tok_off=tok_off),
        out_shape=(jax.ShapeDtypeStruct((bsz, s, D), F32), jax.ShapeDtypeStruct((bsz, s, D), BF16),
                   jax.ShapeDtypeStruct((bsz, s, 2 * KV_DIM), BF16), jax.ShapeDtypeStruct((bsz, s, 2 * KV_DIM), BF16)),
        grid_spec=pltpu.PrefetchScalarGridSpec(
            num_scalar_prefetch=1,
            grid=(bsz, nt),
            in_specs=[
                pl.BlockSpec((1, ts, D), tok),
                pl.BlockSpec(memory_space=pl.ANY),
                pl.BlockSpec((1, 6, D), mod_map), pl.BlockSpec((1, 6, D), mod_map),
                pl.BlockSpec((1, D), const),
                pl.BlockSpec((D, D + 4 * KV_DIM), const),
                pl.BlockSpec((ts, LANES), tab), pl.BlockSpec((ts, LANES), tab),
                pl.BlockSpec((ts, LANES), tab), pl.BlockSpec((ts, LANES), tab),
            ],
            out_specs=(pl.BlockSpec((1, ts, D), tok), pl.BlockSpec((1, ts, D), tok),
                       pl.BlockSpec((1, ts, 2 * KV_DIM), tok), pl.BlockSpec((1, ts, 2 * KV_DIM), tok)),
            scratch_shapes=[pltpu.VMEM((2, ts, D), F32), pltpu.SemaphoreType.DMA((2,))],
        ),
        compiler_params=_cparams(("arbitrary", "arbitrary"), disable_bounds_checks=True),
        name="combine_qkv",
    )(pos, xs, ys, modp, mod, n1g, w_ext, cq, sq, ck, sk)


ATTN_QBLOCKS = 2


def _attn_kernel(q_ref, *rest, lc):
    nq = ATTN_QBLOCKS
    k_refs = rest[:nq + 2]
    v_refs = rest[nq + 2:2 * nq + 4]
    kx_ref, vx_ref, sink_ref, o_ref = rest[2 * nq + 4:]
    i = pl.program_id(1)
    nsteps = pl.num_programs(1)
    nband = 3 * QBLK
    qrow = lax.broadcasted_iota(jnp.int32, (GROUP * QBLK, nband), 0) % QBLK
    kcol = lax.broadcasted_iota(jnp.int32, (GROUP * QBLK, nband), 1)
    lane = lax.broadcasted_iota(jnp.int32, (1, LANES), 1)
    low = lane < HEAD_DIM
    units = [(qb, h) for qb in range(nq) for h in range(N_KV_HEADS)]

    def band_ok(qb):
        lo_lim = jnp.where(i > 0, 0, QBLK) if qb == 0 else 0
        hi_lim = jnp.where(i < nsteps - 1, nband, 2 * QBLK - 1) if qb == nq - 1 else nband
        return (kcol >= jnp.maximum(qrow, lo_lim)) & (kcol <= jnp.minimum(qrow + 2 * WINDOW, hi_lim))

    ok = [band_ok(qb) for qb in range(nq)]

    def scores(unit):
        qb, h = unit
        hs = slice(h * LANES, (h + 1) * LANES)
        k_all = jnp.concatenate([r[0, :, hs] for r in k_refs[qb:qb + 3]] + [kx_ref[0, :, hs]], axis=0)
        qs = []
        for j in range(GROUP):
            c0 = h * GROUP * HEAD_DIM + (j // 2) * LANES
            qh = q_ref[0, qb * QBLK:(qb + 1) * QBLK, c0:c0 + LANES]
            keep = low if j % 2 == 0 else jnp.logical_not(low)
            qs.append(jnp.where(keep, qh, jnp.zeros_like(qh)))
        qm = jnp.concatenate(qs, axis=0)
        return lax.dot_general(qm, k_all, (((1,), (1,)), ((), ())), preferred_element_type=F32)

    def softmax(unit, s):
        qb, h = unit
        s = jnp.concatenate([jnp.where(ok[qb], s[:, :nband], NEG_INF), s[:, nband:]], axis=1)
        sink = LOG2E * jnp.concatenate(
            [jnp.broadcast_to(sink_ref[:, h * GROUP + j:h * GROUP + j + 1], (QBLK, 1)) for j in range(GROUP)], axis=0)
        nblk = s.shape[1] // LANES
        smax = s[:, :LANES]
        for kb in range(1, nblk):
            smax = jnp.maximum(smax, s[:, kb * LANES:(kb + 1) * LANES])
        m = jnp.maximum(jnp.max(smax, axis=1, keepdims=True), sink)
        return jnp.exp2(s - m).astype(BF16), jnp.exp2(sink - m)

    def weighted_values(unit, p, p_sink):
        qb, h = unit
        hs = slice(h * LANES, (h + 1) * LANES)
        v_all = jnp.concatenate([r[0, :, hs] for r in v_refs[qb:qb + 3]] + [vx_ref[0, :, hs]], axis=0)
        ones_lo = jnp.broadcast_to(jnp.where(lane == HEAD_DIM, 1.0, 0.0).astype(v_all.dtype), v_all.shape)
        ones_hi = jnp.broadcast_to(jnp.where(lane == 0, 1.0, 0.0).astype(v_all.dtype), v_all.shape)
        v_lo = jnp.where(low, v_all, ones_lo)
        v_hi = jnp.where(low, ones_hi, v_all)
        for pr in range(GROUP // 2):
            r0 = 2 * pr * QBLK
            o_lo = jnp.dot(p[r0:r0 + QBLK], v_lo, preferred_element_type=F32)
            o_hi = jnp.dot(p[r0 + QBLK:r0 + 2 * QBLK], v_hi, preferred_element_type=F32)
            inv_lo = 1.0 / (o_lo[:, HEAD_DIM:HEAD_DIM + 1] + p_sink[r0:r0 + QBLK])
            inv_hi = 1.0 / (o_hi[:, 0:1] + p_sink[r0 + QBLK:r0 + 2 * QBLK])
            c0 = h * GROUP * HEAD_DIM + pr * LANES
            o_ref[0, qb * QBLK:(qb + 1) * QBLK, c0:c0 + LANES] = (
                jnp.where(low, o_lo * inv_lo, o_hi * inv_hi).astype(BF16))

    n = len(units)
    s_all = [scores(units[0]), scores(units[1])]
    p_cur = softmax(units[0], s_all[0])
    for u in range(n):
        if u + 2 < n:
            s_all.append(scores(units[u + 2]))
        p_next = softmax(units[u + 1], s_all[u + 1]) if u + 1 < n else None
        weighted_values(units[u], *p_cur)
        p_cur = p_next


def _attention(q, kd, vd, kxd, vxd, sink):
    bsz, s, _ = q.shape
    lc = kxd.shape[1]
    nq = ATTN_QBLOCKS
    nb = s // QBLK
    assert nb % nq == 0
    tok = lambda b, i: (b, i, 0)
    ctx = lambda b, i: (b, 0, 0)
    kvw = 2 * KV_DIM
    band = [pl.BlockSpec((1, QBLK, kvw), lambda b, i, d=d: (b, jnp.clip(nq * i - 1 + d, 0, nb - 1), 0))
            for d in range(nq + 2)]
    return pl.pallas_call(
        functools.partial(_attn_kernel, lc=lc),
        out_shape=jax.ShapeDtypeStruct((bsz, s, D), BF16),
        grid=(bsz, nb // nq),
        in_specs=[pl.BlockSpec((1, nq * QBLK, D), tok)] + band + band + [
            pl.BlockSpec((1, lc, kvw), ctx), pl.BlockSpec((1, lc, kvw), ctx),
            pl.BlockSpec((1, N_HEADS), lambda b, i: (0, 0)),
        ],
        out_specs=pl.BlockSpec((1, nq * QBLK, D), tok),
        compiler_params=_cparams(("arbitrary", "arbitrary")),
        name="window_attention",
    )(q, *([kd] * (nq + 2)), *([vd] * (nq + 2)), kxd, vxd, sink)


def _attn_out_kernel(a_ref, x_ref, mod_ref, n2g_ref, wo_ref, rwt_ref, rb_ref, x_out_ref, hrow_ref, cls_ref):
    mod = mod_ref[0]
    g1, sh2, sc2 = mod[2:3, :], mod[3:4, :], mod[4:5, :]
    gm2 = n2g_ref[...] * (1.0 + sc2)
    y = jnp.dot(a_ref[0], wo_ref[...], preferred_element_type=F32)
    x_out_ref[0] = _epilogue(x_ref[0], y, g1, sh2, gm2, rwt_ref, rb_ref, hrow_ref, cls_ref)


def _attn_out(attn, xs, mod, n2g, wo, rwt, rb, *, ts):
    bsz, s, _ = xs.shape
    nt = s // ts
    t_all = bsz * s
    const = lambda b, i: (0, 0)
    tok = lambda b, i: (b, i, 0)
    return pl.pallas_call(
        _attn_out_kernel,
        out_shape=(jax.ShapeDtypeStruct((bsz, s, D), F32),
                   jax.ShapeDtypeStruct((t_all, ROW_W), F32),
                   jax.ShapeDtypeStruct((1, t_all), jnp.int32)),
        grid=(bsz, nt),
        in_specs=[
            pl.BlockSpec((1, ts, D), tok), pl.BlockSpec((1, ts, D), tok),
            pl.BlockSpec((1, 6, D), lambda b, i: (b, 0, 0)),
            pl.BlockSpec((1, D), const), pl.BlockSpec((D, D), const),
            pl.BlockSpec((2, N_EXPERTS, D), lambda b, i: (0, 0, 0)), pl.BlockSpec((N_EXPERTS, 1), const),
        ],
        out_specs=(pl.BlockSpec((1, ts, D), tok),
                   pl.BlockSpec((ts, ROW_W), lambda b, i: (b * nt + i, 0)),
                   pl.BlockSpec((1, ts), lambda b, i: (0, b * nt + i))),
        compiler_params=_cparams(("arbitrary", "arbitrary")),
        name="attn_out",
    )(attn, xs, mod, n2g, wo, rwt, rb)


def _final_kernel(pos_ref, x_ref, ys_ref, mod_ref, fg_ref, o_ref, ybuf, sem, *, ts, seq):
    y = _combine_gather(pos_ref, ys_ref, ybuf, sem, ts=ts, tok_stride=seq, tok_off=0)
    x4 = x_ref[0] + mod_ref[0][5:6, :] * y
    r = lax.rsqrt(jnp.mean(x4 * x4, axis=-1, keepdims=True) + EPS)
    o_ref[0] = x4 * r * fg_ref[...]
    _combine_drain(ys_ref, ybuf, sem, ts)


def _final(pos, xs, ys, mod, fg, *, ts):
    bsz, s, _ = xs.shape
    tok = lambda b, i, p: (b, i, 0)
    return pl.pallas_call(
        functools.partial(_final_kernel, ts=ts, seq=s),
        out_shape=jax.ShapeDtypeStruct((bsz, s, D), F32),
        grid_spec=pltpu.PrefetchScalarGridSpec(
            num_scalar_prefetch=1,
            grid=(bsz, s // ts),
            in_specs=[pl.BlockSpec((1, ts, D), tok), pl.BlockSpec(memory_space=pl.ANY),
                      pl.BlockSpec((1, 6, D), lambda b, i, p: (b, 0, 0)), pl.BlockSpec((1, D), lambda b, i, p: (0, 0))],
            out_specs=pl.BlockSpec((1, ts, D), tok),
            scratch_shapes=[pltpu.VMEM((2, ts, D), F32), pltpu.SemaphoreType.DMA((2,))],
        ),
        compiler_params=_cparams(("arbitrary", "arbitrary"), disable_bounds_checks=True),
        name="combine_final",
    )(pos, xs, ys, mod, fg)


def _rope_tables(seq):
    rows = seq // GRID_W
    inv_freq = jnp.power(jnp.float32(ROPE_THETA), -jnp.arange(ROPE_PAIRS, dtype=F32) / ROPE_PAIRS)
    ang_r = jnp.arange(rows, dtype=jnp.int32).astype(F32)[:, None] * inv_freq
    ang_c = jnp.arange(GRID_W, dtype=jnp.int32).astype(F32)[:, None] * inv_freq
    by_row = lambda v: jnp.repeat(v, GRID_W, axis=0)
    by_col = lambda v: jnp.tile(v, (rows, 1))
    cos_r, sin_r = by_row(jnp.cos(ang_r)), by_row(jnp.sin(ang_r))
    cos_c, sin_c = by_col(jnp.cos(ang_c)), by_col(jnp.sin(ang_c))
    cos = jnp.concatenate([cos_r, cos_r, cos_c, cos_c], axis=1)
    sin = jnp.concatenate([-sin_r, sin_r, -sin_c, sin_c], axis=1)
    cos = jnp.concatenate([cos, cos], axis=1)
    sin = jnp.concatenate([sin, sin], axis=1)
    scale = HEAD_DIM ** -0.5 * LOG2E
    return cos * scale, sin * scale, cos, sin


def kernel(x, c, ctx, c_ctx, w_mod, b_mod, norm1_g, norm2_g, conv_w_pw1, conv_b_pw1, conv_w_dw, conv_b_dw,
           conv_ln_g, conv_ln_b, conv_w_pw2, conv_b_pw2, attn_w_qkv, attn_w_o, attn_sink, router_w, router_b,
           moe_w_gate, moe_w_up, moe_w_down, final_g):
    bsz, s, d = x.shape
    lc = ctx.shape[1]
    ts = 512
    tm = 256
    assert d == D and w_mod.shape[0] == 2 and bsz + 1 <= COND_ROWS
    assert lc <= ts and lc % 256 == 0 and s % ts == 0 and s % (ATTN_QBLOCKS * QBLK) == 0

    cond = jnp.concatenate([c, c_ctx[None, :], jnp.zeros((COND_ROWS - bsz - 1, D), F32)], axis=0)
    mods = _adaln(cond, w_mod, b_mod).reshape(2, COND_ROWS, 6, D)
    mod_lat = [mods[l, :bsz] for l in range(2)]
    mod_ctx = [mods[l, bsz:bsz + 1] for l in range(2)]

    rwt_hi = router_w.T.astype(BF16)
    rwt = jnp.stack([rwt_hi, (router_w.T - rwt_hi.astype(F32)).astype(BF16)])
    rb = router_b.reshape(N_EXPERTS, 1)
    row = lambda v: v.reshape(1, -1)
    wg = moe_w_gate.astype(BF16).reshape(2 * N_EXPERTS, D, D_EXPERT)
    wu = moe_w_up.astype(BF16).reshape(2 * N_EXPERTS, D, D_EXPERT)
    wd = moe_w_down.astype(BF16).reshape(2 * N_EXPERTS, D_EXPERT, D)

    x1, ctx1, hrow, cls = _conv_layer(
        x, ctx, mod_lat[0], mod_ctx[0], row(norm1_g[0]), row(norm2_g[0]), conv_w_pw1[0].astype(BF16),
        row(conv_b_pw1[0]), conv_w_dw[0], row(conv_b_dw[0]), row(conv_ln_g[0]), row(conv_ln_b[0]),
        conv_w_pw2[0].astype(BF16), row(conv_b_pw2[0]), rwt, rb, ts=ts)
    ctx1 = ctx1[:, :lc]
    rank, counts = _plan(cls)
    pos, src, ea, eb, nused = _dispatch_tables(cls, rank, counts, tm=tm, n_pad=bsz * (ts - lc))
    ys = _moe(hrow, src, ea, eb, nused, wg, wu, wd, tm=tm, layer=0)

    wq = attn_w_qkv[0][:, :D]
    wk = attn_w_qkv[0][:, D:D + KV_DIM].reshape(D, N_KV_HEADS, 1, HEAD_DIM)
    wv = attn_w_qkv[0][:, D + KV_DIM:].reshape(D, N_KV_HEADS, 1, HEAD_DIM)
    dup = lambda w: jnp.broadcast_to(w, (D, N_KV_HEADS, 2, HEAD_DIM)).reshape(D, 2 * KV_DIM)
    w_ext = jnp.concatenate([wq, dup(wk), dup(wv)], axis=1).astype(BF16)
    cq, sq, ck, sk = _rope_tables(s)
    ones = jnp.ones((lc, LANES), F32)
    zeros = jnp.zeros((lc, LANES), F32)
    n1g = row(norm1_g[1])
    x2, q, kd, vd = _qkv(pos, x1, ys, mod_lat[0], mod_lat[1], n1g, w_ext, cq, sq, ck, sk,
                         ts=ts, tok_stride=s + ts, tok_off=0)
    _, _, kxd, vxd = _qkv(pos, ctx1, ys, mod_ctx[0], mod_ctx[1], n1g, w_ext, ones, zeros, ones, zeros,
                          ts=lc, tok_stride=s + ts, tok_off=s)
    attn = _attention(q, kd, vd, kxd, vxd, attn_sink[0].reshape(1, N_HEADS))
    x3, hrow1, cls1 = _attn_out(attn, x2, mod_lat[1], row(norm2_g[1]), attn_w_o[0].astype(BF16), rwt, rb, ts=ts)
    rank1, counts1 = _plan(cls1)
    pos1, src1, ea1, eb1, nused1 = _dispatch_tables(cls1, rank1, counts1, tm=tm)
    ys1 = _moe(hrow1, src1, ea1, eb1, nused1, wg, wu, wd, tm=tm, layer=1)
    return _final(pos1, x3, ys1, mod_lat[1], row(final_g), ts=ts)
```

```python
import functools
import math

import jax
import jax.numpy as jnp
from jax import lax
from jax.experimental import pallas as pl
from jax.experimental.pallas import tpu as pltpu

D = 1024
GRID_W = 64
CONV_WIDTH = 31
CONV_PAD = CONV_WIDTH // 2
SUBLANES = 8
HALO = 16
HEAD_DIM = 64
N_HEADS = D // HEAD_DIM
N_KV_HEADS = N_HEADS // 4
GROUP = N_HEADS // N_KV_HEADS
KV_DIM = N_KV_HEADS * HEAD_DIM
WINDOW = 128
QBLK = 128
ROPE_THETA = 10000.0
ROPE_PAIRS = HEAD_DIM // 4
N_EXPERTS = 16
N_GROUPS = 4
EXPERTS_PER_GROUP = N_EXPERTS // N_GROUPS
N_PAIRS = 6
N_CLASSES = N_GROUPS * N_PAIRS
CLASS_ROWS = 32
NO_CLASS = CLASS_ROWS - 1
D_EXPERT = D // 2
EPS = 1e-6
NEG_INF = -1e30
LOG2E = 1.4426950408889634
LANES = 128
AUX = LANES
ROW_W = D + AUX
COND_ROWS = 16
VMEM_LIMIT = 56 * 1024 * 1024

F32 = jnp.float32
BF16 = jnp.bfloat16
HIGHEST = lax.Precision.HIGHEST


def _cparams(sem, **kw):
    return pltpu.CompilerParams(dimension_semantics=sem, vmem_limit_bytes=VMEM_LIMIT, **kw)


def _sigmoid(v):
    return jax.nn.sigmoid(v)


def _silu(v):
    return v * jax.nn.sigmoid(v)


def _rms_mod(xr, gm, sh):
    r = lax.rsqrt(jnp.mean(xr * xr, axis=-1, keepdims=True) + EPS)
    return xr * r * gm + sh


N_CAST = 3


def _cast_plumbing(arrays, part, n_parts, n_steps, step_of):
    in_specs, out_specs, out_shapes = [], [], []
    for a in arrays:
        rows, cols = a.shape[0] // n_parts, a.shape[1]
        blk = rows
        while (blk // 2) % 16 == 0 and (blk // 2) * n_steps >= rows:
            blk //= 2
        assert rows % blk == 0 and a.shape[0] % n_parts == 0
        nblk = rows // blk
        in_specs.append(pl.BlockSpec(
            (blk, cols), lambda *g, nblk=nblk: (part * nblk + jnp.minimum(step_of(*g), nblk - 1), 0)))
        out_specs.append(pl.BlockSpec((blk, cols), lambda *g, nblk=nblk: (jnp.minimum(step_of(*g), nblk - 1), 0)))
        out_shapes.append(jax.ShapeDtypeStruct((rows, cols), BF16))
    return in_specs, out_specs, out_shapes


def _cast_blocks(in_refs, out_refs):
    for src, dst in zip(in_refs, out_refs):
        dst[...] = src[...].astype(BF16)


def _adaln_kernel(cond_ref, w_ref, b_ref, out_ref):
    cnd = cond_ref[...]
    act = _silu(cnd)
    out_ref[0] = jnp.dot(act, w_ref[0], precision=HIGHEST, preferred_element_type=F32) + b_ref[0]


def _adaln(cond, w_mod, b_mod):
    depth = w_mod.shape[0]
    tn = 1536
    return pl.pallas_call(
        _adaln_kernel,
        out_shape=jax.ShapeDtypeStruct((depth, COND_ROWS, 6 * D), F32),
        grid=(depth, 6 * D // tn),
        in_specs=[
            pl.BlockSpec((COND_ROWS, D), lambda l, n: (0, 0)),
            pl.BlockSpec((1, D, tn), lambda l, n: (l, 0, n)),
            pl.BlockSpec((1, 1, tn), lambda l, n: (l, 0, n)),
        ],
        out_specs=pl.BlockSpec((1, COND_ROWS, tn), lambda l, n: (l, 0, n)),
        compiler_params=_cparams(("arbitrary", "arbitrary")),
        name="adaln",
    )(cond, w_mod, b_mod.reshape(depth, 1, 6 * D))


def _route_rows(h2, rwt_ref, rb_ref):
    nt_dims = (((1,), (1,)), ((), ()))
    h_hi = h2.astype(BF16)
    h_lo = (h2 - h_hi.astype(F32)).astype(BF16)
    w_hi = rwt_ref[0]
    w_lo = rwt_ref[1]
    lt = (lax.dot_general(w_hi, h_hi, nt_dims, preferred_element_type=F32)
          + lax.dot_general(w_hi, h_lo, nt_dims, preferred_element_type=F32)
          + lax.dot_general(w_lo, h_hi, nt_dims, preferred_element_type=F32))
    s = _sigmoid(lt)
    sel = s + rb_ref[...]
    srow = [s[e:e + 1, :] for e in range(N_EXPERTS)]
    vrow = [sel[e:e + 1, :] for e in range(N_EXPERTS)]
    gscore = []
    for g in range(N_GROUPS):
        a, b, c, d = vrow[4 * g:4 * g + 4]
        gscore.append(jnp.maximum(jnp.maximum(jnp.maximum(a + b, a + c), jnp.maximum(a + d, b + c)),
                                  jnp.maximum(b + d, c + d)))
    bg = jnp.zeros_like(gscore[0], dtype=jnp.int32)
    best = gscore[0]
    for g in range(1, N_GROUPS):
        better = gscore[g] > best
        bg = jnp.where(better, g, bg)
        best = jnp.where(better, gscore[g], best)
    v = []
    sv = []
    for j in range(EXPERTS_PER_GROUP):
        vj = vrow[j]
        sj = srow[j]
        for g in range(1, N_GROUPS):
            vj = jnp.where(bg == g, vrow[4 * g + j], vj)
            sj = jnp.where(bg == g, srow[4 * g + j], sj)
        v.append(vj)
        sv.append(sj)
    i1 = jnp.zeros_like(bg)
    m1 = v[0]
    for j in range(1, EXPERTS_PER_GROUP):
        gt = v[j] > m1
        i1 = jnp.where(gt, j, i1)
        m1 = jnp.where(gt, v[j], m1)
    v2 = [jnp.where(i1 == j, -jnp.inf, v[j]) for j in range(EXPERTS_PER_GROUP)]
    i2 = jnp.zeros_like(bg)
    m2 = v2[0]
    for j in range(1, EXPERTS_PER_GROUP):
        gt = v2[j] > m2
        i2 = jnp.where(gt, j, i2)
        m2 = jnp.where(gt, v2[j], m2)
    lo = jnp.minimum(i1, i2)
    hi = jnp.maximum(i1, i2)
    s_lo = sv[0]
    s_hi = sv[0]
    for j in range(1, EXPERTS_PER_GROUP):
        s_lo = jnp.where(lo == j, sv[j], s_lo)
        s_hi = jnp.where(hi == j, sv[j], s_hi)
    den = s_lo + s_hi
    pair = jnp.where(lo == 0, 0, jnp.where(lo == 1, 3, 5)) + (hi - lo - 1)
    cls = N_PAIRS * bg + pair
    return s_lo / den, s_hi / den, cls


def _epilogue(x_in, y, g1, sh2, gm2, rwt_ref, rb_ref, hrow_ref, cls_ref, n_valid=None):
    x1 = x_in + g1 * y
    h2 = _rms_mod(x1, gm2, sh2)
    ga, gb, cls = _route_rows(h2, rwt_ref, rb_ref)
    ts = h2.shape[0]
    if n_valid is not None:
        cls = jnp.where(lax.broadcasted_iota(jnp.int32, cls.shape, 1) < n_valid, cls, NO_CLASS)
    cls_ref[...] = cls
    g8 = jnp.concatenate([ga, gb, jnp.zeros((6, ts), F32)], axis=0)
    g128 = jnp.concatenate([g8, jnp.zeros((AUX - 8, ts), F32)], axis=0)
    hrow_ref[:, :D] = h2
    hrow_ref[:, D:] = jnp.transpose(g128)
    return x1


def _conv_layer_kernel(x_ref, c_ref, xp_ref, xn_ref, modl_ref, modc_ref, n1g_ref, n2g_ref, w1_ref, b1_ref, wdw_ref,
                       bdw_ref, lng_ref, lnb_ref, w2_ref, b2_ref, rwt_ref, rb_ref, *rest, ts, n_ctx):
    cast_in = rest[:N_CAST]
    x_out_ref, c_out_ref, hrow_ref, cls_ref = rest[N_CAST:N_CAST + 4]
    cast_out = rest[N_CAST + 4:2 * N_CAST + 4]
    u_scr, cv_scr, v_scr = rest[2 * N_CAST + 4:]
    _cast_blocks(cast_in, cast_out)
    i = pl.program_id(1)
    nt = pl.num_programs(1) - 1
    is_ctx = jnp.full((1, 1), i, jnp.int32) == nt
    n_valid = jnp.where(i == nt, n_ctx, ts)
    mod = jnp.where(is_ctx, modc_ref[0], modl_ref[0])
    sh1, sc1, g1, sh2, sc2 = (mod[k:k + 1, :] for k in range(5))
    gm1 = n1g_ref[...] * (1.0 + sc1)
    gm2 = n2g_ref[...] * (1.0 + sc2)
    cw = 512

    def glu(hb, c0):
        a = jnp.dot(hb, w1_ref[:, c0:c0 + cw], preferred_element_type=F32) + b1_ref[:, c0:c0 + cw]
        g = jnp.dot(hb, w1_ref[:, D + c0:D + c0 + cw], preferred_element_type=F32) + b1_ref[:, D + c0:D + c0 + cw]
        return a * _sigmoid(g)

    xm = jnp.where(is_ctx, c_ref[0], x_ref[0])
    hb = jnp.concatenate([_rms_mod(xp_ref[0, 0], gm1, sh1).astype(BF16), _rms_mod(xm, gm1, sh1).astype(BF16),
                          _rms_mod(xn_ref[0, 0], gm1, sh1).astype(BF16)], axis=0)
    keep_prev = jnp.logical_and(i > 0, i < nt).astype(F32)
    keep_next = (i < nt - 1).astype(F32)
    keep_row = (lax.broadcasted_iota(jnp.int32, (ts, 1), 0) < n_valid).astype(F32)
    for c0 in range(0, D, cw):
        u = glu(hb, c0)
        u_scr[0:HALO, c0:c0 + cw] = u[:HALO] * keep_prev
        u_scr[HALO:HALO + ts, c0:c0 + cw] = u[HALO:HALO + ts] * keep_row
        u_scr[HALO + ts:2 * HALO + ts, c0:c0 + cw] = u[HALO + ts:] * keep_next

    rr = 128
    taps_off = HALO - CONV_PAD

    def conv_chunk(rc, carry):
        r0 = pl.multiple_of(rc * rr, rr)
        for c0 in range(0, D, LANES):
            win = u_scr[pl.ds(r0, rr + 2 * HALO), c0:c0 + LANES]
            out = None
            for r in range(SUBLANES):
                nrow = rr + SUBLANES if r else rr
                part = None
                for q in range((CONV_WIDTH + taps_off) // SUBLANES + 1):
                    k = SUBLANES * q + r - taps_off
                    if 0 <= k < CONV_WIDTH:
                        term = win[SUBLANES * q:SUBLANES * q + nrow, :] * wdw_ref[k:k + 1, c0:c0 + LANES]
                        part = term if part is None else part + term
                shifted = part[r:r + rr, :]
                out = shifted if out is None else out + shifted
            cv_scr[pl.ds(r0, rr), c0:c0 + LANES] = out
        cv = cv_scr[pl.ds(r0, rr), :] + bdw_ref[...]
        mu = jnp.mean(cv, axis=-1, keepdims=True)
        dv = cv - mu
        var = jnp.mean(dv * dv, axis=-1, keepdims=True)
        ln = dv * lax.rsqrt(var + EPS) * lng_ref[...] + lnb_ref[...]
        v_scr[pl.ds(r0, rr), :] = _silu(ln).astype(BF16)
        return carry

    lax.fori_loop(0, ts // rr, conv_chunk, 0)
    y = jnp.dot(v_scr[...], w2_ref[...], preferred_element_type=F32) + b2_ref[...]
    x1 = _epilogue(xm, y, g1, sh2, gm2, rwt_ref, rb_ref, hrow_ref, cls_ref, n_valid=n_valid)

    @pl.when(i < nt)
    def _():
        x_out_ref[0] = x1

    @pl.when(i == nt)
    def _():
        c_out_ref[0] = x1


def _conv_layer(xs, cs, mod_lat, mod_ctx, n1g, n2g, w1, b1, wdw, bdw, lng, lnb, w2, b2, rwt, rb, to_cast, *, ts,
                cast_part):
    bsz, s, _ = xs.shape
    n_ctx = cs.shape[1]
    assert n_ctx <= ts
    cs = jnp.pad(cs, ((0, 0), (0, ts - n_ctx), (0, 0)))
    nt = s // ts
    nh = s // HALO
    hpt = ts // HALO
    x4 = xs.reshape(bsz, nh, HALO, D)
    t_all = bsz * (s + ts)
    const = lambda b, i: (0, 0)
    lat = lambda b, i: (b, jnp.minimum(i, nt - 1), 0)
    row_blk = lambda b, i: (b * (nt + 1) + i, 0)
    cast_in, cast_out, cast_shapes = _cast_plumbing(to_cast, *cast_part, bsz * (nt + 1), lambda b, i: b * (nt + 1) + i)
    return pl.pallas_call(
        functools.partial(_conv_layer_kernel, ts=ts, n_ctx=n_ctx),
        out_shape=(jax.ShapeDtypeStruct((bsz, s, D), F32),
                   jax.ShapeDtypeStruct((bsz, ts, D), F32),
                   jax.ShapeDtypeStruct((t_all, ROW_W), F32),
                   jax.ShapeDtypeStruct((1, t_all), jnp.int32), *cast_shapes),
        grid=(bsz, nt + 1),
        in_specs=[
            pl.BlockSpec((1, ts, D), lat),
            pl.BlockSpec((1, ts, D), lambda b, i: (b, 0, 0)),
            pl.BlockSpec((1, 1, HALO, D), lambda b, i: (b, jnp.clip(i * hpt - 1, 0, nh - 1), 0, 0)),
            pl.BlockSpec((1, 1, HALO, D), lambda b, i: (b, jnp.minimum((i + 1) * hpt, nh - 1), 0, 0)),
            pl.BlockSpec((1, 6, D), lambda b, i: (b, 0, 0)),
            pl.BlockSpec((1, 6, D), lambda b, i: (0, 0, 0)),
            pl.BlockSpec((1, D), const), pl.BlockSpec((1, D), const),
            pl.BlockSpec((D, 2 * D), const), pl.BlockSpec((1, 2 * D), const),
            pl.BlockSpec((CONV_WIDTH, D), const), pl.BlockSpec((1, D), const),
            pl.BlockSpec((1, D), const), pl.BlockSpec((1, D), const),
            pl.BlockSpec((D, D), const), pl.BlockSpec((1, D), const),
            pl.BlockSpec((2, N_EXPERTS, D), lambda b, i: (0, 0, 0)), pl.BlockSpec((N_EXPERTS, 1), const),
        ] + cast_in,
        out_specs=(pl.BlockSpec((1, ts, D), lat),
                   pl.BlockSpec((1, ts, D), lambda b, i: (b, 0, 0)),
                   pl.BlockSpec((ts, ROW_W), row_blk),
                   pl.BlockSpec((1, ts), lambda b, i: (0, b * (nt + 1) + i)), *cast_out),
        scratch_shapes=[pltpu.VMEM((ts + 2 * HALO, D), F32), pltpu.VMEM((ts, D), F32), pltpu.VMEM((ts, D), BF16)],
        compiler_params=_cparams(("arbitrary", "arbitrary")),
        name="conv_layer",
    )(xs, cs, x4, x4, mod_lat, mod_ctx, n1g, n2g, w1, b1, wdw, bdw, lng, lnb, w2, b2, rwt, rb, *to_cast)


def _plan_kernel(cls_ref, rank_ref, cnt_ref, carry_scr, *, tb):
    @pl.when(pl.program_id(0) == 0)
    def _():
        carry_scr[...] = jnp.zeros_like(carry_scr)

    cls = cls_ref[...]
    onehot = (lax.broadcasted_iota(jnp.int32, (CLASS_ROWS, tb), 0) == cls).astype(F32)
    upper = (lax.broadcasted_iota(jnp.int32, (tb, tb), 0) <= lax.broadcasted_iota(jnp.int32, (tb, tb), 1))
    prefix = jnp.dot(onehot.astype(BF16), upper.astype(BF16), preferred_element_type=F32)
    carry = carry_scr[:, 0:1]
    rank = jnp.sum(onehot * (prefix + carry), axis=0, keepdims=True) - 1.0
    rank_ref[...] = rank.astype(jnp.int32)
    total = carry + jnp.sum(onehot, axis=1, keepdims=True)
    carry_scr[...] = jnp.broadcast_to(total, carry_scr.shape)
    cnt_ref[...] = jnp.broadcast_to(total, cnt_ref.shape)


def _plan(cls):
    t_all = cls.shape[1]
    tb = math.gcd(t_all, 1024)
    return pl.pallas_call(
        functools.partial(_plan_kernel, tb=tb),
        out_shape=(jax.ShapeDtypeStruct((1, t_all), jnp.int32), jax.ShapeDtypeStruct((CLASS_ROWS, LANES), F32)),
        grid=(t_all // tb,),
        in_specs=[pl.BlockSpec((1, tb), lambda i: (0, i))],
        out_specs=(pl.BlockSpec((1, tb), lambda i: (0, i)), pl.BlockSpec((CLASS_ROWS, LANES), lambda i: (0, 0))),
        scratch_shapes=[pltpu.VMEM((CLASS_ROWS, LANES), F32)],
        compiler_params=_cparams(("arbitrary",)),
        name="route_plan",
    )(cls)


def _invert_kernel(pos_ref, gap_ref, src_ref, *, n_tok, n_gap):
    def zero(i, carry):
        src_ref[i] = 0
        return carry
    for g in range(n_gap):
        lax.fori_loop(gap_ref[2 * g], gap_ref[2 * g + 1], zero, 0)

    unroll = 16

    def put(i, carry):
        for u in range(unroll):
            t = i * unroll + u
            src_ref[pos_ref[t]] = t
        return carry
    lax.fori_loop(0, n_tok // unroll, put, 0)


def _invert(pos, gaps, n_slot):
    n_tok = pos.shape[0]
    return pl.pallas_call(
        functools.partial(_invert_kernel, n_tok=n_tok, n_gap=gaps.shape[0] // 2),
        out_shape=jax.ShapeDtypeStruct((n_slot,), jnp.int32),
        in_specs=[pl.BlockSpec(memory_space=pltpu.SMEM), pl.BlockSpec(memory_space=pltpu.SMEM)],
        out_specs=pl.BlockSpec(memory_space=pltpu.SMEM),
        name="invert_slots",
    )(pos, gaps)


def _dispatch_tables(cls, rank, counts, *, tm, n_pad=0):
    t_all = cls.shape[1]
    n_tiles = -(-((t_all - n_pad) // tm + N_CLASSES) // MOE_TILES_PER_STEP) * MOE_TILES_PER_STEP
    cnt = counts[:N_CLASSES, 0].astype(jnp.int32)
    padded = ((cnt + tm - 1) // tm) * tm
    ends = jnp.cumsum(padded)
    offs = ends - padded
    first_slot = jnp.zeros((CLASS_ROWS,), jnp.int32).at[:N_CLASSES].set(offs).at[NO_CLASS].set(n_tiles * tm)
    pos = first_slot[cls[0]] + rank[0]
    gap_lo = jnp.concatenate([offs + cnt, ends[-1:]])
    gap_hi = jnp.concatenate([ends, jnp.full((1,), n_tiles * tm, jnp.int32)])
    src = _invert(pos, jnp.stack([gap_lo, gap_hi], axis=1).reshape(-1).astype(jnp.int32), n_tiles * tm + n_pad)
    n_used = ends[-1] // tm
    tile_start = jnp.minimum(jnp.arange(n_tiles, dtype=jnp.int32), n_used - 1) * tm
    tile_cls = jnp.minimum(jnp.sum(ends[None, :] <= tile_start[:, None], axis=1), N_CLASSES - 1).astype(jnp.int32)
    pair_lo = jnp.array([0, 0, 0, 1, 1, 2], jnp.int32)
    pair_hi = jnp.array([1, 2, 3, 2, 3, 3], jnp.int32)
    grp = tile_cls // N_PAIRS
    ea = EXPERTS_PER_GROUP * grp + pair_lo[tile_cls % N_PAIRS]
    eb = EXPERTS_PER_GROUP * grp + pair_hi[tile_cls % N_PAIRS]
    return pos, src, ea, eb, n_used.reshape(1).astype(jnp.int32)


def _start_row_gather(idx_ref, idx_base, src_hbm, dst, sem, n, unroll):
    def one(r):
        row = idx_ref[idx_base + r]
        pltpu.make_async_copy(src_hbm.at[pl.ds(row, 1), :], dst.at[pl.ds(r, 1), :], sem).start()

    if unroll:
        for r in range(n):
            one(r)
    else:
        def body(r, carry):
            one(r)
            return carry
        lax.fori_loop(0, n, body, 0)


def _wait_row_gather(src_hbm, dst, sem, n):
    pltpu.make_async_copy(src_hbm.at[pl.ds(0, n), :], dst, sem).wait()


MOE_TILES_PER_STEP = 2


def _moe_kernel(src_ref, ea_ref, eb_ref, nused_ref, hrow_ref, *rest, tm):
    k = MOE_TILES_PER_STEP
    weights = [rest[6 * i:6 * i + 6] for i in range(k)]
    ys_ref, gbuf, sem = rest[6 * k:]
    j = pl.program_id(0)
    nused = nused_ref[0]

    def gather(tile, slot, unroll):
        @pl.when(tile < nused)
        def _():
            _start_row_gather(src_ref, tile * tm, hrow_ref, gbuf.at[slot], sem.at[slot], tm, unroll=unroll)

    @pl.when(j == 0)
    def _():
        for t in range(2 * k):
            gather(t, t, unroll=False)

    for i in range(k):
        tile = j * k + i
        slot = (j % 2) * k + i
        wga, wua, wda, wgb, wub, wdb = weights[i]
        out = ys_ref.at[pl.ds(i * tm, tm), :]

        @pl.when(tile < nused)
        def _():
            _wait_row_gather(hrow_ref, gbuf.at[slot], sem.at[slot], tm)
            h = gbuf[slot, :, :D].astype(BF16)
            ga = gbuf[slot, :, D:D + 1]
            gb = gbuf[slot, :, D + 1:D + 2]

            def expert(wg, wu, wd):
                a = jnp.dot(h, wg[0], preferred_element_type=F32)
                u = jnp.dot(h, wu[0], preferred_element_type=F32)
                act = (_silu(a) * u).astype(BF16)
                return jnp.dot(act, wd[0], preferred_element_type=F32)

            out[...] = ga * expert(wga, wua, wda) + gb * expert(wgb, wub, wdb)

        @pl.when(tile >= nused)
        def _():
            out[...] = jnp.zeros((tm, D), F32)

        gather(tile + 2 * k, slot, unroll=True)


def _moe(hrow, src, ea, eb, nused, wg, wu, wd, *, tm):
    k = MOE_TILES_PER_STEP
    n_tiles = ea.shape[0]
    assert n_tiles % k == 0
    f = wg.shape[-1]
    w_specs = []
    for i in range(k):
        wa = lambda j, src, ea, eb, nu, i=i: (ea[j * k + i], 0, 0)
        wb = lambda j, src, ea, eb, nu, i=i: (eb[j * k + i], 0, 0)
        w_specs += [pl.BlockSpec((1, D, f), wa), pl.BlockSpec((1, D, f), wa), pl.BlockSpec((1, f, D), wa),
                    pl.BlockSpec((1, D, f), wb), pl.BlockSpec((1, D, f), wb), pl.BlockSpec((1, f, D), wb)]
    return pl.pallas_call(
        functools.partial(_moe_kernel, tm=tm),
        out_shape=jax.ShapeDtypeStruct((n_tiles * tm, D), F32),
        grid_spec=pltpu.PrefetchScalarGridSpec(
            num_scalar_prefetch=4,
            grid=(n_tiles // k,),
            in_specs=[pl.BlockSpec(memory_space=pl.ANY)] + w_specs,
            out_specs=pl.BlockSpec((k * tm, D), lambda j, src, ea, eb, nu: (j, 0)),
            scratch_shapes=[pltpu.VMEM((2 * k, tm, ROW_W), F32), pltpu.SemaphoreType.DMA((2 * k,))],
        ),
        compiler_params=_cparams(("arbitrary",), disable_bounds_checks=True),
        name="moe_experts",
    )(src, ea, eb, nused, hrow, *([wg, wu, wd, wg, wu, wd] * k))


def _combine_base(step, *, ts, tok_stride, tok_off):
    nt = pl.num_programs(1)
    return (step // nt) * tok_stride + tok_off + (step % nt) * ts


def _combine_gather(pos_ref, ys_ref, ybuf, sem, *, ts, tok_stride, tok_off):
    nt = pl.num_programs(1)
    n = pl.program_id(0) * nt + pl.program_id(1)

    @pl.when(n == 0)
    def _():
        _start_row_gather(pos_ref, _combine_base(0, ts=ts, tok_stride=tok_stride, tok_off=tok_off), ys_ref,
                          ybuf.at[0], sem.at[0], ts, unroll=False)

    nxt = jnp.minimum(n + 1, pl.num_programs(0) * nt - 1)
    _start_row_gather(pos_ref, _combine_base(nxt, ts=ts, tok_stride=tok_stride, tok_off=tok_off), ys_ref,
                      ybuf.at[1 - n % 2], sem.at[1 - n % 2], ts, unroll=True)
    _wait_row_gather(ys_ref, ybuf.at[n % 2], sem.at[n % 2], ts)
    return ybuf[n % 2]


def _combine_drain(ys_ref, ybuf, sem, ts):
    nt = pl.num_programs(1)
    n = pl.program_id(0) * nt + pl.program_id(1)

    @pl.when(n == pl.num_programs(0) * nt - 1)
    def _():
        _wait_row_gather(ys_ref, ybuf.at[1 - n % 2], sem.at[1 - n % 2], ts)


def _rope(v, cos, sin):
    lane = lax.broadcasted_iota(jnp.int32, v.shape, 1)
    first = (lane % (2 * ROPE_PAIRS)) < ROPE_PAIRS
    partner = jnp.where(first, pltpu.roll(v, LANES - ROPE_PAIRS, 1), pltpu.roll(v, ROPE_PAIRS, 1))
    return v * cos + partner * sin


def _qkv_kernel(pos_ref, x_ref, ys_ref, modp_ref, mod_ref, n1g_ref, w_ref, cq_ref, sq_ref, ck_ref, sk_ref,
                x_out_ref, q_ref, kd_ref, vd_ref, ybuf, sem, *, ts, seq, tok_stride, tok_off):
    y = _combine_gather(pos_ref, ys_ref, ybuf, sem, ts=ts, tok_stride=tok_stride, tok_off=tok_off)
    g2_prev = modp_ref[0][5:6, :]
    x2 = x_ref[0] + g2_prev * y
    x_out_ref[0] = x2
    mod = mod_ref[0]
    sh1 = mod[0:1, :]
    gm1 = n1g_ref[...] * (1.0 + mod[1:2, :])
    hb = _rms_mod(x2, gm1, sh1).astype(BF16)
    cq, sq, ck, sk = cq_ref[...], sq_ref[...], ck_ref[...], sk_ref[...]
    nw = 2 * LANES
    for c0 in range(0, D, nw):
        qc = jnp.dot(hb, w_ref[:, c0:c0 + nw], preferred_element_type=F32)
        for l0 in range(0, nw, LANES):
            q_ref[0, :, c0 + l0:c0 + l0 + LANES] = _rope(qc[:, l0:l0 + LANES], cq, sq).astype(BF16)
    for c0 in range(0, 2 * KV_DIM, nw):
        kc = jnp.dot(hb, w_ref[:, D + c0:D + c0 + nw], preferred_element_type=F32)
        for l0 in range(0, nw, LANES):
            kd_ref[0, :, c0 + l0:c0 + l0 + LANES] = _rope(kc[:, l0:l0 + LANES], ck, sk).astype(BF16)
    vd_ref[0] = jnp.dot(hb, w_ref[:, D + 2 * KV_DIM:], preferred_element_type=F32).astype(BF16)
    _combine_drain(ys_ref, ybuf, sem, ts)


def _qkv(pos, xs, ys, modp, mod, n1g, w_ext, cq, sq, ck, sk, *, ts, tok_stride, tok_off):
    bsz, s, _ = xs.shape
    nt = s // ts
    per_batch = mod.shape[0] > 1
    mod_map = (lambda b, i, p: (b, 0, 0)) if per_batch else (lambda b, i, p: (0, 0, 0))
    const = lambda b, i, p: (0, 0)
    tok = lambda b, i, p: (b, i, 0)
    tab = lambda b, i, p: (i, 0)
    return pl.pallas_call(
        functools.partial(_qkv_kernel, ts=ts, seq=s, tok_stride=tok_stride, tok_off=tok_off),
        out_shape=(jax.ShapeDtypeStruct((bsz, s, D), F32), jax.ShapeDtypeStruct((bsz, s, D), BF16),
                   jax.ShapeDtypeStruct((bsz, s, 2 * KV_DIM), BF16), jax.ShapeDtypeStruct((bsz, s, 2 * KV_DIM), BF16)),
        grid_spec=pltpu.PrefetchScalarGridSpec(
            num_scalar_prefetch=1,
            grid=(bsz, nt),
            in_specs=[
                pl.BlockSpec((1, ts, D), tok),
                pl.BlockSpec(memory_space=pl.ANY),
                pl.BlockSpec((1, 6, D), mod_map), pl.BlockSpec((1, 6, D), mod_map),
                pl.BlockSpec((1, D), const),
                pl.BlockSpec((D, D + 4 * KV_DIM), const),
                pl.BlockSpec((ts, LANES), tab), pl.BlockSpec((ts, LANES), tab),
                pl.BlockSpec((ts, LANES), tab), pl.BlockSpec((ts, LANES), tab),
            ],
            out_specs=(pl.BlockSpec((1, ts, D), tok), pl.BlockSpec((1, ts, D), tok),
                       pl.BlockSpec((1, ts, 2 * KV_DIM), tok), pl.BlockSpec((1, ts, 2 * KV_DIM), tok)),
            scratch_shapes=[pltpu.VMEM((2, ts, D), F32), pltpu.SemaphoreType.DMA((2,))],
        ),
        compiler_params=_cparams(("arbitrary", "arbitrary"), disable_bounds_checks=True),
        name="combine_qkv",
    )(pos, xs, ys, modp, mod, n1g, w_ext, cq, sq, ck, sk)


ATTN_QBLOCKS = 2


def _attn_kernel(q_ref, *rest, lc):
    nq = ATTN_QBLOCKS
    k_refs = rest[:nq + 2]
    v_refs = rest[nq + 2:2 * nq + 4]
    kx_ref, vx_ref, sink_ref = rest[2 * nq + 4:2 * nq + 7]
    cast_in = rest[2 * nq + 7:2 * nq + 7 + N_CAST]
    o_ref = rest[2 * nq + 7 + N_CAST]
    cast_out = rest[2 * nq + 8 + N_CAST:]
    _cast_blocks(cast_in, cast_out)
    i = pl.program_id(1)
    nsteps = pl.num_programs(1)
    nband = 3 * QBLK
    qrow = lax.broadcasted_iota(jnp.int32, (GROUP * QBLK, nband), 0) % QBLK
    kcol = lax.broadcasted_iota(jnp.int32, (GROUP * QBLK, nband), 1)
    lane = lax.broadcasted_iota(jnp.int32, (1, LANES), 1)
    low = lane < HEAD_DIM
    units = [(qb, h) for qb in range(nq) for h in range(N_KV_HEADS)]

    def band_ok(qb):
        lo_lim = jnp.where(i > 0, 0, QBLK) if qb == 0 else 0
        hi_lim = jnp.where(i < nsteps - 1, nband, 2 * QBLK - 1) if qb == nq - 1 else nband
        return (kcol >= jnp.maximum(qrow, lo_lim)) & (kcol <= jnp.minimum(qrow + 2 * WINDOW, hi_lim))

    ok = [band_ok(qb) for qb in range(nq)]

    def scores(unit):
        qb, h = unit
        hs = slice(h * LANES, (h + 1) * LANES)
        k_all = jnp.concatenate([r[0, :, hs] for r in k_refs[qb:qb + 3]] + [kx_ref[0, :, hs]], axis=0)
        qs = []
        for j in range(GROUP):
            c0 = h * GROUP * HEAD_DIM + (j // 2) * LANES
            qh = q_ref[0, qb * QBLK:(qb + 1) * QBLK, c0:c0 + LANES]
            keep = low if j % 2 == 0 else jnp.logical_not(low)
            qs.append(jnp.where(keep, qh, jnp.zeros_like(qh)))
        qm = jnp.concatenate(qs, axis=0)
        return lax.dot_general(qm, k_all, (((1,), (1,)), ((), ())), preferred_element_type=F32)

    def softmax(unit, s):
        qb, h = unit
        s = jnp.concatenate([jnp.where(ok[qb], s[:, :nband], NEG_INF), s[:, nband:]], axis=1)
        sink = LOG2E * jnp.concatenate(
            [jnp.broadcast_to(sink_ref[:, h * GROUP + j:h * GROUP + j + 1], (QBLK, 1)) for j in range(GROUP)], axis=0)
        nblk = s.shape[1] // LANES
        smax = s[:, :LANES]
        for kb in range(1, nblk):
            smax = jnp.maximum(smax, s[:, kb * LANES:(kb + 1) * LANES])
        m = jnp.maximum(jnp.max(smax, axis=1, keepdims=True), sink)
        return jnp.exp2(s - m).astype(BF16), jnp.exp2(sink - m)

    def weighted_values(unit, p, p_sink):
        qb, h = unit
        hs = slice(h * LANES, (h + 1) * LANES)
        v_all = jnp.concatenate([r[0, :, hs] for r in v_refs[qb:qb + 3]] + [vx_ref[0, :, hs]], axis=0)
        ones_lo = jnp.broadcast_to(jnp.where(lane == HEAD_DIM, 1.0, 0.0).astype(v_all.dtype), v_all.shape)
        ones_hi = jnp.broadcast_to(jnp.where(lane == 0, 1.0, 0.0).astype(v_all.dtype), v_all.shape)
        v_lo = jnp.where(low, v_all, ones_lo)
        v_hi = jnp.where(low, ones_hi, v_all)
        for pr in range(GROUP // 2):
            r0 = 2 * pr * QBLK
            o_lo = jnp.dot(p[r0:r0 + QBLK], v_lo, preferred_element_type=F32)
            o_hi = jnp.dot(p[r0 + QBLK:r0 + 2 * QBLK], v_hi, preferred_element_type=F32)
            inv_lo = 1.0 / (o_lo[:, HEAD_DIM:HEAD_DIM + 1] + p_sink[r0:r0 + QBLK])
            inv_hi = 1.0 / (o_hi[:, 0:1] + p_sink[r0 + QBLK:r0 + 2 * QBLK])
            c0 = h * GROUP * HEAD_DIM + pr * LANES
            o_ref[0, qb * QBLK:(qb + 1) * QBLK, c0:c0 + LANES] = (
                jnp.where(low, o_lo * inv_lo, o_hi * inv_hi).astype(BF16))

    n = len(units)
    s_all = [scores(units[0]), scores(units[1])]
    p_cur = softmax(units[0], s_all[0])
    for u in range(n):
        if u + 2 < n:
            s_all.append(scores(units[u + 2]))
        p_next = softmax(units[u + 1], s_all[u + 1]) if u + 1 < n else None
        weighted_values(units[u], *p_cur)
        p_cur = p_next


def _attention(q, kd, vd, kxd, vxd, sink, to_cast, cast_part):
    bsz, s, _ = q.shape
    lc = kxd.shape[1]
    nq = ATTN_QBLOCKS
    nb = s // QBLK
    assert nb % nq == 0
    tok = lambda b, i: (b, i, 0)
    ctx = lambda b, i: (b, 0, 0)
    kvw = 2 * KV_DIM
    band = [pl.BlockSpec((1, QBLK, kvw), lambda b, i, d=d: (b, jnp.clip(nq * i - 1 + d, 0, nb - 1), 0))
            for d in range(nq + 2)]
    nsteps = nb // nq
    cast_in, cast_out, cast_shapes = _cast_plumbing(to_cast, *cast_part, bsz * nsteps, lambda b, i: b * nsteps + i)
    return pl.pallas_call(
        functools.partial(_attn_kernel, lc=lc),
        out_shape=(jax.ShapeDtypeStruct((bsz, s, D), BF16), *cast_shapes),
        grid=(bsz, nsteps),
        in_specs=[pl.BlockSpec((1, nq * QBLK, D), tok)] + band + band + [
            pl.BlockSpec((1, lc, kvw), ctx), pl.BlockSpec((1, lc, kvw), ctx),
            pl.BlockSpec((1, N_HEADS), lambda b, i: (0, 0)),
        ] + cast_in,
        out_specs=(pl.BlockSpec((1, nq * QBLK, D), tok), *cast_out),
        compiler_params=_cparams(("arbitrary", "arbitrary")),
        name="window_attention",
    )(q, *([kd] * (nq + 2)), *([vd] * (nq + 2)), kxd, vxd, sink, *to_cast)


def _attn_out_kernel(a_ref, x_ref, mod_ref, n2g_ref, wo_ref, rwt_ref, rb_ref, x_out_ref, hrow_ref, cls_ref):
    mod = mod_ref[0]
    g1, sh2, sc2 = mod[2:3, :], mod[3:4, :], mod[4:5, :]
    gm2 = n2g_ref[...] * (1.0 + sc2)
    y = jnp.dot(a_ref[0], wo_ref[...], preferred_element_type=F32)
    x_out_ref[0] = _epilogue(x_ref[0], y, g1, sh2, gm2, rwt_ref, rb_ref, hrow_ref, cls_ref)


def _attn_out(attn, xs, mod, n2g, wo, rwt, rb, *, ts):
    bsz, s, _ = xs.shape
    nt = s // ts
    t_all = bsz * s
    const = lambda b, i: (0, 0)
    tok = lambda b, i: (b, i, 0)
    return pl.pallas_call(
        _attn_out_kernel,
        out_shape=(jax.ShapeDtypeStruct((bsz, s, D), F32),
                   jax.ShapeDtypeStruct((t_all, ROW_W), F32),
                   jax.ShapeDtypeStruct((1, t_all), jnp.int32)),
        grid=(bsz, nt),
        in_specs=[
            pl.BlockSpec((1, ts, D), tok), pl.BlockSpec((1, ts, D), tok),
            pl.BlockSpec((1, 6, D), lambda b, i: (b, 0, 0)),
            pl.BlockSpec((1, D), const), pl.BlockSpec((D, D), const),
            pl.BlockSpec((2, N_EXPERTS, D), lambda b, i: (0, 0, 0)), pl.BlockSpec((N_EXPERTS, 1), const),
        ],
        out_specs=(pl.BlockSpec((1, ts, D), tok),
                   pl.BlockSpec((ts, ROW_W), lambda b, i: (b * nt + i, 0)),
                   pl.BlockSpec((1, ts), lambda b, i: (0, b * nt + i))),
        compiler_params=_cparams(("arbitrary", "arbitrary")),
        name="attn_out",
    )(attn, xs, mod, n2g, wo, rwt, rb)


def _final_kernel(pos_ref, x_ref, ys_ref, mod_ref, fg_ref, o_ref, ybuf, sem, *, ts, seq):
    y = _combine_gather(pos_ref, ys_ref, ybuf, sem, ts=ts, tok_stride=seq, tok_off=0)
    x4 = x_ref[0] + mod_ref[0][5:6, :] * y
    r = lax.rsqrt(jnp.mean(x4 * x4, axis=-1, keepdims=True) + EPS)
    o_ref[0] = x4 * r * fg_ref[...]
    _combine_drain(ys_ref, ybuf, sem, ts)


def _final(pos, xs, ys, mod, fg, *, ts):
    bsz, s, _ = xs.shape
    tok = lambda b, i, p: (b, i, 0)
    return pl.pallas_call(
        functools.partial(_final_kernel, ts=ts, seq=s),
        out_shape=jax.ShapeDtypeStruct((bsz, s, D), F32),
        grid_spec=pltpu.PrefetchScalarGridSpec(
            num_scalar_prefetch=1,
            grid=(bsz, s // ts),
            in_specs=[pl.BlockSpec((1, ts, D), tok), pl.BlockSpec(memory_space=pl.ANY),
                      pl.BlockSpec((1, 6, D), lambda b, i, p: (b, 0, 0)), pl.BlockSpec((1, D), lambda b, i, p: (0, 0))],
            out_specs=pl.BlockSpec((1, ts, D), tok),
            scratch_shapes=[pltpu.VMEM((2, ts, D), F32), pltpu.SemaphoreType.DMA((2,))],
        ),
        compiler_params=_cparams(("arbitrary", "arbitrary"), disable_bounds_checks=True),
        name="combine_final",
    )(pos, xs, ys, mod, fg)


def _rope_tables(seq):
    rows = seq // GRID_W
    inv_freq = jnp.power(jnp.float32(ROPE_THETA), -jnp.arange(ROPE_PAIRS, dtype=F32) / ROPE_PAIRS)
    ang_r = jnp.arange(rows, dtype=jnp.int32).astype(F32)[:, None] * inv_freq
    ang_c = jnp.arange(GRID_W, dtype=jnp.int32).astype(F32)[:, None] * inv_freq
    by_row = lambda v: jnp.repeat(v, GRID_W, axis=0)
    by_col = lambda v: jnp.tile(v, (rows, 1))
    cos_r, sin_r = by_row(jnp.cos(ang_r)), by_row(jnp.sin(ang_r))
    cos_c, sin_c = by_col(jnp.cos(ang_c)), by_col(jnp.sin(ang_c))
    cos = jnp.concatenate([cos_r, cos_r, cos_c, cos_c], axis=1)
    sin = jnp.concatenate([-sin_r, sin_r, -sin_c, sin_c], axis=1)
    cos = jnp.concatenate([cos, cos], axis=1)
    sin = jnp.concatenate([sin, sin], axis=1)
    scale = HEAD_DIM ** -0.5 * LOG2E
    return cos * scale, sin * scale, cos, sin


def kernel(x, c, ctx, c_ctx, w_mod, b_mod, norm1_g, norm2_g, conv_w_pw1, conv_b_pw1, conv_w_dw, conv_b_dw,
           conv_ln_g, conv_ln_b, conv_w_pw2, conv_b_pw2, attn_w_qkv, attn_w_o, attn_sink, router_w, router_b,
           moe_w_gate, moe_w_up, moe_w_down, final_g):
    bsz, s, d = x.shape
    lc = ctx.shape[1]
    ts = 512
    tm = 256
    assert d == D and w_mod.shape[0] == 2 and bsz + 1 <= COND_ROWS
    assert lc <= ts and lc % 256 == 0 and s % ts == 0 and s % (ATTN_QBLOCKS * QBLK) == 0

    cond = jnp.concatenate([c, c_ctx[None, :], jnp.zeros((COND_ROWS - bsz - 1, D), F32)], axis=0)
    mods = _adaln(cond, w_mod, b_mod).reshape(2, COND_ROWS, 6, D)
    mod_lat = [mods[l, :bsz] for l in range(2)]
    mod_ctx = [mods[l, bsz:bsz + 1] for l in range(2)]

    rwt_hi = router_w.T.astype(BF16)
    rwt = jnp.stack([rwt_hi, (router_w.T - rwt_hi.astype(F32)).astype(BF16)])
    rb = router_b.reshape(N_EXPERTS, 1)
    row = lambda v: v.reshape(1, -1)
    experts_f32 = [moe_w_gate.reshape(-1, D_EXPERT), moe_w_up.reshape(-1, D_EXPERT), moe_w_down.reshape(-1, D)]
    as_experts = lambda ws: (ws[0].reshape(N_EXPERTS, D, D_EXPERT), ws[1].reshape(N_EXPERTS, D, D_EXPERT),
                             ws[2].reshape(N_EXPERTS, D_EXPERT, D))

    x1, ctx1, hrow, cls, *w_l0 = _conv_layer(
        x, ctx, mod_lat[0], mod_ctx[0], row(norm1_g[0]), row(norm2_g[0]), conv_w_pw1[0].astype(BF16),
        row(conv_b_pw1[0]), conv_w_dw[0], row(conv_b_dw[0]), row(conv_ln_g[0]), row(conv_ln_b[0]),
        conv_w_pw2[0].astype(BF16), row(conv_b_pw2[0]), rwt, rb, experts_f32, ts=ts, cast_part=(0, 2))
    ctx1 = ctx1[:, :lc]
    rank, counts = _plan(cls)
    pos, src, ea, eb, nused = _dispatch_tables(cls, rank, counts, tm=tm, n_pad=bsz * (ts - lc))
    ys = _moe(hrow, src, ea, eb, nused, *as_experts(w_l0), tm=tm)

    wq = attn_w_qkv[0][:, :D]
    wk = attn_w_qkv[0][:, D:D + KV_DIM].reshape(D, N_KV_HEADS, 1, HEAD_DIM)
    wv = attn_w_qkv[0][:, D + KV_DIM:].reshape(D, N_KV_HEADS, 1, HEAD_DIM)
    dup = lambda w: jnp.broadcast_to(w, (D, N_KV_HEADS, 2, HEAD_DIM)).reshape(D, 2 * KV_DIM)
    w_ext = jnp.concatenate([wq, dup(wk), dup(wv)], axis=1).astype(BF16)
    cq, sq, ck, sk = _rope_tables(s)
    ones = jnp.ones((lc, LANES), F32)
    zeros = jnp.zeros((lc, LANES), F32)
    n1g = row(norm1_g[1])
    x2, q, kd, vd = _qkv(pos, x1, ys, mod_lat[0], mod_lat[1], n1g, w_ext, cq, sq, ck, sk,
                         ts=ts, tok_stride=s + ts, tok_off=0)
    _, _, kxd, vxd = _qkv(pos, ctx1, ys, mod_ctx[0], mod_ctx[1], n1g, w_ext, ones, zeros, ones, zeros,
                          ts=lc, tok_stride=s + ts, tok_off=s)
    attn, *w_l1 = _attention(q, kd, vd, kxd, vxd, attn_sink[0].reshape(1, N_HEADS), experts_f32, (1, 2))
    x3, hrow1, cls1 = _attn_out(attn, x2, mod_lat[1], row(norm2_g[1]), attn_w_o[0].astype(BF16), rwt, rb, ts=ts)
    rank1, counts1 = _plan(cls1)
    pos1, src1, ea1, eb1, nused1 = _dispatch_tables(cls1, rank1, counts1, tm=tm)
    ys1 = _moe(hrow1, src1, ea1, eb1, nused1, *as_experts(w_l1), tm=tm)
    return _final(pos1, x3, ys1, mod_lat[1], row(final_g), ts=ts)
```

```python
import functools
import math

import jax
import jax.numpy as jnp
from jax import lax
from jax.experimental import pallas as pl
from jax.experimental.pallas import tpu as pltpu

D = 1024
GRID_W = 64
CONV_WIDTH = 31
CONV_PAD = CONV_WIDTH // 2
SUBLANES = 8
HALO = 16
HEAD_DIM = 64
N_HEADS = D // HEAD_DIM
N_KV_HEADS = N_HEADS // 4
GROUP = N_HEADS // N_KV_HEADS
KV_DIM = N_KV_HEADS * HEAD_DIM
WINDOW = 128
QBLK = 128
ROPE_THETA = 10000.0
ROPE_PAIRS = HEAD_DIM // 4
N_EXPERTS = 16
N_GROUPS = 4
EXPERTS_PER_GROUP = N_EXPERTS // N_GROUPS
N_PAIRS = 6
N_CLASSES = N_GROUPS * N_PAIRS
CLASS_ROWS = 32
NO_CLASS = CLASS_ROWS - 1
D_EXPERT = D // 2
EPS = 1e-6
NEG_INF = -1e30
LOG2E = 1.4426950408889634
LANES = 128
AUX = LANES
ROW_W = D + AUX
COND_ROWS = 16
VMEM_LIMIT = 56 * 1024 * 1024

F32 = jnp.float32
BF16 = jnp.bfloat16
HIGHEST = lax.Precision.HIGHEST


def _cparams(sem, **kw):
    return pltpu.CompilerParams(dimension_semantics=sem, vmem_limit_bytes=VMEM_LIMIT, **kw)


def _sigmoid(v):
    return jax.nn.sigmoid(v)


def _silu(v):
    return v * jax.nn.sigmoid(v)


def _rms_mod(xr, gm, sh):
    r = lax.rsqrt(jnp.mean(xr * xr, axis=-1, keepdims=True) + EPS)
    return xr * r * gm + sh


N_CAST = 3


def _cast_plumbing(arrays, part, n_parts, n_steps, step_of):
    in_specs, out_specs, out_shapes = [], [], []
    for a in arrays:
        rows, cols = a.shape[0] // n_parts, a.shape[1]
        blk = rows
        while (blk // 2) % 16 == 0 and (blk // 2) * n_steps >= rows:
            blk //= 2
        assert rows % blk == 0 and a.shape[0] % n_parts == 0
        nblk = rows // blk
        in_specs.append(pl.BlockSpec(
            (blk, cols), lambda *g, nblk=nblk: (part * nblk + jnp.minimum(step_of(*g), nblk - 1), 0)))
        out_specs.append(pl.BlockSpec((blk, cols), lambda *g, nblk=nblk: (jnp.minimum(step_of(*g), nblk - 1), 0)))
        out_shapes.append(jax.ShapeDtypeStruct((rows, cols), BF16))
    return in_specs, out_specs, out_shapes


def _cast_blocks(in_refs, out_refs):
    for src, dst in zip(in_refs, out_refs):
        dst[...] = src[...].astype(BF16)


def _adaln_kernel(cond_ref, w_ref, b_ref, out_ref):
    cnd = cond_ref[...]
    act = _silu(cnd)
    out_ref[0] = jnp.dot(act, w_ref[0], precision=HIGHEST, preferred_element_type=F32) + b_ref[0]


def _adaln(cond, w_mod, b_mod):
    depth = w_mod.shape[0]
    tn = 1536
    return pl.pallas_call(
        _adaln_kernel,
        out_shape=jax.ShapeDtypeStruct((depth, COND_ROWS, 6 * D), F32),
        grid=(depth, 6 * D // tn),
        in_specs=[
            pl.BlockSpec((COND_ROWS, D), lambda l, n: (0, 0)),
            pl.BlockSpec((1, D, tn), lambda l, n: (l, 0, n)),
            pl.BlockSpec((1, 1, tn), lambda l, n: (l, 0, n)),
        ],
        out_specs=pl.BlockSpec((1, COND_ROWS, tn), lambda l, n: (l, 0, n)),
        compiler_params=_cparams(("arbitrary", "arbitrary")),
        name="adaln",
    )(cond, w_mod, b_mod.reshape(depth, 1, 6 * D))


def _route_rows(h2, rwt_ref, rb_ref):
    nt_dims = (((1,), (1,)), ((), ()))
    h_hi = h2.astype(BF16)
    h_lo = (h2 - h_hi.astype(F32)).astype(BF16)
    w_both = rwt_ref[...]
    hh = lax.dot_general(w_both, h_hi, nt_dims, preferred_element_type=F32)
    lt = (hh[:N_EXPERTS] + hh[N_EXPERTS:]
          + lax.dot_general(w_both[:N_EXPERTS], h_lo, nt_dims, preferred_element_type=F32))
    s = _sigmoid(lt)
    sel = s + rb_ref[...]
    srow = [s[e:e + 1, :] for e in range(N_EXPERTS)]
    vrow = [sel[e:e + 1, :] for e in range(N_EXPERTS)]
    gscore = []
    for g in range(N_GROUPS):
        a, b, c, d = vrow[4 * g:4 * g + 4]
        gscore.append(jnp.maximum(jnp.maximum(jnp.maximum(a + b, a + c), jnp.maximum(a + d, b + c)),
                                  jnp.maximum(b + d, c + d)))
    bg = jnp.zeros_like(gscore[0], dtype=jnp.int32)
    best = gscore[0]
    for g in range(1, N_GROUPS):
        better = gscore[g] > best
        bg = jnp.where(better, g, bg)
        best = jnp.where(better, gscore[g], best)
    v = []
    sv = []
    for j in range(EXPERTS_PER_GROUP):
        vj = vrow[j]
        sj = srow[j]
        for g in range(1, N_GROUPS):
            vj = jnp.where(bg == g, vrow[4 * g + j], vj)
            sj = jnp.where(bg == g, srow[4 * g + j], sj)
        v.append(vj)
        sv.append(sj)
    i1 = jnp.zeros_like(bg)
    m1 = v[0]
    for j in range(1, EXPERTS_PER_GROUP):
        gt = v[j] > m1
        i1 = jnp.where(gt, j, i1)
        m1 = jnp.where(gt, v[j], m1)
    v2 = [jnp.where(i1 == j, -jnp.inf, v[j]) for j in range(EXPERTS_PER_GROUP)]
    i2 = jnp.zeros_like(bg)
    m2 = v2[0]
    for j in range(1, EXPERTS_PER_GROUP):
        gt = v2[j] > m2
        i2 = jnp.where(gt, j, i2)
        m2 = jnp.where(gt, v2[j], m2)
    lo = jnp.minimum(i1, i2)
    hi = jnp.maximum(i1, i2)
    s_lo = sv[0]
    s_hi = sv[0]
    for j in range(1, EXPERTS_PER_GROUP):
        s_lo = jnp.where(lo == j, sv[j], s_lo)
        s_hi = jnp.where(hi == j, sv[j], s_hi)
    den = s_lo + s_hi
    pair = jnp.where(lo == 0, 0, jnp.where(lo == 1, 3, 5)) + (hi - lo - 1)
    cls = N_PAIRS * bg + pair
    return s_lo / den, s_hi / den, cls


def _epilogue(x_in, y, g1, sh2, gm2, rwt_ref, rb_ref, hrow_ref, cls_ref, n_valid=None):
    x1 = x_in + g1 * y
    h2 = _rms_mod(x1, gm2, sh2)
    ga, gb, cls = _route_rows(h2, rwt_ref, rb_ref)
    ts = h2.shape[0]
    if n_valid is not None:
        cls = jnp.where(lax.broadcasted_iota(jnp.int32, cls.shape, 1) < n_valid, cls, NO_CLASS)
    cls_ref[...] = cls
    g8 = jnp.concatenate([ga, gb, jnp.zeros((6, ts), F32)], axis=0)
    g128 = jnp.concatenate([g8, jnp.zeros((AUX - 8, ts), F32)], axis=0)
    hrow_ref[:, :D] = h2
    hrow_ref[:, D:] = jnp.transpose(g128)
    return x1


def _conv_layer_kernel(x_ref, c_ref, xp_ref, xn_ref, modl_ref, modc_ref, n1g_ref, n2g_ref, w1_ref, b1_ref, wdw_ref,
                       bdw_ref, lng_ref, lnb_ref, w2_ref, b2_ref, rwt_ref, rb_ref, *rest, ts, n_ctx):
    cast_in = rest[:N_CAST]
    x_out_ref, c_out_ref, hrow_ref, cls_ref = rest[N_CAST:N_CAST + 4]
    cast_out = rest[N_CAST + 4:2 * N_CAST + 4]
    u_scr, cv_scr, v_scr = rest[2 * N_CAST + 4:]
    _cast_blocks(cast_in, cast_out)
    i = pl.program_id(1)
    nt = pl.num_programs(1) - 1
    is_ctx = jnp.full((1, 1), i, jnp.int32) == nt
    n_valid = jnp.where(i == nt, n_ctx, ts)
    mod = jnp.where(is_ctx, modc_ref[0], modl_ref[0])
    sh1, sc1, g1, sh2, sc2 = (mod[k:k + 1, :] for k in range(5))
    gm1 = n1g_ref[...] * (1.0 + sc1)
    gm2 = n2g_ref[...] * (1.0 + sc2)
    cw = 512

    def glu(hb, c0):
        a = jnp.dot(hb, w1_ref[:, c0:c0 + cw], preferred_element_type=F32) + b1_ref[:, c0:c0 + cw]
        g = jnp.dot(hb, w1_ref[:, D + c0:D + c0 + cw], preferred_element_type=F32) + b1_ref[:, D + c0:D + c0 + cw]
        return a * _sigmoid(g)

    ctx_tile = c_ref[0]
    if n_ctx < ts:
        ctx_tile = jnp.concatenate([ctx_tile, jnp.zeros((ts - n_ctx, D), F32)], axis=0)
    xm = jnp.where(is_ctx, ctx_tile, x_ref[0])
    hb = jnp.concatenate([_rms_mod(xp_ref[0, 0], gm1, sh1).astype(BF16), _rms_mod(xm, gm1, sh1).astype(BF16),
                          _rms_mod(xn_ref[0, 0], gm1, sh1).astype(BF16)], axis=0)
    keep_prev = jnp.logical_and(i > 0, i < nt).astype(F32)
    keep_next = (i < nt - 1).astype(F32)
    keep_row = (lax.broadcasted_iota(jnp.int32, (ts, 1), 0) < n_valid).astype(F32)
    for c0 in range(0, D, cw):
        u = glu(hb, c0)
        u_scr[0:HALO, c0:c0 + cw] = u[:HALO] * keep_prev
        u_scr[HALO:HALO + ts, c0:c0 + cw] = u[HALO:HALO + ts] * keep_row
        u_scr[HALO + ts:2 * HALO + ts, c0:c0 + cw] = u[HALO + ts:] * keep_next

    rr = 128
    taps_off = HALO - CONV_PAD

    def conv_chunk(rc, carry):
        r0 = pl.multiple_of(rc * rr, rr)
        for c0 in range(0, D, LANES):
            win = u_scr[pl.ds(r0, rr + 2 * HALO), c0:c0 + LANES]
            out = None
            for r in range(SUBLANES):
                nrow = rr + SUBLANES if r else rr
                part = None
                for q in range((CONV_WIDTH + taps_off) // SUBLANES + 1):
                    k = SUBLANES * q + r - taps_off
                    if 0 <= k < CONV_WIDTH:
                        term = win[SUBLANES * q:SUBLANES * q + nrow, :] * wdw_ref[k:k + 1, c0:c0 + LANES]
                        part = term if part is None else part + term
                shifted = part[r:r + rr, :]
                out = shifted if out is None else out + shifted
            cv_scr[pl.ds(r0, rr), c0:c0 + LANES] = out
        cv = cv_scr[pl.ds(r0, rr), :] + bdw_ref[...]
        mu = jnp.mean(cv, axis=-1, keepdims=True)
        dv = cv - mu
        var = jnp.mean(dv * dv, axis=-1, keepdims=True)
        ln = dv * lax.rsqrt(var + EPS) * lng_ref[...] + lnb_ref[...]
        v_scr[pl.ds(r0, rr), :] = _silu(ln).astype(BF16)
        return carry

    lax.fori_loop(0, ts // rr, conv_chunk, 0)
    y = jnp.dot(v_scr[...], w2_ref[...], preferred_element_type=F32) + b2_ref[...]
    x1 = _epilogue(xm, y, g1, sh2, gm2, rwt_ref, rb_ref, hrow_ref, cls_ref, n_valid=n_valid)

    @pl.when(i < nt)
    def _():
        x_out_ref[0] = x1

    @pl.when(i == nt)
    def _():
        c_out_ref[0] = x1[:n_ctx]


def _conv_layer(xs, cs, mod_lat, mod_ctx, n1g, n2g, w1, b1, wdw, bdw, lng, lnb, w2, b2, rwt, rb, to_cast, *, ts,
                cast_part):
    bsz, s, _ = xs.shape
    n_ctx = cs.shape[1]
    assert n_ctx <= ts and n_ctx % SUBLANES == 0
    nt = s // ts
    nh = s // HALO
    hpt = ts // HALO
    x4 = xs.reshape(bsz, nh, HALO, D)
    t_all = bsz * (s + ts)
    const = lambda b, i: (0, 0)
    lat = lambda b, i: (b, jnp.minimum(i, nt - 1), 0)
    row_blk = lambda b, i: (b * (nt + 1) + i, 0)
    cast_in, cast_out, cast_shapes = _cast_plumbing(to_cast, *cast_part, bsz * (nt + 1), lambda b, i: b * (nt + 1) + i)
    return pl.pallas_call(
        functools.partial(_conv_layer_kernel, ts=ts, n_ctx=n_ctx),
        out_shape=(jax.ShapeDtypeStruct((bsz, s, D), F32),
                   jax.ShapeDtypeStruct((bsz, n_ctx, D), F32),
                   jax.ShapeDtypeStruct((t_all, ROW_W), F32),
                   jax.ShapeDtypeStruct((1, t_all), jnp.int32), *cast_shapes),
        grid=(bsz, nt + 1),
        in_specs=[
            pl.BlockSpec((1, ts, D), lat),
            pl.BlockSpec((1, n_ctx, D), lambda b, i: (b, 0, 0)),
            pl.BlockSpec((1, 1, HALO, D), lambda b, i: (b, jnp.clip(i * hpt - 1, 0, nh - 1), 0, 0)),
            pl.BlockSpec((1, 1, HALO, D), lambda b, i: (b, jnp.minimum((i + 1) * hpt, nh - 1), 0, 0)),
            pl.BlockSpec((1, 6, D), lambda b, i: (b, 0, 0)),
            pl.BlockSpec((1, 6, D), lambda b, i: (0, 0, 0)),
            pl.BlockSpec((1, D), const), pl.BlockSpec((1, D), const),
            pl.BlockSpec((D, 2 * D), const), pl.BlockSpec((1, 2 * D), const),
            pl.BlockSpec((CONV_WIDTH, D), const), pl.BlockSpec((1, D), const),
            pl.BlockSpec((1, D), const), pl.BlockSpec((1, D), const),
            pl.BlockSpec((D, D), const), pl.BlockSpec((1, D), const),
            pl.BlockSpec((2 * N_EXPERTS, D), const), pl.BlockSpec((N_EXPERTS, 1), const),
        ] + cast_in,
        out_specs=(pl.BlockSpec((1, ts, D), lat),
                   pl.BlockSpec((1, n_ctx, D), lambda b, i: (b, 0, 0)),
                   pl.BlockSpec((ts, ROW_W), row_blk),
                   pl.BlockSpec((1, ts), lambda b, i: (0, b * (nt + 1) + i)), *cast_out),
        scratch_shapes=[pltpu.VMEM((ts + 2 * HALO, D), F32), pltpu.VMEM((ts, D), F32), pltpu.VMEM((ts, D), BF16)],
        compiler_params=_cparams(("arbitrary", "arbitrary")),
        name="conv_layer",
    )(xs, cs, x4, x4, mod_lat, mod_ctx, n1g, n2g, w1, b1, wdw, bdw, lng, lnb, w2, b2, rwt, rb, *to_cast)


def _plan_kernel(cls_ref, rank_ref, cnt_ref, carry_scr, *, tb):
    @pl.when(pl.program_id(0) == 0)
    def _():
        carry_scr[...] = jnp.zeros_like(carry_scr)

    cls = cls_ref[...]
    onehot = (lax.broadcasted_iota(jnp.int32, (CLASS_ROWS, tb), 0) == cls).astype(F32)
    upper = (lax.broadcasted_iota(jnp.int32, (tb, tb), 0) <= lax.broadcasted_iota(jnp.int32, (tb, tb), 1))
    prefix = jnp.dot(onehot.astype(BF16), upper.astype(BF16), preferred_element_type=F32)
    carry = carry_scr[:, 0:1]
    rank = jnp.sum(onehot * (prefix + carry), axis=0, keepdims=True) - 1.0
    rank_ref[...] = rank.astype(jnp.int32)
    total = carry + jnp.sum(onehot, axis=1, keepdims=True)
    carry_scr[...] = jnp.broadcast_to(total, carry_scr.shape)
    cnt_ref[...] = jnp.broadcast_to(total, cnt_ref.shape)


def _plan(cls):
    t_all = cls.shape[1]
    tb = math.gcd(t_all, 1024)
    return pl.pallas_call(
        functools.partial(_plan_kernel, tb=tb),
        out_shape=(jax.ShapeDtypeStruct((1, t_all), jnp.int32), jax.ShapeDtypeStruct((CLASS_ROWS, LANES), F32)),
        grid=(t_all // tb,),
        in_specs=[pl.BlockSpec((1, tb), lambda i: (0, i))],
        out_specs=(pl.BlockSpec((1, tb), lambda i: (0, i)), pl.BlockSpec((CLASS_ROWS, LANES), lambda i: (0, 0))),
        scratch_shapes=[pltpu.VMEM((CLASS_ROWS, LANES), F32)],
        compiler_params=_cparams(("arbitrary",)),
        name="route_plan",
    )(cls)


def _invert_kernel(pos_ref, gap_ref, src_ref, *, n_tok, n_gap):
    def zero(i, carry):
        src_ref[i] = 0
        return carry
    for g in range(n_gap):
        lax.fori_loop(gap_ref[2 * g], gap_ref[2 * g + 1], zero, 0)

    unroll = 16

    def put(i, carry):
        for u in range(unroll):
            t = i * unroll + u
            src_ref[pos_ref[t]] = t
        return carry
    lax.fori_loop(0, n_tok // unroll, put, 0)


def _invert(pos, gaps, n_slot):
    n_tok = pos.shape[0]
    return pl.pallas_call(
        functools.partial(_invert_kernel, n_tok=n_tok, n_gap=gaps.shape[0] // 2),
        out_shape=jax.ShapeDtypeStruct((n_slot,), jnp.int32),
        in_specs=[pl.BlockSpec(memory_space=pltpu.SMEM), pl.BlockSpec(memory_space=pltpu.SMEM)],
        out_specs=pl.BlockSpec(memory_space=pltpu.SMEM),
        name="invert_slots",
    )(pos, gaps)


def _dispatch_tables(cls, rank, counts, *, tm, n_pad=0):
    t_all = cls.shape[1]
    n_tiles = -(-((t_all - n_pad) // tm + N_CLASSES) // MOE_TILES_PER_STEP) * MOE_TILES_PER_STEP
    cnt = counts[:N_CLASSES, 0].astype(jnp.int32)
    padded = ((cnt + tm - 1) // tm) * tm
    ends = jnp.cumsum(padded)
    offs = ends - padded
    first_slot = jnp.zeros((CLASS_ROWS,), jnp.int32).at[:N_CLASSES].set(offs).at[NO_CLASS].set(n_tiles * tm)
    pos = first_slot[cls[0]] + rank[0]
    gap_lo = jnp.concatenate([offs + cnt, ends[-1:]])
    gap_hi = jnp.concatenate([ends, jnp.full((1,), n_tiles * tm, jnp.int32)])
    src = _invert(pos, jnp.stack([gap_lo, gap_hi], axis=1).reshape(-1).astype(jnp.int32), n_tiles * tm + n_pad)
    n_used = ends[-1] // tm
    tile_start = jnp.minimum(jnp.arange(n_tiles, dtype=jnp.int32), n_used - 1) * tm
    tile_cls = jnp.minimum(jnp.sum(ends[None, :] <= tile_start[:, None], axis=1), N_CLASSES - 1).astype(jnp.int32)
    pair_lo = jnp.array([0, 0, 0, 1, 1, 2], jnp.int32)
    pair_hi = jnp.array([1, 2, 3, 2, 3, 3], jnp.int32)
    grp = tile_cls // N_PAIRS
    ea = EXPERTS_PER_GROUP * grp + pair_lo[tile_cls % N_PAIRS]
    eb = EXPERTS_PER_GROUP * grp + pair_hi[tile_cls % N_PAIRS]
    return pos, src, ea, eb, n_used.reshape(1).astype(jnp.int32)


def _start_row_gather(idx_ref, idx_base, src_hbm, dst, sem, n, unroll):
    def one(r):
        row = idx_ref[idx_base + r]
        pltpu.make_async_copy(src_hbm.at[pl.ds(row, 1), :], dst.at[pl.ds(r, 1), :], sem).start()

    if unroll:
        for r in range(n):
            one(r)
    else:
        def body(r, carry):
            one(r)
            return carry
        lax.fori_loop(0, n, body, 0)


def _wait_row_gather(src_hbm, dst, sem, n):
    pltpu.make_async_copy(src_hbm.at[pl.ds(0, n), :], dst, sem).wait()


MOE_TILES_PER_STEP = 2


def _moe_kernel(src_ref, ea_ref, eb_ref, nused_ref, hrow_ref, *rest, tm):
    k = MOE_TILES_PER_STEP
    weights = [rest[6 * i:6 * i + 6] for i in range(k)]
    ys_ref, gbuf, sem = rest[6 * k:]
    j = pl.program_id(0)
    nused = nused_ref[0]

    def gather(tile, slot, unroll):
        @pl.when(tile < nused)
        def _():
            _start_row_gather(src_ref, tile * tm, hrow_ref, gbuf.at[slot], sem.at[slot], tm, unroll=unroll)

    @pl.when(j == 0)
    def _():
        for t in range(2 * k):
            gather(t, t, unroll=False)

    for i in range(k):
        tile = j * k + i
        slot = (j % 2) * k + i
        wga, wua, wda, wgb, wub, wdb = weights[i]
        out = ys_ref.at[pl.ds(i * tm, tm), :]

        @pl.when(tile < nused)
        def _():
            _wait_row_gather(hrow_ref, gbuf.at[slot], sem.at[slot], tm)
            h = gbuf[slot, :, :D].astype(BF16)
            ga = gbuf[slot, :, D:D + 1]
            gb = gbuf[slot, :, D + 1:D + 2]

            def expert(wg, wu, wd):
                a = jnp.dot(h, wg[0], preferred_element_type=F32)
                u = jnp.dot(h, wu[0], preferred_element_type=F32)
                act = (_silu(a) * u).astype(BF16)
                return jnp.dot(act, wd[0], preferred_element_type=F32)

            out[...] = ga * expert(wga, wua, wda) + gb * expert(wgb, wub, wdb)

        @pl.when(tile >= nused)
        def _():
            out[...] = jnp.zeros((tm, D), F32)

        gather(tile + 2 * k, slot, unroll=True)


def _moe(hrow, src, ea, eb, nused, wg, wu, wd, *, tm):
    k = MOE_TILES_PER_STEP
    n_tiles = ea.shape[0]
    assert n_tiles % k == 0
    f = wg.shape[-1]
    w_specs = []
    for i in range(k):
        wa = lambda j, src, ea, eb, nu, i=i: (ea[j * k + i], 0, 0)
        wb = lambda j, src, ea, eb, nu, i=i: (eb[j * k + i], 0, 0)
        w_specs += [pl.BlockSpec((1, D, f), wa), pl.BlockSpec((1, D, f), wa), pl.BlockSpec((1, f, D), wa),
                    pl.BlockSpec((1, D, f), wb), pl.BlockSpec((1, D, f), wb), pl.BlockSpec((1, f, D), wb)]
    return pl.pallas_call(
        functools.partial(_moe_kernel, tm=tm),
        out_shape=jax.ShapeDtypeStruct((n_tiles * tm, D), F32),
        grid_spec=pltpu.PrefetchScalarGridSpec(
            num_scalar_prefetch=4,
            grid=(n_tiles // k,),
            in_specs=[pl.BlockSpec(memory_space=pl.ANY)] + w_specs,
            out_specs=pl.BlockSpec((k * tm, D), lambda j, src, ea, eb, nu: (j, 0)),
            scratch_shapes=[pltpu.VMEM((2 * k, tm, ROW_W), F32), pltpu.SemaphoreType.DMA((2 * k,))],
        ),
        compiler_params=_cparams(("arbitrary",), disable_bounds_checks=True),
        name="moe_experts",
    )(src, ea, eb, nused, hrow, *([wg, wu, wd, wg, wu, wd] * k))


def _combine_base(step, *, ts, tok_stride, tok_off):
    nt = pl.num_programs(1)
    return (step // nt) * tok_stride + tok_off + (step % nt) * ts


def _combine_gather(pos_ref, ys_ref, ybuf, sem, *, ts, tok_stride, tok_off):
    nt = pl.num_programs(1)
    n = pl.program_id(0) * nt + pl.program_id(1)

    @pl.when(n == 0)
    def _():
        _start_row_gather(pos_ref, _combine_base(0, ts=ts, tok_stride=tok_stride, tok_off=tok_off), ys_ref,
                          ybuf.at[0], sem.at[0], ts, unroll=False)

    nxt = jnp.minimum(n + 1, pl.num_programs(0) * nt - 1)
    _start_row_gather(pos_ref, _combine_base(nxt, ts=ts, tok_stride=tok_stride, tok_off=tok_off), ys_ref,
                      ybuf.at[1 - n % 2], sem.at[1 - n % 2], ts, unroll=True)
    _wait_row_gather(ys_ref, ybuf.at[n % 2], sem.at[n % 2], ts)
    return ybuf[n % 2]


def _combine_drain(ys_ref, ybuf, sem, ts):
    nt = pl.num_programs(1)
    n = pl.program_id(0) * nt + pl.program_id(1)

    @pl.when(n == pl.num_programs(0) * nt - 1)
    def _():
        _wait_row_gather(ys_ref, ybuf.at[1 - n % 2], sem.at[1 - n % 2], ts)


def _rope(v, cos, sin):
    lane = lax.broadcasted_iota(jnp.int32, v.shape, 1)
    first = (lane % (2 * ROPE_PAIRS)) < ROPE_PAIRS
    partner = jnp.where(first, pltpu.roll(v, LANES - ROPE_PAIRS, 1), pltpu.roll(v, ROPE_PAIRS, 1))
    return v * cos + partner * sin


def _qkv_kernel(pos_ref, x_ref, ys_ref, modp_ref, mod_ref, n1g_ref, w_ref, cq_ref, sq_ref, ck_ref, sk_ref,
                x_out_ref, q_ref, kd_ref, vd_ref, ybuf, sem, *, ts, seq, tok_stride, tok_off):
    y = _combine_gather(pos_ref, ys_ref, ybuf, sem, ts=ts, tok_stride=tok_stride, tok_off=tok_off)
    g2_prev = modp_ref[0][5:6, :]
    x2 = x_ref[0] + g2_prev * y
    x_out_ref[0] = x2
    mod = mod_ref[0]
    sh1 = mod[0:1, :]
    gm1 = n1g_ref[...] * (1.0 + mod[1:2, :])
    hb = _rms_mod(x2, gm1, sh1).astype(BF16)
    cq, sq, ck, sk = cq_ref[...], sq_ref[...], ck_ref[...], sk_ref[...]
    nw = 2 * LANES
    for c0 in range(0, D, nw):
        qc = jnp.dot(hb, w_ref[:, c0:c0 + nw], preferred_element_type=F32)
        for l0 in range(0, nw, LANES):
            q_ref[0, :, c0 + l0:c0 + l0 + LANES] = _rope(qc[:, l0:l0 + LANES], cq, sq).astype(BF16)
    for c0 in range(0, 2 * KV_DIM, nw):
        kc = jnp.dot(hb, w_ref[:, D + c0:D + c0 + nw], preferred_element_type=F32)
        for l0 in range(0, nw, LANES):
            kd_ref[0, :, c0 + l0:c0 + l0 + LANES] = _rope(kc[:, l0:l0 + LANES], ck, sk).astype(BF16)
    vd_ref[0] = jnp.dot(hb, w_ref[:, D + 2 * KV_DIM:], preferred_element_type=F32).astype(BF16)
    _combine_drain(ys_ref, ybuf, sem, ts)


def _qkv(pos, xs, ys, modp, mod, n1g, w_ext, cq, sq, ck, sk, *, ts, tok_stride, tok_off):
    bsz, s, _ = xs.shape
    nt = s // ts
    per_batch = mod.shape[0] > 1
    mod_map = (lambda b, i, p: (b, 0, 0)) if per_batch else (lambda b, i, p: (0, 0, 0))
    const = lambda b, i, p: (0, 0)
    tok = lambda b, i, p: (b, i, 0)
    tab = lambda b, i, p: (i, 0)
    return pl.pallas_call(
        functools.partial(_qkv_kernel, ts=ts, seq=s, tok_stride=tok_stride, tok_off=tok_off),
        out_shape=(jax.ShapeDtypeStruct((bsz, s, D), F32), jax.ShapeDtypeStruct((bsz, s, D), BF16),
                   jax.ShapeDtypeStruct((bsz, s, 2 * KV_DIM), BF16), jax.ShapeDtypeStruct((bsz, s, 2 * KV_DIM), BF16)),
        grid_spec=pltpu.PrefetchScalarGridSpec(
            num_scalar_prefetch=1,
            grid=(bsz, nt),
            in_specs=[
                pl.BlockSpec((1, ts, D), tok),
                pl.BlockSpec(memory_space=pl.ANY),
                pl.BlockSpec((1, 6, D), mod_map), pl.BlockSpec((1, 6, D), mod_map),
                pl.BlockSpec((1, D), const),
                pl.BlockSpec((D, D + 4 * KV_DIM), const),
                pl.BlockSpec((ts, LANES), tab), pl.BlockSpec((ts, LANES), tab),
                pl.BlockSpec((ts, LANES), tab), pl.BlockSpec((ts, LANES), tab),
            ],
            out_specs=(pl.BlockSpec((1, ts, D), tok), pl.BlockSpec((1, ts, D), tok),
                       pl.BlockSpec((1, ts, 2 * KV_DIM), tok), pl.BlockSpec((1, ts, 2 * KV_DIM), tok)),
            scratch_shapes=[pltpu.VMEM((2, ts, D), F32), pltpu.SemaphoreType.DMA((2,))],
        ),
        compiler_params=_cparams(("arbitrary", "arbitrary"), disable_bounds_checks=True),
        name="combine_qkv",
    )(pos, xs, ys, modp, mod, n1g, w_ext, cq, sq, ck, sk)


ATTN_QBLOCKS = 4


def _attn_kernel(q_ref, *rest, lc):
    nq = ATTN_QBLOCKS
    k_refs = rest[:nq + 2]
    v_refs = rest[nq + 2:2 * nq + 4]
    kx_ref, vx_ref, sink_ref = rest[2 * nq + 4:2 * nq + 7]
    cast_in = rest[2 * nq + 7:2 * nq + 7 + N_CAST]
    o_ref = rest[2 * nq + 7 + N_CAST]
    cast_out = rest[2 * nq + 8 + N_CAST:]
    _cast_blocks(cast_in, cast_out)
    i = pl.program_id(1)
    nsteps = pl.num_programs(1)
    nband = 3 * QBLK
    qrow = lax.broadcasted_iota(jnp.int32, (GROUP * QBLK, nband), 0) % QBLK
    kcol = lax.broadcasted_iota(jnp.int32, (GROUP * QBLK, nband), 1)
    lane = lax.broadcasted_iota(jnp.int32, (1, LANES), 1)
    low = lane < HEAD_DIM
    units = [(qb, h) for qb in range(nq) for h in range(N_KV_HEADS)]

    def band_ok(qb):
        lo_lim = jnp.where(i > 0, 0, QBLK) if qb == 0 else 0
        hi_lim = jnp.where(i < nsteps - 1, nband, 2 * QBLK - 1) if qb == nq - 1 else nband
        return (kcol >= jnp.maximum(qrow, lo_lim)) & (kcol <= jnp.minimum(qrow + 2 * WINDOW, hi_lim))

    ok = [band_ok(qb) for qb in range(nq)]

    def scores(unit):
        qb, h = unit
        hs = slice(h * LANES, (h + 1) * LANES)
        k_all = jnp.concatenate([r[0, :, hs] for r in k_refs[qb:qb + 3]] + [kx_ref[0, :, hs]], axis=0)
        qs = []
        for j in range(GROUP):
            c0 = h * GROUP * HEAD_DIM + (j // 2) * LANES
            qh = q_ref[0, qb * QBLK:(qb + 1) * QBLK, c0:c0 + LANES]
            keep = low if j % 2 == 0 else jnp.logical_not(low)
            qs.append(jnp.where(keep, qh, jnp.zeros_like(qh)))
        qm = jnp.concatenate(qs, axis=0)
        return lax.dot_general(qm, k_all, (((1,), (1,)), ((), ())), preferred_element_type=F32)

    def softmax(unit, s):
        qb, h = unit
        s = jnp.concatenate([jnp.where(ok[qb], s[:, :nband], NEG_INF), s[:, nband:]], axis=1)
        sink = LOG2E * jnp.concatenate(
            [jnp.broadcast_to(sink_ref[:, h * GROUP + j:h * GROUP + j + 1], (QBLK, 1)) for j in range(GROUP)], axis=0)
        nblk = s.shape[1] // LANES
        smax = s[:, :LANES]
        for kb in range(1, nblk):
            smax = jnp.maximum(smax, s[:, kb * LANES:(kb + 1) * LANES])
        m = jnp.maximum(jnp.max(smax, axis=1, keepdims=True), sink)
        return jnp.exp2(s - m).astype(BF16), jnp.exp2(sink - m)

    def weighted_values(unit, p, p_sink):
        qb, h = unit
        hs = slice(h * LANES, (h + 1) * LANES)
        v_all = jnp.concatenate([r[0, :, hs] for r in v_refs[qb:qb + 3]] + [vx_ref[0, :, hs]], axis=0)
        ones_lo = jnp.broadcast_to(jnp.where(lane == HEAD_DIM, 1.0, 0.0).astype(v_all.dtype), v_all.shape)
        ones_hi = jnp.broadcast_to(jnp.where(lane == 0, 1.0, 0.0).astype(v_all.dtype), v_all.shape)
        v_lo = jnp.where(low, v_all, ones_lo)
        v_hi = jnp.where(low, ones_hi, v_all)
        for pr in range(GROUP // 2):
            r0 = 2 * pr * QBLK
            o_lo = jnp.dot(p[r0:r0 + QBLK], v_lo, preferred_element_type=F32)
            o_hi = jnp.dot(p[r0 + QBLK:r0 + 2 * QBLK], v_hi, preferred_element_type=F32)
            inv_lo = 1.0 / (o_lo[:, HEAD_DIM:HEAD_DIM + 1] + p_sink[r0:r0 + QBLK])
            inv_hi = 1.0 / (o_hi[:, 0:1] + p_sink[r0 + QBLK:r0 + 2 * QBLK])
            c0 = h * GROUP * HEAD_DIM + pr * LANES
            o_ref[0, qb * QBLK:(qb + 1) * QBLK, c0:c0 + LANES] = (
                jnp.where(low, o_lo * inv_lo, o_hi * inv_hi).astype(BF16))

    n = len(units)
    s_all = [scores(units[0]), scores(units[1])]
    p_cur = softmax(units[0], s_all[0])
    for u in range(n):
        if u + 2 < n:
            s_all.append(scores(units[u + 2]))
        p_next = softmax(units[u + 1], s_all[u + 1]) if u + 1 < n else None
        weighted_values(units[u], *p_cur)
        p_cur = p_next


def _attention(q, kd, vd, kxd, vxd, sink, to_cast, cast_part):
    bsz, s, _ = q.shape
    lc = kxd.shape[1]
    nq = ATTN_QBLOCKS
    nb = s // QBLK
    assert nb % nq == 0
    tok = lambda b, i: (b, i, 0)
    ctx = lambda b, i: (b, 0, 0)
    kvw = 2 * KV_DIM
    band = [pl.BlockSpec((1, QBLK, kvw), lambda b, i, d=d: (b, jnp.clip(nq * i - 1 + d, 0, nb - 1), 0))
            for d in range(nq + 2)]
    nsteps = nb // nq
    cast_in, cast_out, cast_shapes = _cast_plumbing(to_cast, *cast_part, bsz * nsteps, lambda b, i: b * nsteps + i)
    return pl.pallas_call(
        functools.partial(_attn_kernel, lc=lc),
        out_shape=(jax.ShapeDtypeStruct((bsz, s, D), BF16), *cast_shapes),
        grid=(bsz, nsteps),
        in_specs=[pl.BlockSpec((1, nq * QBLK, D), tok)] + band + band + [
            pl.BlockSpec((1, lc, kvw), ctx), pl.BlockSpec((1, lc, kvw), ctx),
            pl.BlockSpec((1, N_HEADS), lambda b, i: (0, 0)),
        ] + cast_in,
        out_specs=(pl.BlockSpec((1, nq * QBLK, D), tok), *cast_out),
        compiler_params=_cparams(("arbitrary", "arbitrary")),
        name="window_attention",
    )(q, *([kd] * (nq + 2)), *([vd] * (nq + 2)), kxd, vxd, sink, *to_cast)


def _attn_out_kernel(a_ref, x_ref, mod_ref, n2g_ref, wo_ref, rwt_ref, rb_ref, x_out_ref, hrow_ref, cls_ref):
    mod = mod_ref[0]
    g1, sh2, sc2 = mod[2:3, :], mod[3:4, :], mod[4:5, :]
    gm2 = n2g_ref[...] * (1.0 + sc2)
    y = jnp.dot(a_ref[0], wo_ref[...], preferred_element_type=F32)
    x_out_ref[0] = _epilogue(x_ref[0], y, g1, sh2, gm2, rwt_ref, rb_ref, hrow_ref, cls_ref)


def _attn_out(attn, xs, mod, n2g, wo, rwt, rb, *, ts):
    bsz, s, _ = xs.shape
    nt = s // ts
    t_all = bsz * s
    const = lambda b, i: (0, 0)
    tok = lambda b, i: (b, i, 0)
    return pl.pallas_call(
        _attn_out_kernel,
        out_shape=(jax.ShapeDtypeStruct((bsz, s, D), F32),
                   jax.ShapeDtypeStruct((t_all, ROW_W), F32),
                   jax.ShapeDtypeStruct((1, t_all), jnp.int32)),
        grid=(bsz, nt),
        in_specs=[
            pl.BlockSpec((1, ts, D), tok), pl.BlockSpec((1, ts, D), tok),
            pl.BlockSpec((1, 6, D), lambda b, i: (b, 0, 0)),
            pl.BlockSpec((1, D), const), pl.BlockSpec((D, D), const),
            pl.BlockSpec((2 * N_EXPERTS, D), const), pl.BlockSpec((N_EXPERTS, 1), const),
        ],
        out_specs=(pl.BlockSpec((1, ts, D), tok),
                   pl.BlockSpec((ts, ROW_W), lambda b, i: (b * nt + i, 0)),
                   pl.BlockSpec((1, ts), lambda b, i: (0, b * nt + i))),
        compiler_params=_cparams(("arbitrary", "arbitrary")),
        name="attn_out",
    )(attn, xs, mod, n2g, wo, rwt, rb)


def _final_kernel(pos_ref, x_ref, ys_ref, mod_ref, fg_ref, o_ref, ybuf, sem, *, ts, seq):
    y = _combine_gather(pos_ref, ys_ref, ybuf, sem, ts=ts, tok_stride=seq, tok_off=0)
    x4 = x_ref[0] + mod_ref[0][5:6, :] * y
    r = lax.rsqrt(jnp.mean(x4 * x4, axis=-1, keepdims=True) + EPS)
    o_ref[0] = x4 * r * fg_ref[...]
    _combine_drain(ys_ref, ybuf, sem, ts)


def _final(pos, xs, ys, mod, fg, *, ts):
    bsz, s, _ = xs.shape
    tok = lambda b, i, p: (b, i, 0)
    return pl.pallas_call(
        functools.partial(_final_kernel, ts=ts, seq=s),
        out_shape=jax.ShapeDtypeStruct((bsz, s, D), F32),
        grid_spec=pltpu.PrefetchScalarGridSpec(
            num_scalar_prefetch=1,
            grid=(bsz, s // ts),
            in_specs=[pl.BlockSpec((1, ts, D), tok), pl.BlockSpec(memory_space=pl.ANY),
                      pl.BlockSpec((1, 6, D), lambda b, i, p: (b, 0, 0)), pl.BlockSpec((1, D), lambda b, i, p: (0, 0))],
            out_specs=pl.BlockSpec((1, ts, D), tok),
            scratch_shapes=[pltpu.VMEM((2, ts, D), F32), pltpu.SemaphoreType.DMA((2,))],
        ),
        compiler_params=_cparams(("arbitrary", "arbitrary"), disable_bounds_checks=True),
        name="combine_final",
    )(pos, xs, ys, mod, fg)


def _rope_tables(seq):
    rows = seq // GRID_W
    inv_freq = jnp.power(jnp.float32(ROPE_THETA), -jnp.arange(ROPE_PAIRS, dtype=F32) / ROPE_PAIRS)
    ang_r = jnp.arange(rows, dtype=jnp.int32).astype(F32)[:, None] * inv_freq
    ang_c = jnp.arange(GRID_W, dtype=jnp.int32).astype(F32)[:, None] * inv_freq
    by_row = lambda v: jnp.repeat(v, GRID_W, axis=0)
    by_col = lambda v: jnp.tile(v, (rows, 1))
    cos_r, sin_r = by_row(jnp.cos(ang_r)), by_row(jnp.sin(ang_r))
    cos_c, sin_c = by_col(jnp.cos(ang_c)), by_col(jnp.sin(ang_c))
    cos = jnp.concatenate([cos_r, cos_r, cos_c, cos_c], axis=1)
    sin = jnp.concatenate([-sin_r, sin_r, -sin_c, sin_c], axis=1)
    cos = jnp.concatenate([cos, cos], axis=1)
    sin = jnp.concatenate([sin, sin], axis=1)
    scale = HEAD_DIM ** -0.5 * LOG2E
    return cos * scale, sin * scale, cos, sin


def kernel(x, c, ctx, c_ctx, w_mod, b_mod, norm1_g, norm2_g, conv_w_pw1, conv_b_pw1, conv_w_dw, conv_b_dw,
           conv_ln_g, conv_ln_b, conv_w_pw2, conv_b_pw2, attn_w_qkv, attn_w_o, attn_sink, router_w, router_b,
           moe_w_gate, moe_w_up, moe_w_down, final_g):
    bsz, s, d = x.shape
    lc = ctx.shape[1]
    ts = 512
    tm = 256
    assert d == D and w_mod.shape[0] == 2 and bsz + 1 <= COND_ROWS
    assert lc <= ts and lc % 256 == 0 and s % ts == 0 and s % (ATTN_QBLOCKS * QBLK) == 0

    cond = jnp.concatenate([c, c_ctx[None, :], jnp.zeros((COND_ROWS - bsz - 1, D), F32)], axis=0)
    mods = _adaln(cond, w_mod, b_mod).reshape(2, COND_ROWS, 6, D)
    mod_lat = [mods[l, :bsz] for l in range(2)]
    mod_ctx = [mods[l, bsz:bsz + 1] for l in range(2)]

    rwt_hi = router_w.T.astype(BF16)
    rwt = jnp.concatenate([rwt_hi, (router_w.T - rwt_hi.astype(F32)).astype(BF16)], axis=0)
    rb = router_b.reshape(N_EXPERTS, 1)
    row = lambda v: v.reshape(1, -1)
    experts_f32 = [moe_w_gate.reshape(-1, D_EXPERT), moe_w_up.reshape(-1, D_EXPERT), moe_w_down.reshape(-1, D)]
    as_experts = lambda ws: (ws[0].reshape(N_EXPERTS, D, D_EXPERT), ws[1].reshape(N_EXPERTS, D, D_EXPERT),
                             ws[2].reshape(N_EXPERTS, D_EXPERT, D))

    x1, ctx1, hrow, cls, *w_l0 = _conv_layer(
        x, ctx, mod_lat[0], mod_ctx[0], row(norm1_g[0]), row(norm2_g[0]), conv_w_pw1[0].astype(BF16),
        row(conv_b_pw1[0]), conv_w_dw[0], row(conv_b_dw[0]), row(conv_ln_g[0]), row(conv_ln_b[0]),
        conv_w_pw2[0].astype(BF16), row(conv_b_pw2[0]), rwt, rb, experts_f32, ts=ts, cast_part=(0, 2))
    rank, counts = _plan(cls)
    pos, src, ea, eb, nused = _dispatch_tables(cls, rank, counts, tm=tm, n_pad=bsz * (ts - lc))
    ys = _moe(hrow, src, ea, eb, nused, *as_experts(w_l0), tm=tm)

    wq = attn_w_qkv[0][:, :D]
    wk = attn_w_qkv[0][:, D:D + KV_DIM].reshape(D, N_KV_HEADS, 1, HEAD_DIM)
    wv = attn_w_qkv[0][:, D + KV_DIM:].reshape(D, N_KV_HEADS, 1, HEAD_DIM)
    dup = lambda w: jnp.broadcast_to(w, (D, N_KV_HEADS, 2, HEAD_DIM)).reshape(D, 2 * KV_DIM)
    w_ext = jnp.concatenate([wq, dup(wk), dup(wv)], axis=1).astype(BF16)
    cq, sq, ck, sk = _rope_tables(s)
    ones = jnp.ones((lc, LANES), F32)
    zeros = jnp.zeros((lc, LANES), F32)
    n1g = row(norm1_g[1])
    x2, q, kd, vd = _qkv(pos, x1, ys, mod_lat[0], mod_lat[1], n1g, w_ext, cq, sq, ck, sk,
                         ts=ts, tok_stride=s + ts, tok_off=0)
    _, _, kxd, vxd = _qkv(pos, ctx1, ys, mod_ctx[0], mod_ctx[1], n1g, w_ext, ones, zeros, ones, zeros,
                          ts=lc, tok_stride=s + ts, tok_off=s)
    attn, *w_l1 = _attention(q, kd, vd, kxd, vxd, attn_sink[0].reshape(1, N_HEADS), experts_f32, (1, 2))
    x3, hrow1, cls1 = _attn_out(attn, x2, mod_lat[1], row(norm2_g[1]), attn_w_o[0].astype(BF16), rwt, rb, ts=ts)
    rank1, counts1 = _plan(cls1)
    pos1, src1, ea1, eb1, nused1 = _dispatch_tables(cls1, rank1, counts1, tm=tm)
    ys1 = _moe(hrow1, src1, ea1, eb1, nused1, *as_experts(w_l1), tm=tm)
    return _final(pos1, x3, ys1, mod_lat[1], row(final_g), ts=ts)
```

```python
import functools
import math

import jax
import jax.numpy as jnp
from jax import lax
from jax.experimental import pallas as pl
from jax.experimental.pallas import tpu as pltpu

D = 1024
GRID_W = 64
CONV_WIDTH = 31
CONV_PAD = CONV_WIDTH // 2
SUBLANES = 8
HALO = 16
HEAD_DIM = 64
N_HEADS = D // HEAD_DIM
N_KV_HEADS = N_HEADS // 4
GROUP = N_HEADS // N_KV_HEADS
KV_DIM = N_KV_HEADS * HEAD_DIM
WINDOW = 128
QBLK = 128
ROPE_THETA = 10000.0
ROPE_PAIRS = HEAD_DIM // 4
N_EXPERTS = 16
N_GROUPS = 4
EXPERTS_PER_GROUP = N_EXPERTS // N_GROUPS
N_PAIRS = 6
N_CLASSES = N_GROUPS * N_PAIRS
CLASS_ROWS = 32
NO_CLASS = CLASS_ROWS - 1
D_EXPERT = D // 2
EPS = 1e-6
NEG_INF = -1e30
LOG2E = 1.4426950408889634
LANES = 128
AUX = LANES
ROW_W = D + AUX
COND_ROWS = 16
VMEM_LIMIT = 56 * 1024 * 1024

F32 = jnp.float32
BF16 = jnp.bfloat16
HIGHEST = lax.Precision.HIGHEST


def _cparams(sem, **kw):
    return pltpu.CompilerParams(dimension_semantics=sem, vmem_limit_bytes=VMEM_LIMIT, **kw)


def _sigmoid(v):
    return jax.nn.sigmoid(v)


def _silu(v):
    return v * jax.nn.sigmoid(v)


def _rms_mod(xr, gm, sh):
    r = lax.rsqrt(jnp.mean(xr * xr, axis=-1, keepdims=True) + EPS)
    return xr * r * gm + sh


N_CAST = 3


def _cast_plumbing(arrays, part, n_parts, n_steps, step_of):
    in_specs, out_specs, out_shapes = [], [], []
    for a in arrays:
        rows, cols = a.shape[0] // n_parts, a.shape[1]
        blk = rows
        while (blk // 2) % 16 == 0 and (blk // 2) * n_steps >= rows:
            blk //= 2
        assert rows % blk == 0 and a.shape[0] % n_parts == 0
        nblk = rows // blk
        in_specs.append(pl.BlockSpec(
            (blk, cols), lambda *g, nblk=nblk: (part * nblk + jnp.minimum(step_of(*g), nblk - 1), 0)))
        out_specs.append(pl.BlockSpec((blk, cols), lambda *g, nblk=nblk: (jnp.minimum(step_of(*g), nblk - 1), 0)))
        out_shapes.append(jax.ShapeDtypeStruct((rows, cols), BF16))
    return in_specs, out_specs, out_shapes


def _cast_blocks(in_refs, out_refs):
    for src, dst in zip(in_refs, out_refs):
        dst[...] = src[...].astype(BF16)


def _adaln_kernel(cond_ref, w_ref, b_ref, out_ref):
    cnd = cond_ref[...]
    act = _silu(cnd)
    out_ref[0] = jnp.dot(act, w_ref[0], precision=HIGHEST, preferred_element_type=F32) + b_ref[0]


def _adaln(cond, w_mod, b_mod):
    depth = w_mod.shape[0]
    tn = 1536
    return pl.pallas_call(
        _adaln_kernel,
        out_shape=jax.ShapeDtypeStruct((depth, COND_ROWS, 6 * D), F32),
        grid=(depth, 6 * D // tn),
        in_specs=[
            pl.BlockSpec((COND_ROWS, D), lambda l, n: (0, 0)),
            pl.BlockSpec((1, D, tn), lambda l, n: (l, 0, n)),
            pl.BlockSpec((1, 1, tn), lambda l, n: (l, 0, n)),
        ],
        out_specs=pl.BlockSpec((1, COND_ROWS, tn), lambda l, n: (l, 0, n)),
        compiler_params=_cparams(("arbitrary", "arbitrary")),
        name="adaln",
    )(cond, w_mod, b_mod.reshape(depth, 1, 6 * D))


def _route_rows(h2, rwt_ref, rb_ref):
    nt_dims = (((1,), (1,)), ((), ()))
    h_hi = h2.astype(BF16)
    h_lo = (h2 - h_hi.astype(F32)).astype(BF16)
    w_both = rwt_ref[...]
    hh = lax.dot_general(w_both, h_hi, nt_dims, preferred_element_type=F32)
    lt = (hh[:N_EXPERTS] + hh[N_EXPERTS:]
          + lax.dot_general(w_both[:N_EXPERTS], h_lo, nt_dims, preferred_element_type=F32))
    s = _sigmoid(lt)
    sel = s + rb_ref[...]
    srow = [s[e:e + 1, :] for e in range(N_EXPERTS)]
    vrow = [sel[e:e + 1, :] for e in range(N_EXPERTS)]
    gscore = []
    for g in range(N_GROUPS):
        a, b, c, d = vrow[4 * g:4 * g + 4]
        gscore.append(jnp.maximum(jnp.maximum(jnp.maximum(a + b, a + c), jnp.maximum(a + d, b + c)),
                                  jnp.maximum(b + d, c + d)))
    bg = jnp.zeros_like(gscore[0], dtype=jnp.int32)
    best = gscore[0]
    for g in range(1, N_GROUPS):
        better = gscore[g] > best
        bg = jnp.where(better, g, bg)
        best = jnp.where(better, gscore[g], best)
    v = []
    sv = []
    for j in range(EXPERTS_PER_GROUP):
        vj = vrow[j]
        sj = srow[j]
        for g in range(1, N_GROUPS):
            vj = jnp.where(bg == g, vrow[4 * g + j], vj)
            sj = jnp.where(bg == g, srow[4 * g + j], sj)
        v.append(vj)
        sv.append(sj)
    i1 = jnp.zeros_like(bg)
    m1 = v[0]
    for j in range(1, EXPERTS_PER_GROUP):
        gt = v[j] > m1
        i1 = jnp.where(gt, j, i1)
        m1 = jnp.where(gt, v[j], m1)
    v2 = [jnp.where(i1 == j, -jnp.inf, v[j]) for j in range(EXPERTS_PER_GROUP)]
    i2 = jnp.zeros_like(bg)
    m2 = v2[0]
    for j in range(1, EXPERTS_PER_GROUP):
        gt = v2[j] > m2
        i2 = jnp.where(gt, j, i2)
        m2 = jnp.where(gt, v2[j], m2)
    lo = jnp.minimum(i1, i2)
    hi = jnp.maximum(i1, i2)
    s_lo = sv[0]
    s_hi = sv[0]
    for j in range(1, EXPERTS_PER_GROUP):
        s_lo = jnp.where(lo == j, sv[j], s_lo)
        s_hi = jnp.where(hi == j, sv[j], s_hi)
    den = s_lo + s_hi
    pair = jnp.where(lo == 0, 0, jnp.where(lo == 1, 3, 5)) + (hi - lo - 1)
    cls = N_PAIRS * bg + pair
    return s_lo / den, s_hi / den, cls


def _epilogue(x_in, y, g1, sh2, gm2, rwt_ref, rb_ref, hrow_ref, cls_ref, n_valid=None):
    x1 = x_in + g1 * y
    h2 = _rms_mod(x1, gm2, sh2)
    ga, gb, cls = _route_rows(h2, rwt_ref, rb_ref)
    ts = h2.shape[0]
    if n_valid is not None:
        cls = jnp.where(lax.broadcasted_iota(jnp.int32, cls.shape, 1) < n_valid, cls, NO_CLASS)
    cls_ref[...] = cls
    g8 = jnp.concatenate([ga, gb, jnp.zeros((6, ts), F32)], axis=0)
    g128 = jnp.concatenate([g8, jnp.zeros((AUX - 8, ts), F32)], axis=0)
    hrow_ref[:, :D] = h2
    hrow_ref[:, D:] = jnp.transpose(g128)
    return x1


def _conv_layer_kernel(x_ref, c_ref, xp_ref, xn_ref, modl_ref, modc_ref, n1g_ref, n2g_ref, w1_ref, b1_ref, wdw_ref,
                       bdw_ref, lng_ref, lnb_ref, w2_ref, b2_ref, rwt_ref, rb_ref, *rest, ts, n_ctx):
    cast_in = rest[:N_CAST]
    x_out_ref, c_out_ref, hrow_ref, cls_ref = rest[N_CAST:N_CAST + 4]
    cast_out = rest[N_CAST + 4:2 * N_CAST + 4]
    u_scr, cv_scr, v_scr = rest[2 * N_CAST + 4:]
    _cast_blocks(cast_in, cast_out)
    i = pl.program_id(1)
    nt = pl.num_programs(1) - 1
    is_ctx = jnp.full((1, 1), i, jnp.int32) == nt
    n_valid = jnp.where(i == nt, n_ctx, ts)
    mod = jnp.where(is_ctx, modc_ref[0], modl_ref[0])
    sh1, sc1, g1, sh2, sc2 = (mod[k:k + 1, :] for k in range(5))
    gm1 = n1g_ref[...] * (1.0 + sc1)
    gm2 = n2g_ref[...] * (1.0 + sc2)
    cw = 256

    def glu(hb, c0):
        a = jnp.dot(hb, w1_ref[:, c0:c0 + cw], preferred_element_type=F32) + b1_ref[:, c0:c0 + cw]
        g = jnp.dot(hb, w1_ref[:, D + c0:D + c0 + cw], preferred_element_type=F32) + b1_ref[:, D + c0:D + c0 + cw]
        return a * _sigmoid(g)

    ctx_tile = c_ref[0]
    if n_ctx < ts:
        ctx_tile = jnp.concatenate([ctx_tile, jnp.zeros((ts - n_ctx, D), F32)], axis=0)
    xm = jnp.where(is_ctx, ctx_tile, x_ref[0])
    hb = jnp.concatenate([_rms_mod(xp_ref[0, 0], gm1, sh1).astype(BF16), _rms_mod(xm, gm1, sh1).astype(BF16),
                          _rms_mod(xn_ref[0, 0], gm1, sh1).astype(BF16)], axis=0)
    keep_prev = jnp.logical_and(i > 0, i < nt).astype(F32)
    keep_next = (i < nt - 1).astype(F32)
    keep_row = (lax.broadcasted_iota(jnp.int32, (ts, 1), 0) < n_valid).astype(F32)
    for c0 in range(0, D, cw):
        u = glu(hb, c0)
        u_scr[0:HALO, c0:c0 + cw] = u[:HALO] * keep_prev
        u_scr[HALO:HALO + ts, c0:c0 + cw] = u[HALO:HALO + ts] * keep_row
        u_scr[HALO + ts:2 * HALO + ts, c0:c0 + cw] = u[HALO + ts:] * keep_next

    rr = 128
    taps_off = HALO - CONV_PAD

    def conv_chunk(rc, carry):
        r0 = pl.multiple_of(rc * rr, rr)
        for c0 in range(0, D, LANES):
            win = u_scr[pl.ds(r0, rr + 2 * HALO), c0:c0 + LANES]
            out = None
            for r in range(SUBLANES):
                nrow = rr + SUBLANES if r else rr
                part = None
                for q in range((CONV_WIDTH + taps_off) // SUBLANES + 1):
                    k = SUBLANES * q + r - taps_off
                    if 0 <= k < CONV_WIDTH:
                        term = win[SUBLANES * q:SUBLANES * q + nrow, :] * wdw_ref[k:k + 1, c0:c0 + LANES]
                        part = term if part is None else part + term
                shifted = part[r:r + rr, :]
                out = shifted if out is None else out + shifted
            cv_scr[pl.ds(r0, rr), c0:c0 + LANES] = out
        cv = cv_scr[pl.ds(r0, rr), :] + bdw_ref[...]
        mu = jnp.mean(cv, axis=-1, keepdims=True)
        dv = cv - mu
        var = jnp.mean(dv * dv, axis=-1, keepdims=True)
        ln = dv * lax.rsqrt(var + EPS) * lng_ref[...] + lnb_ref[...]
        v_scr[pl.ds(r0, rr), :] = _silu(ln).astype(BF16)
        return carry

    lax.fori_loop(0, ts // rr, conv_chunk, 0)
    y = jnp.dot(v_scr[...], w2_ref[...], preferred_element_type=F32) + b2_ref[...]
    x1 = _epilogue(xm, y, g1, sh2, gm2, rwt_ref, rb_ref, hrow_ref, cls_ref, n_valid=n_valid)

    @pl.when(i < nt)
    def _():
        x_out_ref[0] = x1

    @pl.when(i == nt)
    def _():
        c_out_ref[0] = x1[:n_ctx]


def _conv_layer(xs, cs, mod_lat, mod_ctx, n1g, n2g, w1, b1, wdw, bdw, lng, lnb, w2, b2, rwt, rb, to_cast, *, ts,
                cast_part):
    bsz, s, _ = xs.shape
    n_ctx = cs.shape[1]
    assert n_ctx <= ts and n_ctx % SUBLANES == 0
    nt = s // ts
    nh = s // HALO
    hpt = ts // HALO
    x4 = xs.reshape(bsz, nh, HALO, D)
    t_all = bsz * (s + ts)
    const = lambda b, i: (0, 0)
    lat = lambda b, i: (b, jnp.minimum(i, nt - 1), 0)
    row_blk = lambda b, i: (b * (nt + 1) + i, 0)
    cast_in, cast_out, cast_shapes = _cast_plumbing(to_cast, *cast_part, bsz * (nt + 1), lambda b, i: b * (nt + 1) + i)
    return pl.pallas_call(
        functools.partial(_conv_layer_kernel, ts=ts, n_ctx=n_ctx),
        out_shape=(jax.ShapeDtypeStruct((bsz, s, D), F32),
                   jax.ShapeDtypeStruct((bsz, n_ctx, D), F32),
                   jax.ShapeDtypeStruct((t_all, ROW_W), F32),
                   jax.ShapeDtypeStruct((1, t_all), jnp.int32), *cast_shapes),
        grid=(bsz, nt + 1),
        in_specs=[
            pl.BlockSpec((1, ts, D), lat),
            pl.BlockSpec((1, n_ctx, D), lambda b, i: (b, 0, 0)),
            pl.BlockSpec((1, 1, HALO, D), lambda b, i: (b, jnp.clip(i * hpt - 1, 0, nh - 1), 0, 0)),
            pl.BlockSpec((1, 1, HALO, D), lambda b, i: (b, jnp.minimum((i + 1) * hpt, nh - 1), 0, 0)),
            pl.BlockSpec((1, 6, D), lambda b, i: (b, 0, 0)),
            pl.BlockSpec((1, 6, D), lambda b, i: (0, 0, 0)),
            pl.BlockSpec((1, D), const), pl.BlockSpec((1, D), const),
            pl.BlockSpec((D, 2 * D), const), pl.BlockSpec((1, 2 * D), const),
            pl.BlockSpec((CONV_WIDTH, D), const), pl.BlockSpec((1, D), const),
            pl.BlockSpec((1, D), const), pl.BlockSpec((1, D), const),
            pl.BlockSpec((D, D), const), pl.BlockSpec((1, D), const),
            pl.BlockSpec((2 * N_EXPERTS, D), const), pl.BlockSpec((N_EXPERTS, 1), const),
        ] + cast_in,
        out_specs=(pl.BlockSpec((1, ts, D), lat),
                   pl.BlockSpec((1, n_ctx, D), lambda b, i: (b, 0, 0)),
                   pl.BlockSpec((ts, ROW_W), row_blk),
                   pl.BlockSpec((1, ts), lambda b, i: (0, b * (nt + 1) + i)), *cast_out),
        scratch_shapes=[pltpu.VMEM((ts + 2 * HALO, D), F32), pltpu.VMEM((ts, D), F32), pltpu.VMEM((ts, D), BF16)],
        compiler_params=_cparams(("arbitrary", "arbitrary")),
        name="conv_layer",
    )(xs, cs, x4, x4, mod_lat, mod_ctx, n1g, n2g, w1, b1, wdw, bdw, lng, lnb, w2, b2, rwt, rb, *to_cast)


def _plan_kernel(cls_ref, rank_ref, cnt_ref, carry_scr, *, tb):
    @pl.when(pl.program_id(0) == 0)
    def _():
        carry_scr[...] = jnp.zeros_like(carry_scr)

    cls = cls_ref[...]
    onehot = (lax.broadcasted_iota(jnp.int32, (CLASS_ROWS, tb), 0) == cls).astype(F32)
    upper = (lax.broadcasted_iota(jnp.int32, (tb, tb), 0) <= lax.broadcasted_iota(jnp.int32, (tb, tb), 1))
    prefix = jnp.dot(onehot.astype(BF16), upper.astype(BF16), preferred_element_type=F32)
    carry = carry_scr[:, 0:1]
    rank = jnp.sum(onehot * (prefix + carry), axis=0, keepdims=True) - 1.0
    rank_ref[...] = rank.astype(jnp.int32)
    total = carry + jnp.sum(onehot, axis=1, keepdims=True)
    carry_scr[...] = jnp.broadcast_to(total, carry_scr.shape)
    cnt_ref[...] = jnp.broadcast_to(total, cnt_ref.shape)


def _plan(cls):
    t_all = cls.shape[1]
    tb = math.gcd(t_all, 1024)
    return pl.pallas_call(
        functools.partial(_plan_kernel, tb=tb),
        out_shape=(jax.ShapeDtypeStruct((1, t_all), jnp.int32), jax.ShapeDtypeStruct((CLASS_ROWS, LANES), F32)),
        grid=(t_all // tb,),
        in_specs=[pl.BlockSpec((1, tb), lambda i: (0, i))],
        out_specs=(pl.BlockSpec((1, tb), lambda i: (0, i)), pl.BlockSpec((CLASS_ROWS, LANES), lambda i: (0, 0))),
        scratch_shapes=[pltpu.VMEM((CLASS_ROWS, LANES), F32)],
        compiler_params=_cparams(("arbitrary",)),
        name="route_plan",
    )(cls)


def _invert_kernel(pos_ref, gap_ref, src_ref, *, n_tok, n_gap):
    def zero(i, carry):
        src_ref[i] = 0
        return carry
    for g in range(n_gap):
        lax.fori_loop(gap_ref[2 * g], gap_ref[2 * g + 1], zero, 0)

    unroll = 16

    def put(i, carry):
        for u in range(unroll):
            t = i * unroll + u
            src_ref[pos_ref[t]] = t
        return carry
    lax.fori_loop(0, n_tok // unroll, put, 0)


def _invert(pos, gaps, n_slot):
    n_tok = pos.shape[0]
    return pl.pallas_call(
        functools.partial(_invert_kernel, n_tok=n_tok, n_gap=gaps.shape[0] // 2),
        out_shape=jax.ShapeDtypeStruct((n_slot,), jnp.int32),
        in_specs=[pl.BlockSpec(memory_space=pltpu.SMEM), pl.BlockSpec(memory_space=pltpu.SMEM)],
        out_specs=pl.BlockSpec(memory_space=pltpu.SMEM),
        name="invert_slots",
    )(pos, gaps)


def _dispatch_tables(cls, rank, counts, *, tm, n_pad=0):
    t_all = cls.shape[1]
    n_tiles = -(-((t_all - n_pad) // tm + N_CLASSES) // MOE_TILES_PER_STEP) * MOE_TILES_PER_STEP
    cnt = counts[:N_CLASSES, 0].astype(jnp.int32)
    padded = ((cnt + tm - 1) // tm) * tm
    ends = jnp.cumsum(padded)
    offs = ends - padded
    first_slot = jnp.zeros((CLASS_ROWS,), jnp.int32).at[:N_CLASSES].set(offs).at[NO_CLASS].set(n_tiles * tm)
    pos = first_slot[cls[0]] + rank[0]
    gap_lo = jnp.concatenate([offs + cnt, ends[-1:]])
    gap_hi = jnp.concatenate([ends, jnp.full((1,), n_tiles * tm, jnp.int32)])
    src = _invert(pos, jnp.stack([gap_lo, gap_hi], axis=1).reshape(-1).astype(jnp.int32), n_tiles * tm + n_pad)
    n_used = ends[-1] // tm
    tile_start = jnp.minimum(jnp.arange(n_tiles, dtype=jnp.int32), n_used - 1) * tm
    tile_cls = jnp.minimum(jnp.sum(ends[None, :] <= tile_start[:, None], axis=1), N_CLASSES - 1).astype(jnp.int32)
    pair_lo = jnp.array([0, 0, 0, 1, 1, 2], jnp.int32)
    pair_hi = jnp.array([1, 2, 3, 2, 3, 3], jnp.int32)
    grp = tile_cls // N_PAIRS
    ea = EXPERTS_PER_GROUP * grp + pair_lo[tile_cls % N_PAIRS]
    eb = EXPERTS_PER_GROUP * grp + pair_hi[tile_cls % N_PAIRS]
    return pos, src, ea, eb, n_used.reshape(1).astype(jnp.int32)


def _start_row_gather(idx_ref, idx_base, src_hbm, dst, sem, n, unroll):
    def one(r):
        row = idx_ref[idx_base + r]
        pltpu.make_async_copy(src_hbm.at[pl.ds(row, 1), :], dst.at[pl.ds(r, 1), :], sem).start()

    if unroll:
        for r in range(n):
            one(r)
    else:
        def body(r, carry):
            one(r)
            return carry
        lax.fori_loop(0, n, body, 0)


def _wait_row_gather(src_hbm, dst, sem, n):
    pltpu.make_async_copy(src_hbm.at[pl.ds(0, n), :], dst, sem).wait()


MOE_TILES_PER_STEP = 2


def _moe_kernel(src_ref, ea_ref, eb_ref, nused_ref, hrow_ref, *rest, tm):
    k = MOE_TILES_PER_STEP
    weights = [rest[6 * i:6 * i + 6] for i in range(k)]
    ys_ref, gbuf, sem = rest[6 * k:]
    j = pl.program_id(0)
    nused = nused_ref[0]

    def gather(tile, slot, unroll):
        @pl.when(tile < nused)
        def _():
            _start_row_gather(src_ref, tile * tm, hrow_ref, gbuf.at[slot], sem.at[slot], tm, unroll=unroll)

    @pl.when(j == 0)
    def _():
        for t in range(2 * k):
            gather(t, t, unroll=False)

    for i in range(k):
        tile = j * k + i
        slot = (j % 2) * k + i
        wga, wua, wda, wgb, wub, wdb = weights[i]
        out = ys_ref.at[pl.ds(i * tm, tm), :]

        @pl.when(tile < nused)
        def _():
            _wait_row_gather(hrow_ref, gbuf.at[slot], sem.at[slot], tm)
            h = gbuf[slot, :, :D].astype(BF16)
            ga = gbuf[slot, :, D:D + 1]
            gb = gbuf[slot, :, D + 1:D + 2]

            def expert(wg, wu, wd):
                a = jnp.dot(h, wg[0], preferred_element_type=F32)
                u = jnp.dot(h, wu[0], preferred_element_type=F32)
                act = (_silu(a) * u).astype(BF16)
                return jnp.dot(act, wd[0], preferred_element_type=F32)

            out[...] = ga * expert(wga, wua, wda) + gb * expert(wgb, wub, wdb)

        @pl.when(tile >= nused)
        def _():
            out[...] = jnp.zeros((tm, D), F32)

        gather(tile + 2 * k, slot, unroll=True)


def _moe(hrow, src, ea, eb, nused, wg, wu, wd, *, tm):
    k = MOE_TILES_PER_STEP
    n_tiles = ea.shape[0]
    assert n_tiles % k == 0
    f = wg.shape[-1]
    w_specs = []
    for i in range(k):
        wa = lambda j, src, ea, eb, nu, i=i: (ea[j * k + i], 0, 0)
        wb = lambda j, src, ea, eb, nu, i=i: (eb[j * k + i], 0, 0)
        w_specs += [pl.BlockSpec((1, D, f), wa), pl.BlockSpec((1, D, f), wa), pl.BlockSpec((1, f, D), wa),
                    pl.BlockSpec((1, D, f), wb), pl.BlockSpec((1, D, f), wb), pl.BlockSpec((1, f, D), wb)]
    return pl.pallas_call(
        functools.partial(_moe_kernel, tm=tm),
        out_shape=jax.ShapeDtypeStruct((n_tiles * tm, D), F32),
        grid_spec=pltpu.PrefetchScalarGridSpec(
            num_scalar_prefetch=4,
            grid=(n_tiles // k,),
            in_specs=[pl.BlockSpec(memory_space=pl.ANY)] + w_specs,
            out_specs=pl.BlockSpec((k * tm, D), lambda j, src, ea, eb, nu: (j, 0)),
            scratch_shapes=[pltpu.VMEM((2 * k, tm, ROW_W), F32), pltpu.SemaphoreType.DMA((2 * k,))],
        ),
        compiler_params=_cparams(("arbitrary",), disable_bounds_checks=True),
        name="moe_experts",
    )(src, ea, eb, nused, hrow, *([wg, wu, wd, wg, wu, wd] * k))


def _combine_base(step, *, ts, tok_stride, tok_off):
    nt = pl.num_programs(1)
    return (step // nt) * tok_stride + tok_off + (step % nt) * ts


def _combine_gather(pos_ref, ys_ref, ybuf, sem, *, ts, tok_stride, tok_off):
    nt = pl.num_programs(1)
    n = pl.program_id(0) * nt + pl.program_id(1)

    @pl.when(n == 0)
    def _():
        _start_row_gather(pos_ref, _combine_base(0, ts=ts, tok_stride=tok_stride, tok_off=tok_off), ys_ref,
                          ybuf.at[0], sem.at[0], ts, unroll=False)

    nxt = jnp.minimum(n + 1, pl.num_programs(0) * nt - 1)
    _start_row_gather(pos_ref, _combine_base(nxt, ts=ts, tok_stride=tok_stride, tok_off=tok_off), ys_ref,
                      ybuf.at[1 - n % 2], sem.at[1 - n % 2], ts, unroll=True)
    _wait_row_gather(ys_ref, ybuf.at[n % 2], sem.at[n % 2], ts)
    return ybuf[n % 2]


def _combine_drain(ys_ref, ybuf, sem, ts):
    nt = pl.num_programs(1)
    n = pl.program_id(0) * nt + pl.program_id(1)

    @pl.when(n == pl.num_programs(0) * nt - 1)
    def _():
        _wait_row_gather(ys_ref, ybuf.at[1 - n % 2], sem.at[1 - n % 2], ts)


def _rope(v, cos, sin):
    lane = lax.broadcasted_iota(jnp.int32, v.shape, 1)
    first = (lane % (2 * ROPE_PAIRS)) < ROPE_PAIRS
    partner = jnp.where(first, pltpu.roll(v, LANES - ROPE_PAIRS, 1), pltpu.roll(v, ROPE_PAIRS, 1))
    return v * cos + partner * sin


def _qkv_kernel(pos_ref, x_ref, ys_ref, modp_ref, mod_ref, n1g_ref, w_ref, cq_ref, sq_ref, ck_ref, sk_ref,
                x_out_ref, q_ref, kd_ref, vd_ref, ybuf, sem, *, ts, seq, tok_stride, tok_off):
    y = _combine_gather(pos_ref, ys_ref, ybuf, sem, ts=ts, tok_stride=tok_stride, tok_off=tok_off)
    g2_prev = modp_ref[0][5:6, :]
    x2 = x_ref[0] + g2_prev * y
    x_out_ref[0] = x2
    mod = mod_ref[0]
    sh1 = mod[0:1, :]
    gm1 = n1g_ref[...] * (1.0 + mod[1:2, :])
    hb = _rms_mod(x2, gm1, sh1).astype(BF16)
    cq, sq, ck, sk = cq_ref[...], sq_ref[...], ck_ref[...], sk_ref[...]
    nw = 2 * LANES
    for c0 in range(0, D, nw):
        qc = jnp.dot(hb, w_ref[:, c0:c0 + nw], preferred_element_type=F32)
        for l0 in range(0, nw, LANES):
            q_ref[0, :, c0 + l0:c0 + l0 + LANES] = _rope(qc[:, l0:l0 + LANES], cq, sq).astype(BF16)
    for c0 in range(0, 2 * KV_DIM, nw):
        kc = jnp.dot(hb, w_ref[:, D + c0:D + c0 + nw], preferred_element_type=F32)
        for l0 in range(0, nw, LANES):
            kd_ref[0, :, c0 + l0:c0 + l0 + LANES] = _rope(kc[:, l0:l0 + LANES], ck, sk).astype(BF16)
    vd_ref[0] = jnp.dot(hb, w_ref[:, D + 2 * KV_DIM:], preferred_element_type=F32).astype(BF16)
    _combine_drain(ys_ref, ybuf, sem, ts)


def _qkv(pos, xs, ys, modp, mod, n1g, w_ext, cq, sq, ck, sk, *, ts, tok_stride, tok_off):
    bsz, s, _ = xs.shape
    nt = s // ts
    per_batch = mod.shape[0] > 1
    mod_map = (lambda b, i, p: (b, 0, 0)) if per_batch else (lambda b, i, p: (0, 0, 0))
    const = lambda b, i, p: (0, 0)
    tok = lambda b, i, p: (b, i, 0)
    tab = lambda b, i, p: (i, 0)
    return pl.pallas_call(
        functools.partial(_qkv_kernel, ts=ts, seq=s, tok_stride=tok_stride, tok_off=tok_off),
        out_shape=(jax.ShapeDtypeStruct((bsz, s, D), F32), jax.ShapeDtypeStruct((bsz, s, D), BF16),
                   jax.ShapeDtypeStruct((bsz, s, 2 * KV_DIM), BF16), jax.ShapeDtypeStruct((bsz, s, 2 * KV_DIM), BF16)),
        grid_spec=pltpu.PrefetchScalarGridSpec(
            num_scalar_prefetch=1,
            grid=(bsz, nt),
            in_specs=[
                pl.BlockSpec((1, ts, D), tok),
                pl.BlockSpec(memory_space=pl.ANY),
                pl.BlockSpec((1, 6, D), mod_map), pl.BlockSpec((1, 6, D), mod_map),
                pl.BlockSpec((1, D), const),
                pl.BlockSpec((D, D + 4 * KV_DIM), const),
                pl.BlockSpec((ts, LANES), tab), pl.BlockSpec((ts, LANES), tab),
                pl.BlockSpec((ts, LANES), tab), pl.BlockSpec((ts, LANES), tab),
            ],
            out_specs=(pl.BlockSpec((1, ts, D), tok), pl.BlockSpec((1, ts, D), tok),
                       pl.BlockSpec((1, ts, 2 * KV_DIM), tok), pl.BlockSpec((1, ts, 2 * KV_DIM), tok)),
            scratch_shapes=[pltpu.VMEM((2, ts, D), F32), pltpu.SemaphoreType.DMA((2,))],
        ),
        compiler_params=_cparams(("arbitrary", "arbitrary"), disable_bounds_checks=True),
        name="combine_qkv",
    )(pos, xs, ys, modp, mod, n1g, w_ext, cq, sq, ck, sk)


ATTN_QBLOCKS = 4


def _attn_kernel(q_ref, *rest, lc):
    nq = ATTN_QBLOCKS
    k_refs = rest[:nq + 2]
    v_refs = rest[nq + 2:2 * nq + 4]
    kx_ref, vx_ref, sink_ref = rest[2 * nq + 4:2 * nq + 7]
    cast_in = rest[2 * nq + 7:2 * nq + 7 + N_CAST]
    o_ref = rest[2 * nq + 7 + N_CAST]
    cast_out = rest[2 * nq + 8 + N_CAST:]
    _cast_blocks(cast_in, cast_out)
    i = pl.program_id(1)
    nsteps = pl.num_programs(1)
    nband = 3 * QBLK
    qrow = lax.broadcasted_iota(jnp.int32, (GROUP * QBLK, nband), 0) % QBLK
    kcol = lax.broadcasted_iota(jnp.int32, (GROUP * QBLK, nband), 1)
    lane = lax.broadcasted_iota(jnp.int32, (1, LANES), 1)
    low = lane < HEAD_DIM
    units = [(qb, h) for qb in range(nq) for h in range(N_KV_HEADS)]

    def band_ok(qb):
        lo_lim = jnp.where(i > 0, 0, QBLK) if qb == 0 else 0
        hi_lim = jnp.where(i < nsteps - 1, nband, 2 * QBLK - 1) if qb == nq - 1 else nband
        return (kcol >= jnp.maximum(qrow, lo_lim)) & (kcol <= jnp.minimum(qrow + 2 * WINDOW, hi_lim))

    ok = [band_ok(qb) for qb in range(nq)]

    def scores(unit):
        qb, h = unit
        hs = slice(h * LANES, (h + 1) * LANES)
        k_all = jnp.concatenate([r[0, :, hs] for r in k_refs[qb:qb + 3]] + [kx_ref[0, :, hs]], axis=0)
        qs = []
        for j in range(GROUP):
            c0 = h * GROUP * HEAD_DIM + (j // 2) * LANES
            qh = q_ref[0, qb * QBLK:(qb + 1) * QBLK, c0:c0 + LANES]
            keep = low if j % 2 == 0 else jnp.logical_not(low)
            qs.append(jnp.where(keep, qh, jnp.zeros_like(qh)))
        qm = jnp.concatenate(qs, axis=0)
        return lax.dot_general(qm, k_all, (((1,), (1,)), ((), ())), preferred_element_type=F32)

    def softmax(unit, s):
        qb, h = unit
        s = jnp.concatenate([jnp.where(ok[qb], s[:, :nband], NEG_INF), s[:, nband:]], axis=1)
        sink = LOG2E * jnp.concatenate(
            [jnp.broadcast_to(sink_ref[:, h * GROUP + j:h * GROUP + j + 1], (QBLK, 1)) for j in range(GROUP)], axis=0)
        nblk = s.shape[1] // LANES
        smax = s[:, :LANES]
        for kb in range(1, nblk):
            smax = jnp.maximum(smax, s[:, kb * LANES:(kb + 1) * LANES])
        m = jnp.maximum(jnp.max(smax, axis=1, keepdims=True), sink)
        return jnp.exp2(s - m).astype(BF16), jnp.exp2(sink - m)

    def weighted_values(unit, p, p_sink):
        qb, h = unit
        hs = slice(h * LANES, (h + 1) * LANES)
        v_all = jnp.concatenate([r[0, :, hs] for r in v_refs[qb:qb + 3]] + [vx_ref[0, :, hs]], axis=0)
        ones_lo = jnp.broadcast_to(jnp.where(lane == HEAD_DIM, 1.0, 0.0).astype(v_all.dtype), v_all.shape)
        ones_hi = jnp.broadcast_to(jnp.where(lane == 0, 1.0, 0.0).astype(v_all.dtype), v_all.shape)
        v_lo = jnp.where(low, v_all, ones_lo)
        v_hi = jnp.where(low, ones_hi, v_all)
        for pr in range(GROUP // 2):
            r0 = 2 * pr * QBLK
            o_lo = jnp.dot(p[r0:r0 + QBLK], v_lo, preferred_element_type=F32)
            o_hi = jnp.dot(p[r0 + QBLK:r0 + 2 * QBLK], v_hi, preferred_element_type=F32)
            inv_lo = 1.0 / (o_lo[:, HEAD_DIM:HEAD_DIM + 1] + p_sink[r0:r0 + QBLK])
            inv_hi = 1.0 / (o_hi[:, 0:1] + p_sink[r0 + QBLK:r0 + 2 * QBLK])
            c0 = h * GROUP * HEAD_DIM + pr * LANES
            o_ref[0, qb * QBLK:(qb + 1) * QBLK, c0:c0 + LANES] = (
                jnp.where(low, o_lo * inv_lo, o_hi * inv_hi).astype(BF16))

    n = len(units)
    s_all = [scores(units[0]), scores(units[1])]
    p_cur = softmax(units[0], s_all[0])
    for u in range(n):
        if u + 2 < n:
            s_all.append(scores(units[u + 2]))
        p_next = softmax(units[u + 1], s_all[u + 1]) if u + 1 < n else None
        weighted_values(units[u], *p_cur)
        p_cur = p_next


def _attention(q, kd, vd, kxd, vxd, sink, to_cast, cast_part):
    bsz, s, _ = q.shape
    lc = kxd.shape[1]
    nq = ATTN_QBLOCKS
    nb = s // QBLK
    assert nb % nq == 0
    tok = lambda b, i: (b, i, 0)
    ctx = lambda b, i: (b, 0, 0)
    kvw = 2 * KV_DIM
    band = [pl.BlockSpec((1, QBLK, kvw), lambda b, i, d=d: (b, jnp.clip(nq * i - 1 + d, 0, nb - 1), 0))
            for d in range(nq + 2)]
    nsteps = nb // nq
    cast_in, cast_out, cast_shapes = _cast_plumbing(to_cast, *cast_part, bsz * nsteps, lambda b, i: b * nsteps + i)
    return pl.pallas_call(
        functools.partial(_attn_kernel, lc=lc),
        out_shape=(jax.ShapeDtypeStruct((bsz, s, D), BF16), *cast_shapes),
        grid=(bsz, nsteps),
        in_specs=[pl.BlockSpec((1, nq * QBLK, D), tok)] + band + band + [
            pl.BlockSpec((1, lc, kvw), ctx), pl.BlockSpec((1, lc, kvw), ctx),
            pl.BlockSpec((1, N_HEADS), lambda b, i: (0, 0)),
        ] + cast_in,
        out_specs=(pl.BlockSpec((1, nq * QBLK, D), tok), *cast_out),
        compiler_params=_cparams(("arbitrary", "arbitrary")),
        name="window_attention",
    )(q, *([kd] * (nq + 2)), *([vd] * (nq + 2)), kxd, vxd, sink, *to_cast)


def _attn_out_kernel(a_ref, x_ref, mod_ref, n2g_ref, wo_ref, rwt_ref, rb_ref, x_out_ref, hrow_ref, cls_ref):
    mod = mod_ref[0]
    g1, sh2, sc2 = mod[2:3, :], mod[3:4, :], mod[4:5, :]
    gm2 = n2g_ref[...] * (1.0 + sc2)
    y = jnp.dot(a_ref[0], wo_ref[...], preferred_element_type=F32)
    x_out_ref[0] = _epilogue(x_ref[0], y, g1, sh2, gm2, rwt_ref, rb_ref, hrow_ref, cls_ref)


def _attn_out(attn, xs, mod, n2g, wo, rwt, rb, *, ts):
    bsz, s, _ = xs.shape
    nt = s // ts
    t_all = bsz * s
    const = lambda b, i: (0, 0)
    tok = lambda b, i: (b, i, 0)
    return pl.pallas_call(
        _attn_out_kernel,
        out_shape=(jax.ShapeDtypeStruct((bsz, s, D), F32),
                   jax.ShapeDtypeStruct((t_all, ROW_W), F32),
                   jax.ShapeDtypeStruct((1, t_all), jnp.int32)),
        grid=(bsz, nt),
        in_specs=[
            pl.BlockSpec((1, ts, D), tok), pl.BlockSpec((1, ts, D), tok),
            pl.BlockSpec((1, 6, D), lambda b, i: (b, 0, 0)),
            pl.BlockSpec((1, D), const), pl.BlockSpec((D, D), const),
            pl.BlockSpec((2 * N_EXPERTS, D), const), pl.BlockSpec((N_EXPERTS, 1), const),
        ],
        out_specs=(pl.BlockSpec((1, ts, D), tok),
                   pl.BlockSpec((ts, ROW_W), lambda b, i: (b * nt + i, 0)),
                   pl.BlockSpec((1, ts), lambda b, i: (0, b * nt + i))),
        compiler_params=_cparams(("arbitrary", "arbitrary")),
        name="attn_out",
    )(attn, xs, mod, n2g, wo, rwt, rb)


def _final_kernel(pos_ref, x_ref, ys_ref, mod_ref, fg_ref, o_ref, ybuf, sem, *, ts, seq):
    y = _combine_gather(pos_ref, ys_ref, ybuf, sem, ts=ts, tok_stride=seq, tok_off=0)
    x4 = x_ref[0] + mod_ref[0][5:6, :] * y
    r = lax.rsqrt(jnp.mean(x4 * x4, axis=-1, keepdims=True) + EPS)
    o_ref[0] = x4 * r * fg_ref[...]
    _combine_drain(ys_ref, ybuf, sem, ts)


def _final(pos, xs, ys, mod, fg, *, ts):
    bsz, s, _ = xs.shape
    tok = lambda b, i, p: (b, i, 0)
    return pl.pallas_call(
        functools.partial(_final_kernel, ts=ts, seq=s),
        out_shape=jax.ShapeDtypeStruct((bsz, s, D), F32),
        grid_spec=pltpu.PrefetchScalarGridSpec(
            num_scalar_prefetch=1,
            grid=(bsz, s // ts),
            in_specs=[pl.BlockSpec((1, ts, D), tok), pl.BlockSpec(memory_space=pl.ANY),
                      pl.BlockSpec((1, 6, D), lambda b, i, p: (b, 0, 0)), pl.BlockSpec((1, D), lambda b, i, p: (0, 0))],
            out_specs=pl.BlockSpec((1, ts, D), tok),
            scratch_shapes=[pltpu.VMEM((2, ts, D), F32), pltpu.SemaphoreType.DMA((2,))],
        ),
        compiler_params=_cparams(("arbitrary", "arbitrary"), disable_bounds_checks=True),
        name="combine_final",
    )(pos, xs, ys, mod, fg)


def _rope_tables(seq):
    rows = seq // GRID_W
    inv_freq = jnp.power(jnp.float32(ROPE_THETA), -jnp.arange(ROPE_PAIRS, dtype=F32) / ROPE_PAIRS)
    ang_r = jnp.arange(rows, dtype=jnp.int32).astype(F32)[:, None] * inv_freq
    ang_c = jnp.arange(GRID_W, dtype=jnp.int32).astype(F32)[:, None] * inv_freq
    by_row = lambda v: jnp.repeat(v, GRID_W, axis=0)
    by_col = lambda v: jnp.tile(v, (rows, 1))
    cos_r, sin_r = by_row(jnp.cos(ang_r)), by_row(jnp.sin(ang_r))
    cos_c, sin_c = by_col(jnp.cos(ang_c)), by_col(jnp.sin(ang_c))
    cos = jnp.concatenate([cos_r, cos_r, cos_c, cos_c], axis=1)
    sin = jnp.concatenate([-sin_r, sin_r, -sin_c, sin_c], axis=1)
    cos = jnp.concatenate([cos, cos], axis=1)
    sin = jnp.concatenate([sin, sin], axis=1)
    scale = HEAD_DIM ** -0.5 * LOG2E
    return cos * scale, sin * scale, cos, sin


def kernel(x, c, ctx, c_ctx, w_mod, b_mod, norm1_g, norm2_g, conv_w_pw1, conv_b_pw1, conv_w_dw, conv_b_dw,
           conv_ln_g, conv_ln_b, conv_w_pw2, conv_b_pw2, attn_w_qkv, attn_w_o, attn_sink, router_w, router_b,
           moe_w_gate, moe_w_up, moe_w_down, final_g):
    bsz, s, d = x.shape
    lc = ctx.shape[1]
    ts = 512
    tm = 256
    assert d == D and w_mod.shape[0] == 2 and bsz + 1 <= COND_ROWS
    assert lc <= ts and lc % 256 == 0 and s % ts == 0 and s % (ATTN_QBLOCKS * QBLK) == 0

    cond = jnp.concatenate([c, c_ctx[None, :], jnp.zeros((COND_ROWS - bsz - 1, D), F32)], axis=0)
    mods = _adaln(cond, w_mod, b_mod).reshape(2, COND_ROWS, 6, D)
    mod_lat = [mods[l, :bsz] for l in range(2)]
    mod_ctx = [mods[l, bsz:bsz + 1] for l in range(2)]

    rwt_hi = router_w.T.astype(BF16)
    rwt = jnp.concatenate([rwt_hi, (router_w.T - rwt_hi.astype(F32)).astype(BF16)], axis=0)
    rb = router_b.reshape(N_EXPERTS, 1)
    row = lambda v: v.reshape(1, -1)
    experts_f32 = [moe_w_gate.reshape(-1, D_EXPERT), moe_w_up.reshape(-1, D_EXPERT), moe_w_down.reshape(-1, D)]
    as_experts = lambda ws: (ws[0].reshape(N_EXPERTS, D, D_EXPERT), ws[1].reshape(N_EXPERTS, D, D_EXPERT),
                             ws[2].reshape(N_EXPERTS, D_EXPERT, D))

    x1, ctx1, hrow, cls, *w_l0 = _conv_layer(
        x, ctx, mod_lat[0], mod_ctx[0], row(norm1_g[0]), row(norm2_g[0]), conv_w_pw1[0].astype(BF16),
        row(conv_b_pw1[0]), conv_w_dw[0], row(conv_b_dw[0]), row(conv_ln_g[0]), row(conv_ln_b[0]),
        conv_w_pw2[0].astype(BF16), row(conv_b_pw2[0]), rwt, rb, experts_f32, ts=ts, cast_part=(0, 2))
    rank, counts = _plan(cls)
    pos, src, ea, eb, nused = _dispatch_tables(cls, rank, counts, tm=tm, n_pad=bsz * (ts - lc))
    ys = _moe(hrow, src, ea, eb, nused, *as_experts(w_l0), tm=tm)

    wq = attn_w_qkv[0][:, :D]
    wk = attn_w_qkv[0][:, D:D + KV_DIM].reshape(D, N_KV_HEADS, 1, HEAD_DIM)
    wv = attn_w_qkv[0][:, D + KV_DIM:].reshape(D, N_KV_HEADS, 1, HEAD_DIM)
    dup = lambda w: jnp.broadcast_to(w, (D, N_KV_HEADS, 2, HEAD_DIM)).reshape(D, 2 * KV_DIM)
    w_ext = jnp.concatenate([wq, dup(wk), dup(wv)], axis=1).astype(BF16)
    cq, sq, ck, sk = _rope_tables(s)
    ones = jnp.ones((lc, LANES), F32)
    zeros = jnp.zeros((lc, LANES), F32)
    n1g = row(norm1_g[1])
    x2, q, kd, vd = _qkv(pos, x1, ys, mod_lat[0], mod_lat[1], n1g, w_ext, cq, sq, ck, sk,
                         ts=ts, tok_stride=s + ts, tok_off=0)
    _, _, kxd, vxd = _qkv(pos, ctx1, ys, mod_ctx[0], mod_ctx[1], n1g, w_ext, ones, zeros, ones, zeros,
                          ts=lc, tok_stride=s + ts, tok_off=s)
    attn, *w_l1 = _attention(q, kd, vd, kxd, vxd, attn_sink[0].reshape(1, N_HEADS), experts_f32, (1, 2))
    x3, hrow1, cls1 = _attn_out(attn, x2, mod_lat[1], row(norm2_g[1]), attn_w_o[0].astype(BF16), rwt, rb, ts=ts)
    rank1, counts1 = _plan(cls1)
    pos1, src1, ea1, eb1, nused1 = _dispatch_tables(cls1, rank1, counts1, tm=tm)
    ys1 = _moe(hrow1, src1, ea1, eb1, nused1, *as_experts(w_l1), tm=tm)
    return _final(pos1, x3, ys1, mod_lat[1], row(final_g), ts=2 * ts if s % (2 * ts) == 0 else ts)
```

```python
import functools
import math

import jax
import jax.numpy as jnp
from jax import lax
from jax.experimental import pallas as pl
from jax.experimental.pallas import tpu as pltpu

D = 1024
GRID_W = 64
CONV_WIDTH = 31
CONV_PAD = CONV_WIDTH // 2
SUBLANES = 8
HALO = 16
HEAD_DIM = 64
N_HEADS = D // HEAD_DIM
N_KV_HEADS = N_HEADS // 4
GROUP = N_HEADS // N_KV_HEADS
KV_DIM = N_KV_HEADS * HEAD_DIM
WINDOW = 128
QBLK = 128
ROPE_THETA = 10000.0
ROPE_PAIRS = HEAD_DIM // 4
N_EXPERTS = 16
N_GROUPS = 4
EXPERTS_PER_GROUP = N_EXPERTS // N_GROUPS
N_PAIRS = 6
N_CLASSES = N_GROUPS * N_PAIRS
CLASS_ROWS = 32
NO_CLASS = CLASS_ROWS - 1
D_EXPERT = D // 2
EPS = 1e-6
NEG_INF = -1e30
LOG2E = 1.4426950408889634
LANES = 128
AUX = LANES
ROW_W = D + AUX
COND_ROWS = 16
VMEM_LIMIT = 56 * 1024 * 1024

F32 = jnp.float32
BF16 = jnp.bfloat16
HIGHEST = lax.Precision.HIGHEST


def _cparams(sem, **kw):
    return pltpu.CompilerParams(dimension_semantics=sem, vmem_limit_bytes=VMEM_LIMIT, **kw)


def _sigmoid(v):
    return jax.nn.sigmoid(v)


def _silu(v):
    return v * jax.nn.sigmoid(v)


def _rms_mod(xr, gm, sh):
    r = lax.rsqrt(jnp.mean(xr * xr, axis=-1, keepdims=True) + EPS)
    return xr * r * gm + sh


N_CAST = 3


def _cast_plumbing(arrays, part, n_parts, n_steps, step_of):
    in_specs, out_specs, out_shapes = [], [], []
    for a in arrays:
        rows, cols = a.shape[0] // n_parts, a.shape[1]
        blk = rows
        while (blk // 2) % 16 == 0 and (blk // 2) * n_steps >= rows:
            blk //= 2
        assert rows % blk == 0 and a.shape[0] % n_parts == 0
        nblk = rows // blk
        in_specs.append(pl.BlockSpec(
            (blk, cols), lambda *g, nblk=nblk: (part * nblk + jnp.minimum(step_of(*g), nblk - 1), 0)))
        out_specs.append(pl.BlockSpec((blk, cols), lambda *g, nblk=nblk: (jnp.minimum(step_of(*g), nblk - 1), 0)))
        out_shapes.append(jax.ShapeDtypeStruct((rows, cols), BF16))
    return in_specs, out_specs, out_shapes


def _cast_blocks(in_refs, out_refs):
    for src, dst in zip(in_refs, out_refs):
        dst[...] = src[...].astype(BF16)


def _adaln_kernel(cond_ref, w_ref, b_ref, out_ref):
    cnd = cond_ref[...]
    act = _silu(cnd)
    out_ref[0] = jnp.dot(act, w_ref[0], precision=HIGHEST, preferred_element_type=F32) + b_ref[0]


def _adaln(cond, w_mod, b_mod):
    depth = w_mod.shape[0]
    tn = 1536
    return pl.pallas_call(
        _adaln_kernel,
        out_shape=jax.ShapeDtypeStruct((depth, COND_ROWS, 6 * D), F32),
        grid=(depth, 6 * D // tn),
        in_specs=[
            pl.BlockSpec((COND_ROWS, D), lambda l, n: (0, 0)),
            pl.BlockSpec((1, D, tn), lambda l, n: (l, 0, n)),
            pl.BlockSpec((1, 1, tn), lambda l, n: (l, 0, n)),
        ],
        out_specs=pl.BlockSpec((1, COND_ROWS, tn), lambda l, n: (l, 0, n)),
        compiler_params=_cparams(("arbitrary", "arbitrary")),
        name="adaln",
    )(cond, w_mod, b_mod.reshape(depth, 1, 6 * D))


def _route_rows(h2, rwt_ref, rb_ref):
    nt_dims = (((1,), (1,)), ((), ()))
    h_hi = h2.astype(BF16)
    h_lo = (h2 - h_hi.astype(F32)).astype(BF16)
    w_both = rwt_ref[...]
    hh = lax.dot_general(w_both, h_hi, nt_dims, preferred_element_type=F32)
    lt = (hh[:N_EXPERTS] + hh[N_EXPERTS:]
          + lax.dot_general(w_both[:N_EXPERTS], h_lo, nt_dims, preferred_element_type=F32))
    s = _sigmoid(lt)
    sel = s + rb_ref[...]
    srow = [s[e:e + 1, :] for e in range(N_EXPERTS)]
    vrow = [sel[e:e + 1, :] for e in range(N_EXPERTS)]
    gscore = []
    for g in range(N_GROUPS):
        a, b, c, d = vrow[4 * g:4 * g + 4]
        gscore.append(jnp.maximum(jnp.maximum(jnp.maximum(a + b, a + c), jnp.maximum(a + d, b + c)),
                                  jnp.maximum(b + d, c + d)))
    bg = jnp.zeros_like(gscore[0], dtype=jnp.int32)
    best = gscore[0]
    for g in range(1, N_GROUPS):
        better = gscore[g] > best
        bg = jnp.where(better, g, bg)
        best = jnp.where(better, gscore[g], best)
    v = []
    sv = []
    for j in range(EXPERTS_PER_GROUP):
        vj = vrow[j]
        sj = srow[j]
        for g in range(1, N_GROUPS):
            vj = jnp.where(bg == g, vrow[4 * g + j], vj)
            sj = jnp.where(bg == g, srow[4 * g + j], sj)
        v.append(vj)
        sv.append(sj)
    i1 = jnp.zeros_like(bg)
    m1 = v[0]
    for j in range(1, EXPERTS_PER_GROUP):
        gt = v[j] > m1
        i1 = jnp.where(gt, j, i1)
        m1 = jnp.where(gt, v[j], m1)
    v2 = [jnp.where(i1 == j, -jnp.inf, v[j]) for j in range(EXPERTS_PER_GROUP)]
    i2 = jnp.zeros_like(bg)
    m2 = v2[0]
    for j in range(1, EXPERTS_PER_GROUP):
        gt = v2[j] > m2
        i2 = jnp.where(gt, j, i2)
        m2 = jnp.where(gt, v2[j], m2)
    lo = jnp.minimum(i1, i2)
    hi = jnp.maximum(i1, i2)
    s_lo = sv[0]
    s_hi = sv[0]
    for j in range(1, EXPERTS_PER_GROUP):
        s_lo = jnp.where(lo == j, sv[j], s_lo)
        s_hi = jnp.where(hi == j, sv[j], s_hi)
    den = s_lo + s_hi
    pair = jnp.where(lo == 0, 0, jnp.where(lo == 1, 3, 5)) + (hi - lo - 1)
    cls = N_PAIRS * bg + pair
    return s_lo / den, s_hi / den, cls


def _epilogue(x_in, y, g1, sh2, gm2, rwt_ref, rb_ref, hrow_ref, cls_ref, n_valid=None):
    x1 = x_in + g1 * y
    h2 = _rms_mod(x1, gm2, sh2)
    ga, gb, cls = _route_rows(h2, rwt_ref, rb_ref)
    ts = h2.shape[0]
    if n_valid is not None:
        cls = jnp.where(lax.broadcasted_iota(jnp.int32, cls.shape, 1) < n_valid, cls, NO_CLASS)
    cls_ref[...] = cls
    g8 = jnp.concatenate([ga, gb, jnp.zeros((6, ts), F32)], axis=0)
    g128 = jnp.concatenate([g8, jnp.zeros((AUX - 8, ts), F32)], axis=0)
    hrow_ref[:, :D] = h2
    hrow_ref[:, D:] = jnp.transpose(g128)
    return x1


def _conv_layer_kernel(x_ref, c_ref, xp_ref, xn_ref, modl_ref, modc_ref, n1g_ref, n2g_ref, w1_ref, b1_ref, wdw_ref,
                       bdw_ref, lng_ref, lnb_ref, w2_ref, b2_ref, rwt_ref, rb_ref, *rest, ts, n_ctx):
    cast_in = rest[:N_CAST]
    x_out_ref, c_out_ref, hrow_ref, cls_ref = rest[N_CAST:N_CAST + 4]
    cast_out = rest[N_CAST + 4:2 * N_CAST + 4]
    u_scr, cv_scr, v_scr = rest[2 * N_CAST + 4:]
    _cast_blocks(cast_in, cast_out)
    i = pl.program_id(1)
    nt = pl.num_programs(1) - 1
    is_ctx = jnp.full((1, 1), i, jnp.int32) == nt
    n_valid = jnp.where(i == nt, n_ctx, ts)
    mod = jnp.where(is_ctx, modc_ref[0], modl_ref[0])
    sh1, sc1, g1, sh2, sc2 = (mod[k:k + 1, :] for k in range(5))
    gm1 = n1g_ref[...] * (1.0 + sc1)
    gm2 = n2g_ref[...] * (1.0 + sc2)
    cw = 512

    def glu(hb, c0):
        a = jnp.dot(hb, w1_ref[:, c0:c0 + cw], preferred_element_type=F32) + b1_ref[:, c0:c0 + cw]
        g = jnp.dot(hb, w1_ref[:, D + c0:D + c0 + cw], preferred_element_type=F32) + b1_ref[:, D + c0:D + c0 + cw]
        return a * _sigmoid(g)

    ctx_tile = c_ref[0]
    if n_ctx < ts:
        ctx_tile = jnp.concatenate([ctx_tile, jnp.zeros((ts - n_ctx, D), F32)], axis=0)
    xm = jnp.where(is_ctx, ctx_tile, x_ref[0])
    hb = jnp.concatenate([_rms_mod(xp_ref[0, 0], gm1, sh1).astype(BF16), _rms_mod(xm, gm1, sh1).astype(BF16),
                          _rms_mod(xn_ref[0, 0], gm1, sh1).astype(BF16)], axis=0)
    keep_prev = jnp.logical_and(i > 0, i < nt).astype(F32)
    keep_next = (i < nt - 1).astype(F32)
    keep_row = (lax.broadcasted_iota(jnp.int32, (ts, 1), 0) < n_valid).astype(F32)
    for c0 in range(0, D, cw):
        u = glu(hb, c0)
        u_scr[0:HALO, c0:c0 + cw] = u[:HALO] * keep_prev
        u_scr[HALO:HALO + ts, c0:c0 + cw] = u[HALO:HALO + ts] * keep_row
        u_scr[HALO + ts:2 * HALO + ts, c0:c0 + cw] = u[HALO + ts:] * keep_next

    rr = 128
    taps_off = HALO - CONV_PAD

    def conv_chunk(rc, carry):
        r0 = pl.multiple_of(rc * rr, rr)
        for c0 in range(0, D, LANES):
            win = u_scr[pl.ds(r0, rr + 2 * HALO), c0:c0 + LANES]
            out = None
            for r in range(SUBLANES):
                nrow = rr + SUBLANES if r else rr
                part = None
                for q in range((CONV_WIDTH + taps_off) // SUBLANES + 1):
                    k = SUBLANES * q + r - taps_off
                    if 0 <= k < CONV_WIDTH:
                        term = win[SUBLANES * q:SUBLANES * q + nrow, :] * wdw_ref[k:k + 1, c0:c0 + LANES]
                        part = term if part is None else part + term
                shifted = part[r:r + rr, :]
                out = shifted if out is None else out + shifted
            cv_scr[pl.ds(r0, rr), c0:c0 + LANES] = out
        cv = cv_scr[pl.ds(r0, rr), :] + bdw_ref[...]
        mu = jnp.mean(cv, axis=-1, keepdims=True)
        dv = cv - mu
        var = jnp.mean(dv * dv, axis=-1, keepdims=True)
        ln = dv * lax.rsqrt(var + EPS) * lng_ref[...] + lnb_ref[...]
        v_scr[pl.ds(r0, rr), :] = _silu(ln).astype(BF16)
        return carry

    lax.fori_loop(0, ts // rr, conv_chunk, 0)
    y = jnp.dot(v_scr[...], w2_ref[...], preferred_element_type=F32) + b2_ref[...]
    x1 = _epilogue(xm, y, g1, sh2, gm2, rwt_ref, rb_ref, hrow_ref, cls_ref, n_valid=n_valid)

    @pl.when(i < nt)
    def _():
        x_out_ref[0] = x1

    @pl.when(i == nt)
    def _():
        c_out_ref[0] = x1[:n_ctx]


def _conv_layer(xs, cs, mod_lat, mod_ctx, n1g, n2g, w1, b1, wdw, bdw, lng, lnb, w2, b2, rwt, rb, to_cast, *, ts,
                cast_part):
    bsz, s, _ = xs.shape
    n_ctx = cs.shape[1]
    assert n_ctx <= ts and n_ctx % SUBLANES == 0
    nt = s // ts
    nh = s // HALO
    hpt = ts // HALO
    x4 = xs.reshape(bsz, nh, HALO, D)
    t_all = bsz * (s + ts)
    const = lambda b, i: (0, 0)
    lat = lambda b, i: (b, jnp.minimum(i, nt - 1), 0)
    row_blk = lambda b, i: (b * (nt + 1) + i, 0)
    cast_in, cast_out, cast_shapes = _cast_plumbing(to_cast, *cast_part, bsz * (nt + 1), lambda b, i: b * (nt + 1) + i)
    return pl.pallas_call(
        functools.partial(_conv_layer_kernel, ts=ts, n_ctx=n_ctx),
        out_shape=(jax.ShapeDtypeStruct((bsz, s, D), F32),
                   jax.ShapeDtypeStruct((bsz, n_ctx, D), F32),
                   jax.ShapeDtypeStruct((t_all, ROW_W), F32),
                   jax.ShapeDtypeStruct((1, t_all), jnp.int32), *cast_shapes),
        grid=(bsz, nt + 1),
        in_specs=[
            pl.BlockSpec((1, ts, D), lat),
            pl.BlockSpec((1, n_ctx, D), lambda b, i: (b, 0, 0)),
            pl.BlockSpec((1, 1, HALO, D), lambda b, i: (b, jnp.clip(i * hpt - 1, 0, nh - 1), 0, 0)),
            pl.BlockSpec((1, 1, HALO, D), lambda b, i: (b, jnp.minimum((i + 1) * hpt, nh - 1), 0, 0)),
            pl.BlockSpec((1, 6, D), lambda b, i: (b, 0, 0)),
            pl.BlockSpec((1, 6, D), lambda b, i: (0, 0, 0)),
            pl.BlockSpec((1, D), const), pl.BlockSpec((1, D), const),
            pl.BlockSpec((D, 2 * D), const), pl.BlockSpec((1, 2 * D), const),
            pl.BlockSpec((CONV_WIDTH, D), const), pl.BlockSpec((1, D), const),
            pl.BlockSpec((1, D), const), pl.BlockSpec((1, D), const),
            pl.BlockSpec((D, D), const), pl.BlockSpec((1, D), const),
            pl.BlockSpec((2 * N_EXPERTS, D), const), pl.BlockSpec((N_EXPERTS, 1), const),
        ] + cast_in,
        out_specs=(pl.BlockSpec((1, ts, D), lat),
                   pl.BlockSpec((1, n_ctx, D), lambda b, i: (b, 0, 0)),
                   pl.BlockSpec((ts, ROW_W), row_blk),
                   pl.BlockSpec((1, ts), lambda b, i: (0, b * (nt + 1) + i)), *cast_out),
        scratch_shapes=[pltpu.VMEM((ts + 2 * HALO, D), F32), pltpu.VMEM((ts, D), F32), pltpu.VMEM((ts, D), BF16)],
        compiler_params=_cparams(("arbitrary", "arbitrary")),
        name="conv_layer",
    )(xs, cs, x4, x4, mod_lat, mod_ctx, n1g, n2g, w1, b1, wdw, bdw, lng, lnb, w2, b2, rwt, rb, *to_cast)


def _plan_kernel(cls_ref, rank_ref, cnt_ref, carry_scr, *, tb):
    @pl.when(pl.program_id(0) == 0)
    def _():
        carry_scr[...] = jnp.zeros_like(carry_scr)

    cls = cls_ref[...]
    onehot = (lax.broadcasted_iota(jnp.int32, (CLASS_ROWS, tb), 0) == cls).astype(F32)
    upper = (lax.broadcasted_iota(jnp.int32, (tb, tb), 0) <= lax.broadcasted_iota(jnp.int32, (tb, tb), 1))
    prefix = jnp.dot(onehot.astype(BF16), upper.astype(BF16), preferred_element_type=F32)
    carry = carry_scr[:, 0:1]
    rank = jnp.sum(onehot * (prefix + carry), axis=0, keepdims=True) - 1.0
    rank_ref[...] = rank.astype(jnp.int32)
    total = carry + jnp.sum(onehot, axis=1, keepdims=True)
    carry_scr[...] = jnp.broadcast_to(total, carry_scr.shape)
    cnt_ref[...] = jnp.broadcast_to(total, cnt_ref.shape)


def _plan(cls):
    t_all = cls.shape[1]
    tb = math.gcd(t_all, 1024)
    return pl.pallas_call(
        functools.partial(_plan_kernel, tb=tb),
        out_shape=(jax.ShapeDtypeStruct((1, t_all), jnp.int32), jax.ShapeDtypeStruct((CLASS_ROWS, LANES), F32)),
        grid=(t_all // tb,),
        in_specs=[pl.BlockSpec((1, tb), lambda i: (0, i))],
        out_specs=(pl.BlockSpec((1, tb), lambda i: (0, i)), pl.BlockSpec((CLASS_ROWS, LANES), lambda i: (0, 0))),
        scratch_shapes=[pltpu.VMEM((CLASS_ROWS, LANES), F32)],
        compiler_params=_cparams(("arbitrary",)),
        name="route_plan",
    )(cls)


def _invert_kernel(pos_ref, gap_ref, src_ref, *, n_tok, n_gap):
    def zero(i, carry):
        src_ref[i] = 0
        return carry
    for g in range(n_gap):
        lax.fori_loop(gap_ref[2 * g], gap_ref[2 * g + 1], zero, 0)

    unroll = 16

    def put(i, carry):
        for u in range(unroll):
            t = i * unroll + u
            src_ref[pos_ref[t]] = t
        return carry
    lax.fori_loop(0, n_tok // unroll, put, 0)


def _invert(pos, gaps, n_slot):
    n_tok = pos.shape[0]
    return pl.pallas_call(
        functools.partial(_invert_kernel, n_tok=n_tok, n_gap=gaps.shape[0] // 2),
        out_shape=jax.ShapeDtypeStruct((n_slot,), jnp.int32),
        in_specs=[pl.BlockSpec(memory_space=pltpu.SMEM), pl.BlockSpec(memory_space=pltpu.SMEM)],
        out_specs=pl.BlockSpec(memory_space=pltpu.SMEM),
        name="invert_slots",
    )(pos, gaps)


def _dispatch_tables(cls, rank, counts, *, tm, n_pad=0):
    t_all = cls.shape[1]
    n_tiles = -(-((t_all - n_pad) // tm + N_CLASSES) // MOE_TILES_PER_STEP) * MOE_TILES_PER_STEP
    cnt = counts[:N_CLASSES, 0].astype(jnp.int32)
    padded = ((cnt + tm - 1) // tm) * tm
    ends = jnp.cumsum(padded)
    offs = ends - padded
    first_slot = jnp.zeros((CLASS_ROWS,), jnp.int32).at[:N_CLASSES].set(offs).at[NO_CLASS].set(n_tiles * tm)
    pos = first_slot[cls[0]] + rank[0]
    gap_lo = jnp.concatenate([offs + cnt, ends[-1:]])
    gap_hi = jnp.concatenate([ends, jnp.full((1,), n_tiles * tm, jnp.int32)])
    src = _invert(pos, jnp.stack([gap_lo, gap_hi], axis=1).reshape(-1).astype(jnp.int32), n_tiles * tm + n_pad)
    n_used = ends[-1] // tm
    tile_start = jnp.minimum(jnp.arange(n_tiles, dtype=jnp.int32), n_used - 1) * tm
    tile_cls = jnp.minimum(jnp.sum(ends[None, :] <= tile_start[:, None], axis=1), N_CLASSES - 1).astype(jnp.int32)
    pair_lo = jnp.array([0, 0, 0, 1, 1, 2], jnp.int32)
    pair_hi = jnp.array([1, 2, 3, 2, 3, 3], jnp.int32)
    grp = tile_cls // N_PAIRS
    ea = EXPERTS_PER_GROUP * grp + pair_lo[tile_cls % N_PAIRS]
    eb = EXPERTS_PER_GROUP * grp + pair_hi[tile_cls % N_PAIRS]
    return pos, src, ea, eb, n_used.reshape(1).astype(jnp.int32)


def _start_row_gather(idx_ref, idx_base, src_hbm, dst, sem, n, unroll):
    def one(r):
        row = idx_ref[idx_base + r]
        copy = pltpu.make_async_copy(src_hbm.at[pl.ds(row, 1), :], dst.at[pl.ds(r, 1), :], sem)
        copy.start(priority=r % 2 if unroll else 0)

    if unroll:
        for r in range(n):
            one(r)
    else:
        def body(r, carry):
            one(r)
            return carry
        lax.fori_loop(0, n, body, 0)


def _wait_row_gather(src_hbm, dst, sem, n):
    pltpu.make_async_copy(src_hbm.at[pl.ds(0, n), :], dst, sem).wait()


MOE_TILES_PER_STEP = 2


def _moe_kernel(src_ref, ea_ref, eb_ref, nused_ref, hrow_ref, *rest, tm):
    k = MOE_TILES_PER_STEP
    weights = [rest[6 * i:6 * i + 6] for i in range(k)]
    ys_ref, gbuf, sem = rest[6 * k:]
    j = pl.program_id(0)
    nused = nused_ref[0]

    def gather(tile, slot, unroll):
        @pl.when(tile < nused)
        def _():
            _start_row_gather(src_ref, tile * tm, hrow_ref, gbuf.at[slot], sem.at[slot], tm, unroll=unroll)

    @pl.when(j == 0)
    def _():
        for t in range(2 * k):
            gather(t, t, unroll=False)

    for i in range(k):
        tile = j * k + i
        slot = (j % 2) * k + i
        wga, wua, wda, wgb, wub, wdb = weights[i]
        out = ys_ref.at[pl.ds(i * tm, tm), :]

        @pl.when(tile < nused)
        def _():
            _wait_row_gather(hrow_ref, gbuf.at[slot], sem.at[slot], tm)
            h = gbuf[slot, :, :D].astype(BF16)
            ga = gbuf[slot, :, D:D + 1]
            gb = gbuf[slot, :, D + 1:D + 2]

            def expert(wg, wu, wd):
                a = jnp.dot(h, wg[0], preferred_element_type=F32)
                u = jnp.dot(h, wu[0], preferred_element_type=F32)
                act = (_silu(a) * u).astype(BF16)
                return jnp.dot(act, wd[0], preferred_element_type=F32)

            out[...] = ga * expert(wga, wua, wda) + gb * expert(wgb, wub, wdb)

        @pl.when(tile >= nused)
        def _():
            out[...] = jnp.zeros((tm, D), F32)

        gather(tile + 2 * k, slot, unroll=True)


def _moe(hrow, src, ea, eb, nused, wg, wu, wd, *, tm):
    k = MOE_TILES_PER_STEP
    n_tiles = ea.shape[0]
    assert n_tiles % k == 0
    f = wg.shape[-1]
    w_specs = []
    for i in range(k):
        wa = lambda j, src, ea, eb, nu, i=i: (ea[j * k + i], 0, 0)
        wb = lambda j, src, ea, eb, nu, i=i: (eb[j * k + i], 0, 0)
        w_specs += [pl.BlockSpec((1, D, f), wa), pl.BlockSpec((1, D, f), wa), pl.BlockSpec((1, f, D), wa),
                    pl.BlockSpec((1, D, f), wb), pl.BlockSpec((1, D, f), wb), pl.BlockSpec((1, f, D), wb)]
    return pl.pallas_call(
        functools.partial(_moe_kernel, tm=tm),
        out_shape=jax.ShapeDtypeStruct((n_tiles * tm, D), F32),
        grid_spec=pltpu.PrefetchScalarGridSpec(
            num_scalar_prefetch=4,
            grid=(n_tiles // k,),
            in_specs=[pl.BlockSpec(memory_space=pl.ANY)] + w_specs,
            out_specs=pl.BlockSpec((k * tm, D), lambda j, src, ea, eb, nu: (j, 0)),
            scratch_shapes=[pltpu.VMEM((2 * k, tm, ROW_W), F32), pltpu.SemaphoreType.DMA((2 * k,))],
        ),
        compiler_params=_cparams(("arbitrary",), disable_bounds_checks=True),
        name="moe_experts",
    )(src, ea, eb, nused, hrow, *([wg, wu, wd, wg, wu, wd] * k))


def _combine_base(step, *, ts, tok_stride, tok_off):
    nt = pl.num_programs(1)
    return (step // nt) * tok_stride + tok_off + (step % nt) * ts


def _combine_gather(pos_ref, ys_ref, ybuf, sem, *, ts, tok_stride, tok_off):
    nt = pl.num_programs(1)
    n = pl.program_id(0) * nt + pl.program_id(1)

    @pl.when(n == 0)
    def _():
        _start_row_gather(pos_ref, _combine_base(0, ts=ts, tok_stride=tok_stride, tok_off=tok_off), ys_ref,
                          ybuf.at[0], sem.at[0], ts, unroll=False)

    nxt = jnp.minimum(n + 1, pl.num_programs(0) * nt - 1)
    _start_row_gather(pos_ref, _combine_base(nxt, ts=ts, tok_stride=tok_stride, tok_off=tok_off), ys_ref,
                      ybuf.at[1 - n % 2], sem.at[1 - n % 2], ts, unroll=True)
    _wait_row_gather(ys_ref, ybuf.at[n % 2], sem.at[n % 2], ts)
    return ybuf[n % 2]


def _combine_drain(ys_ref, ybuf, sem, ts):
    nt = pl.num_programs(1)
    n = pl.program_id(0) * nt + pl.program_id(1)

    @pl.when(n == pl.num_programs(0) * nt - 1)
    def _():
        _wait_row_gather(ys_ref, ybuf.at[1 - n % 2], sem.at[1 - n % 2], ts)


def _rope(v, cos, sin):
    lane = lax.broadcasted_iota(jnp.int32, v.shape, 1)
    first = (lane % (2 * ROPE_PAIRS)) < ROPE_PAIRS
    partner = jnp.where(first, pltpu.roll(v, LANES - ROPE_PAIRS, 1), pltpu.roll(v, ROPE_PAIRS, 1))
    return v * cos + partner * sin


def _qkv_kernel(pos_ref, x_ref, ys_ref, modp_ref, mod_ref, n1g_ref, w_ref, cq_ref, sq_ref, ck_ref, sk_ref,
                x_out_ref, q_ref, kd_ref, vd_ref, ybuf, sem, *, ts, seq, tok_stride, tok_off):
    y = _combine_gather(pos_ref, ys_ref, ybuf, sem, ts=ts, tok_stride=tok_stride, tok_off=tok_off)
    g2_prev = modp_ref[0][5:6, :]
    x2 = x_ref[0] + g2_prev * y
    x_out_ref[0] = x2
    mod = mod_ref[0]
    sh1 = mod[0:1, :]
    gm1 = n1g_ref[...] * (1.0 + mod[1:2, :])
    hb = _rms_mod(x2, gm1, sh1).astype(BF16)
    cq, sq, ck, sk = cq_ref[...], sq_ref[...], ck_ref[...], sk_ref[...]
    nw = 2 * LANES
    for c0 in range(0, D, nw):
        qc = jnp.dot(hb, w_ref[:, c0:c0 + nw], preferred_element_type=F32)
        for l0 in range(0, nw, LANES):
            q_ref[0, :, c0 + l0:c0 + l0 + LANES] = _rope(qc[:, l0:l0 + LANES], cq, sq).astype(BF16)
    for c0 in range(0, 2 * KV_DIM, nw):
        kc = jnp.dot(hb, w_ref[:, D + c0:D + c0 + nw], preferred_element_type=F32)
        for l0 in range(0, nw, LANES):
            kd_ref[0, :, c0 + l0:c0 + l0 + LANES] = _rope(kc[:, l0:l0 + LANES], ck, sk).astype(BF16)
    vd_ref[0] = jnp.dot(hb, w_ref[:, D + 2 * KV_DIM:], preferred_element_type=F32).astype(BF16)
    _combine_drain(ys_ref, ybuf, sem, ts)


def _qkv(pos, xs, ys, modp, mod, n1g, w_ext, cq, sq, ck, sk, *, ts, tok_stride, tok_off):
    bsz, s, _ = xs.shape
    nt = s // ts
    per_batch = mod.shape[0] > 1
    mod_map = (lambda b, i, p: (b, 0, 0)) if per_batch else (lambda b, i, p: (0, 0, 0))
    const = lambda b, i, p: (0, 0)
    tok = lambda b, i, p: (b, i, 0)
    tab = lambda b, i, p: (i, 0)
    return pl.pallas_call(
        functools.partial(_qkv_kernel, ts=ts, seq=s, tok_stride=tok_stride, tok_off=tok_off),
        out_shape=(jax.ShapeDtypeStruct((bsz, s, D), F32), jax.ShapeDtypeStruct((bsz, s, D), BF16),
                   jax.ShapeDtypeStruct((bsz, s, 2 * KV_DIM), BF16), jax.ShapeDtypeStruct((bsz, s, 2 * KV_DIM), BF16)),
        grid_spec=pltpu.PrefetchScalarGridSpec(
            num_scalar_prefetch=1,
            grid=(bsz, nt),
            in_specs=[
                pl.BlockSpec((1, ts, D), tok),
                pl.BlockSpec(memory_space=pl.ANY),
                pl.BlockSpec((1, 6, D), mod_map), pl.BlockSpec((1, 6, D), mod_map),
                pl.BlockSpec((1, D), const),
                pl.BlockSpec((D, D + 4 * KV_DIM), const),
                pl.BlockSpec((ts, LANES), tab), pl.BlockSpec((ts, LANES), tab),
                pl.BlockSpec((ts, LANES), tab), pl.BlockSpec((ts, LANES), tab),
            ],
            out_specs=(pl.BlockSpec((1, ts, D), tok), pl.BlockSpec((1, ts, D), tok),
                       pl.BlockSpec((1, ts, 2 * KV_DIM), tok), pl.BlockSpec((1, ts, 2 * KV_DIM), tok)),
            scratch_shapes=[pltpu.VMEM((2, ts, D), F32), pltpu.SemaphoreType.DMA((2,))],
        ),
        compiler_params=_cparams(("arbitrary", "arbitrary"), disable_bounds_checks=True),
        name="combine_qkv",
    )(pos, xs, ys, modp, mod, n1g, w_ext, cq, sq, ck, sk)


ATTN_QBLOCKS = 4


def _attn_kernel(q_ref, *rest, lc):
    nq = ATTN_QBLOCKS
    k_refs = rest[:nq + 2]
    v_refs = rest[nq + 2:2 * nq + 4]
    kx_ref, vx_ref, sink_ref = rest[2 * nq + 4:2 * nq + 7]
    cast_in = rest[2 * nq + 7:2 * nq + 7 + N_CAST]
    o_ref = rest[2 * nq + 7 + N_CAST]
    cast_out = rest[2 * nq + 8 + N_CAST:]
    _cast_blocks(cast_in, cast_out)
    i = pl.program_id(1)
    nsteps = pl.num_programs(1)
    nband = 3 * QBLK
    qrow = lax.broadcasted_iota(jnp.int32, (GROUP * QBLK, nband), 0) % QBLK
    kcol = lax.broadcasted_iota(jnp.int32, (GROUP * QBLK, nband), 1)
    lane = lax.broadcasted_iota(jnp.int32, (1, LANES), 1)
    low = lane < HEAD_DIM
    units = [(qb, h) for qb in range(nq) for h in range(N_KV_HEADS)]

    def band_ok(qb):
        lo_lim = jnp.where(i > 0, 0, QBLK) if qb == 0 else 0
        hi_lim = jnp.where(i < nsteps - 1, nband, 2 * QBLK - 1) if qb == nq - 1 else nband
        return (kcol >= jnp.maximum(qrow, lo_lim)) & (kcol <= jnp.minimum(qrow + 2 * WINDOW, hi_lim))

    ok = [band_ok(qb) for qb in range(nq)]

    def scores(unit):
        qb, h = unit
        hs = slice(h * LANES, (h + 1) * LANES)
        k_all = jnp.concatenate([r[0, :, hs] for r in k_refs[qb:qb + 3]] + [kx_ref[0, :, hs]], axis=0)
        qs = []
        for j in range(GROUP):
            c0 = h * GROUP * HEAD_DIM + (j // 2) * LANES
            qh = q_ref[0, qb * QBLK:(qb + 1) * QBLK, c0:c0 + LANES]
            keep = low if j % 2 == 0 else jnp.logical_not(low)
            qs.append(jnp.where(keep, qh, jnp.zeros_like(qh)))
        qm = jnp.concatenate(qs, axis=0)
        return lax.dot_general(qm, k_all, (((1,), (1,)), ((), ())), preferred_element_type=F32)

    def softmax(unit, s):
        qb, h = unit
        s = jnp.concatenate([jnp.where(ok[qb], s[:, :nband], NEG_INF), s[:, nband:]], axis=1)
        sink = LOG2E * jnp.concatenate(
            [jnp.broadcast_to(sink_ref[:, h * GROUP + j:h * GROUP + j + 1], (QBLK, 1)) for j in range(GROUP)], axis=0)
        nblk = s.shape[1] // LANES
        smax = s[:, :LANES]
        for kb in range(1, nblk):
            smax = jnp.maximum(smax, s[:, kb * LANES:(kb + 1) * LANES])
        m = jnp.maximum(jnp.max(smax, axis=1, keepdims=True), sink)
        return jnp.exp2(s - m).astype(BF16), jnp.exp2(sink - m)

    def weighted_values(unit, p, p_sink):
        qb, h = unit
        hs = slice(h * LANES, (h + 1) * LANES)
        v_all = jnp.concatenate([r[0, :, hs] for r in v_refs[qb:qb + 3]] + [vx_ref[0, :, hs]], axis=0)
        ones_lo = jnp.broadcast_to(jnp.where(lane == HEAD_DIM, 1.0, 0.0).astype(v_all.dtype), v_all.shape)
        ones_hi = jnp.broadcast_to(jnp.where(lane == 0, 1.0, 0.0).astype(v_all.dtype), v_all.shape)
        v_lo = jnp.where(low, v_all, ones_lo)
        v_hi = jnp.where(low, ones_hi, v_all)
        for pr in range(GROUP // 2):
            r0 = 2 * pr * QBLK
            o_lo = jnp.dot(p[r0:r0 + QBLK], v_lo, preferred_element_type=F32)
            o_hi = jnp.dot(p[r0 + QBLK:r0 + 2 * QBLK], v_hi, preferred_element_type=F32)
            inv_lo = 1.0 / (o_lo[:, HEAD_DIM:HEAD_DIM + 1] + p_sink[r0:r0 + QBLK])
            inv_hi = 1.0 / (o_hi[:, 0:1] + p_sink[r0 + QBLK:r0 + 2 * QBLK])
            c0 = h * GROUP * HEAD_DIM + pr * LANES
            o_ref[0, qb * QBLK:(qb + 1) * QBLK, c0:c0 + LANES] = (
                jnp.where(low, o_lo * inv_lo, o_hi * inv_hi).astype(BF16))

    n = len(units)
    s_all = [scores(units[0]), scores(units[1])]
    p_cur = softmax(units[0], s_all[0])
    for u in range(n):
        if u + 2 < n:
            s_all.append(scores(units[u + 2]))
        p_next = softmax(units[u + 1], s_all[u + 1]) if u + 1 < n else None
        weighted_values(units[u], *p_cur)
        p_cur = p_next


def _attention(q, kd, vd, kxd, vxd, sink, to_cast, cast_part):
    bsz, s, _ = q.shape
    lc = kxd.shape[1]
    nq = ATTN_QBLOCKS
    nb = s // QBLK
    assert nb % nq == 0
    tok = lambda b, i: (b, i, 0)
    ctx = lambda b, i: (b, 0, 0)
    kvw = 2 * KV_DIM
    band = [pl.BlockSpec((1, QBLK, kvw), lambda b, i, d=d: (b, jnp.clip(nq * i - 1 + d, 0, nb - 1), 0))
            for d in range(nq + 2)]
    nsteps = nb // nq
    cast_in, cast_out, cast_shapes = _cast_plumbing(to_cast, *cast_part, bsz * nsteps, lambda b, i: b * nsteps + i)
    return pl.pallas_call(
        functools.partial(_attn_kernel, lc=lc),
        out_shape=(jax.ShapeDtypeStruct((bsz, s, D), BF16), *cast_shapes),
        grid=(bsz, nsteps),
        in_specs=[pl.BlockSpec((1, nq * QBLK, D), tok)] + band + band + [
            pl.BlockSpec((1, lc, kvw), ctx), pl.BlockSpec((1, lc, kvw), ctx),
            pl.BlockSpec((1, N_HEADS), lambda b, i: (0, 0)),
        ] + cast_in,
        out_specs=(pl.BlockSpec((1, nq * QBLK, D), tok), *cast_out),
        compiler_params=_cparams(("arbitrary", "arbitrary")),
        name="window_attention",
    )(q, *([kd] * (nq + 2)), *([vd] * (nq + 2)), kxd, vxd, sink, *to_cast)


def _attn_out_kernel(a_ref, x_ref, mod_ref, n2g_ref, wo_ref, rwt_ref, rb_ref, x_out_ref, hrow_ref, cls_ref):
    mod = mod_ref[0]
    g1, sh2, sc2 = mod[2:3, :], mod[3:4, :], mod[4:5, :]
    gm2 = n2g_ref[...] * (1.0 + sc2)
    y = jnp.dot(a_ref[0], wo_ref[...], preferred_element_type=F32)
    x_out_ref[0] = _epilogue(x_ref[0], y, g1, sh2, gm2, rwt_ref, rb_ref, hrow_ref, cls_ref)


def _attn_out(attn, xs, mod, n2g, wo, rwt, rb, *, ts):
    bsz, s, _ = xs.shape
    nt = s // ts
    t_all = bsz * s
    const = lambda b, i: (0, 0)
    tok = lambda b, i: (b, i, 0)
    return pl.pallas_call(
        _attn_out_kernel,
        out_shape=(jax.ShapeDtypeStruct((bsz, s, D), F32),
                   jax.ShapeDtypeStruct((t_all, ROW_W), F32),
                   jax.ShapeDtypeStruct((1, t_all), jnp.int32)),
        grid=(bsz, nt),
        in_specs=[
            pl.BlockSpec((1, ts, D), tok), pl.BlockSpec((1, ts, D), tok),
            pl.BlockSpec((1, 6, D), lambda b, i: (b, 0, 0)),
            pl.BlockSpec((1, D), const), pl.BlockSpec((D, D), const),
            pl.BlockSpec((2 * N_EXPERTS, D), const), pl.BlockSpec((N_EXPERTS, 1), const),
        ],
        out_specs=(pl.BlockSpec((1, ts, D), tok),
                   pl.BlockSpec((ts, ROW_W), lambda b, i: (b * nt + i, 0)),
                   pl.BlockSpec((1, ts), lambda b, i: (0, b * nt + i))),
        compiler_params=_cparams(("arbitrary", "arbitrary")),
        name="attn_out",
    )(attn, xs, mod, n2g, wo, rwt, rb)


def _final_kernel(pos_ref, x_ref, ys_ref, mod_ref, fg_ref, o_ref, ybuf, sem, *, ts, seq):
    y = _combine_gather(pos_ref, ys_ref, ybuf, sem, ts=ts, tok_stride=seq, tok_off=0)
    x4 = x_ref[0] + mod_ref[0][5:6, :] * y
    r = lax.rsqrt(jnp.mean(x4 * x4, axis=-1, keepdims=True) + EPS)
    o_ref[0] = x4 * r * fg_ref[...]
    _combine_drain(ys_ref, ybuf, sem, ts)


def _final(pos, xs, ys, mod, fg, *, ts):
    bsz, s, _ = xs.shape
    tok = lambda b, i, p: (b, i, 0)
    return pl.pallas_call(
        functools.partial(_final_kernel, ts=ts, seq=s),
        out_shape=jax.ShapeDtypeStruct((bsz, s, D), F32),
        grid_spec=pltpu.PrefetchScalarGridSpec(
            num_scalar_prefetch=1,
            grid=(bsz, s // ts),
            in_specs=[pl.BlockSpec((1, ts, D), tok), pl.BlockSpec(memory_space=pl.ANY),
                      pl.BlockSpec((1, 6, D), lambda b, i, p: (b, 0, 0)), pl.BlockSpec((1, D), lambda b, i, p: (0, 0))],
            out_specs=pl.BlockSpec((1, ts, D), tok),
            scratch_shapes=[pltpu.VMEM((2, ts, D), F32), pltpu.SemaphoreType.DMA((2,))],
        ),
        compiler_params=_cparams(("arbitrary", "arbitrary"), disable_bounds_checks=True),
        name="combine_final",
    )(pos, xs, ys, mod, fg)


def _rope_tables(seq):
    rows = seq // GRID_W
    inv_freq = jnp.power(jnp.float32(ROPE_THETA), -jnp.arange(ROPE_PAIRS, dtype=F32) / ROPE_PAIRS)
    ang_r = jnp.arange(rows, dtype=jnp.int32).astype(F32)[:, None] * inv_freq
    ang_c = jnp.arange(GRID_W, dtype=jnp.int32).astype(F32)[:, None] * inv_freq
    by_row = lambda v: jnp.repeat(v, GRID_W, axis=0)
    by_col = lambda v: jnp.tile(v, (rows, 1))
    cos_r, sin_r = by_row(jnp.cos(ang_r)), by_row(jnp.sin(ang_r))
    cos_c, sin_c = by_col(jnp.cos(ang_c)), by_col(jnp.sin(ang_c))
    cos = jnp.concatenate([cos_r, cos_r, cos_c, cos_c], axis=1)
    sin = jnp.concatenate([-sin_r, sin_r, -sin_c, sin_c], axis=1)
    cos = jnp.concatenate([cos, cos], axis=1)
    sin = jnp.concatenate([sin, sin], axis=1)
    scale = HEAD_DIM ** -0.5 * LOG2E
    return cos * scale, sin * scale, cos, sin


def kernel(x, c, ctx, c_ctx, w_mod, b_mod, norm1_g, norm2_g, conv_w_pw1, conv_b_pw1, conv_w_dw, conv_b_dw,
           conv_ln_g, conv_ln_b, conv_w_pw2, conv_b_pw2, attn_w_qkv, attn_w_o, attn_sink, router_w, router_b,
           moe_w_gate, moe_w_up, moe_w_down, final_g):
    bsz, s, d = x.shape
    lc = ctx.shape[1]
    ts = 512
    tm = 256
    assert d == D and w_mod.shape[0] == 2 and bsz + 1 <= COND_ROWS
    assert lc <= ts and lc % 256 == 0 and s % ts == 0 and s % (ATTN_QBLOCKS * QBLK) == 0

    cond = jnp.concatenate([c, c_ctx[None, :], jnp.zeros((COND_ROWS - bsz - 1, D), F32)], axis=0)
    mods = _adaln(cond, w_mod, b_mod).reshape(2, COND_ROWS, 6, D)
    mod_lat = [mods[l, :bsz] for l in range(2)]
    mod_ctx = [mods[l, bsz:bsz + 1] for l in range(2)]

    rwt_hi = router_w.T.astype(BF16)
    rwt = jnp.concatenate([rwt_hi, (router_w.T - rwt_hi.astype(F32)).astype(BF16)], axis=0)
    rb = router_b.reshape(N_EXPERTS, 1)
    row = lambda v: v.reshape(1, -1)
    experts_f32 = [moe_w_gate.reshape(-1, D_EXPERT), moe_w_up.reshape(-1, D_EXPERT), moe_w_down.reshape(-1, D)]
    as_experts = lambda ws: (ws[0].reshape(N_EXPERTS, D, D_EXPERT), ws[1].reshape(N_EXPERTS, D, D_EXPERT),
                             ws[2].reshape(N_EXPERTS, D_EXPERT, D))

    x1, ctx1, hrow, cls, *w_l0 = _conv_layer(
        x, ctx, mod_lat[0], mod_ctx[0], row(norm1_g[0]), row(norm2_g[0]), conv_w_pw1[0].astype(BF16),
        row(conv_b_pw1[0]), conv_w_dw[0], row(conv_b_dw[0]), row(conv_ln_g[0]), row(conv_ln_b[0]),
        conv_w_pw2[0].astype(BF16), row(conv_b_pw2[0]), rwt, rb, experts_f32, ts=ts, cast_part=(0, 2))
    rank, counts = _plan(cls)
    pos, src, ea, eb, nused = _dispatch_tables(cls, rank, counts, tm=tm, n_pad=bsz * (ts - lc))
    ys = _moe(hrow, src, ea, eb, nused, *as_experts(w_l0), tm=tm)

    wq = attn_w_qkv[0][:, :D]
    wk = attn_w_qkv[0][:, D:D + KV_DIM].reshape(D, N_KV_HEADS, 1, HEAD_DIM)
    wv = attn_w_qkv[0][:, D + KV_DIM:].reshape(D, N_KV_HEADS, 1, HEAD_DIM)
    dup = lambda w: jnp.broadcast_to(w, (D, N_KV_HEADS, 2, HEAD_DIM)).reshape(D, 2 * KV_DIM)
    w_ext = jnp.concatenate([wq, dup(wk), dup(wv)], axis=1).astype(BF16)
    cq, sq, ck, sk = _rope_tables(s)
    ones = jnp.ones((lc, LANES), F32)
    zeros = jnp.zeros((lc, LANES), F32)
    n1g = row(norm1_g[1])
    x2, q, kd, vd = _qkv(pos, x1, ys, mod_lat[0], mod_lat[1], n1g, w_ext, cq, sq, ck, sk,
                         ts=ts, tok_stride=s + ts, tok_off=0)
    _, _, kxd, vxd = _qkv(pos, ctx1, ys, mod_ctx[0], mod_ctx[1], n1g, w_ext, ones, zeros, ones, zeros,
                          ts=lc, tok_stride=s + ts, tok_off=s)
    attn, *w_l1 = _attention(q, kd, vd, kxd, vxd, attn_sink[0].reshape(1, N_HEADS), experts_f32, (1, 2))
    x3, hrow1, cls1 = _attn_out(attn, x2, mod_lat[1], row(norm2_g[1]), attn_w_o[0].astype(BF16), rwt, rb, ts=ts)
    rank1, counts1 = _plan(cls1)
    pos1, src1, ea1, eb1, nused1 = _dispatch_tables(cls1, rank1, counts1, tm=tm)
    ys1 = _moe(hrow1, src1, ea1, eb1, nused1, *as_experts(w_l1), tm=tm)
    return _final(pos1, x3, ys1, mod_lat[1], row(final_g), ts=ts)
```
